```python
import jax, jax.numpy as jnp
from jax import lax
import numpy as np

D_MODEL = 1024
BATCH = 2
SEQ = 8192
DEPTH = 2

HEAD_DIM = 64
N_HEADS_MOBA = D_MODEL // 128
N_HEADS_SB = D_MODEL // 128
WIDTH_MOBA = N_HEADS_MOBA * HEAD_DIM
WIDTH_SB = N_HEADS_SB * HEAD_DIM
WIDTH_CONV = D_MODEL // 2
CONV_K = 3
MOBA_BLOCK = 256
MOBA_TOPK = 3
Q_BLOCK = 128
D_FF = -(-8 * D_MODEL // (3 * 256)) * 256
RMS_EPS = 1e-6
IN_SPLITS = (WIDTH_MOBA, WIDTH_MOBA, WIDTH_MOBA, WIDTH_SB, WIDTH_SB, WIDTH_SB,
             WIDTH_CONV, WIDTH_CONV, WIDTH_CONV, 3 * D_MODEL)
IN_WIDTH = sum(IN_SPLITS)

kernel_name = "hybrid_moba_stickbreak_shortconv_block"


def rms_norm(x, g):
    xf = x.astype(jnp.float32)
    y = xf * lax.rsqrt(jnp.mean(xf * xf, axis=-1, keepdims=True) + RMS_EPS)
    return (y * g.astype(jnp.float32)).astype(x.dtype)


def alibi_slopes(n_heads):
    return jnp.exp2(-8.0 * (jnp.arange(n_heads, dtype=jnp.float32) + 1.0) / n_heads)


def moba_attention(q, k, v):
    B, S, H, d = q.shape
    nq = S // Q_BLOCK
    nb = -(-S // MOBA_BLOCK)
    s_pad = nb * MOBA_BLOCK
    topk = min(MOBA_TOPK, nb)
    scale = d ** -0.5
    slopes = alibi_slopes(H)
    pad = ((0, 0), (0, 0), (0, s_pad - S), (0, 0))
    kt = jnp.pad(k.transpose(0, 2, 1, 3), pad)
    vt = jnp.pad(v.transpose(0, 2, 1, 3), pad)
    kb = kt.reshape(B, H, nb, MOBA_BLOCK, d)
    vb = vt.reshape(B, H, nb, MOBA_BLOCK, d)
    kmean = jnp.mean(kb.astype(jnp.float32), axis=3).astype(k.dtype)
    qc_all = q.reshape(B, nq, Q_BLOCK, H, d).transpose(1, 0, 3, 2, 4)
    bi = jnp.arange(B)[:, None, None, None]
    hi = jnp.arange(H)[None, :, None, None]
    offs = jnp.arange(MOBA_BLOCK)
    blk_ids = jnp.arange(nb)

    def chunk(args):
        c, qc = args
        t = c * Q_BLOCK + jnp.arange(Q_BLOCK)
        blk = (c * Q_BLOCK) // MOBA_BLOCK
        gate = jnp.einsum('bhqd,bhnd->bhqn', qc, kmean).astype(jnp.float32)
        gate = jnp.where(blk_ids < blk, gate, -jnp.inf)
        _, idx = lax.top_k(gate, topk)
        kg = kb[bi, hi, idx]
        vg = vb[bi, hi, idx]
        s_sel = jnp.einsum('bhqd,bhqrld->bhqrl', qc, kg).astype(jnp.float32) * scale
        pos_sel = idx[..., None] * MOBA_BLOCK + offs
        dist_sel = (t[None, None, :, None, None] - pos_sel).astype(jnp.float32)
        s_sel = s_sel - slopes[None, :, None, None, None] * dist_sel
        valid = jnp.arange(topk) < blk
        s_sel = jnp.where(valid[:, None], s_sel, -jnp.inf)
        ko = lax.dynamic_index_in_dim(kb, blk, axis=2, keepdims=False)
        vo = lax.dynamic_index_in_dim(vb, blk, axis=2, keepdims=False)
        s_own = jnp.einsum('bhqd,bhld->bhql', qc, ko).astype(jnp.float32) * scale
        dist_own = t[:, None] - (blk * MOBA_BLOCK + offs)[None, :]
        s_own = s_own - slopes[None, :, None, None] * dist_own.astype(jnp.float32)
        s_own = jnp.where(dist_own >= 0, s_own, -jnp.inf)
        n_sel = topk * MOBA_BLOCK
        scores = jnp.concatenate([s_sel.reshape(B, H, Q_BLOCK, n_sel), s_own], axis=-1)
        p = jax.nn.softmax(scores, axis=-1).astype(v.dtype)
        p_sel = p[..., :n_sel].reshape(B, H, Q_BLOCK, topk, MOBA_BLOCK)
        p_own = p[..., n_sel:]
        return (jnp.einsum('bhqrl,bhqrld->bhqd', p_sel, vg)
                + jnp.einsum('bhql,bhld->bhqd', p_own, vo))

    out = lax.map(chunk, (jnp.arange(nq), qc_all))
    return out.transpose(1, 0, 3, 2, 4).reshape(B, S, H * d)


def stick_breaking_attention(q, k, v):
    B, S, H, d = q.shape
    nq = S // Q_BLOCK
    scale = d ** -0.5
    kt = k.transpose(0, 2, 1, 3)
    vt = v.transpose(0, 2, 1, 3)
    qc_all = q.reshape(B, nq, Q_BLOCK, H, d).transpose(1, 0, 3, 2, 4)
    s_pos = jnp.arange(S)

    def chunk(args):
        c, qc = args
        t = c * Q_BLOCK + jnp.arange(Q_BLOCK)
        z = jnp.einsum('bhqd,bhsd->bhqs', qc, kt).astype(jnp.float32) * scale
        strict = s_pos[None, :] < t[:, None]
        log_not = jnp.where(strict, jax.nn.log_sigmoid(-z), 0.0)
        after = lax.cumsum(log_not, axis=3, reverse=True) - log_not
        w = jnp.where(strict, jnp.exp(jax.nn.log_sigmoid(z) + after), 0.0)
        return jnp.einsum('bhqs,bhsd->bhqd', w.astype(v.dtype), vt)

    out = lax.map(chunk, (jnp.arange(nq), qc_all))
    return out.transpose(1, 0, 3, 2, 4).reshape(B, S, H * d)


def short_conv_mixer(xc, bg, cg, conv_w):
    u = cg * xc
    S = u.shape[1]
    up = jnp.pad(u, ((0, 0), (CONV_K - 1, 0), (0, 0)))
    y = up[:, 0:S] * conv_w[0]
    for j in range(1, CONV_K):
        y = y + up[:, j:j + S] * conv_w[j]
    return bg * y


def hybrid_layer(x, g_mix, w_in, b_gate, conv_w, w_proj_moba, w_proj_sb, w_proj_conv,
                 w_out, g_ffn, w_ffn_gate, w_ffn_up, w_ffn_down):
    B, S, _ = x.shape
    h = rms_norm(x, g_mix)
    proj = jnp.einsum('bsd,de->bse', h, w_in)
    offsets = np.cumsum(IN_SPLITS)[:-1].tolist()
    qa, ka, va, qs, ks, vs, xc, bc, cc, gates = jnp.split(proj, offsets, axis=-1)
    o_a = moba_attention(qa.reshape(B, S, N_HEADS_MOBA, HEAD_DIM),
                         ka.reshape(B, S, N_HEADS_MOBA, HEAD_DIM),
                         va.reshape(B, S, N_HEADS_MOBA, HEAD_DIM))
    o_b = stick_breaking_attention(qs.reshape(B, S, N_HEADS_SB, HEAD_DIM),
                                   ks.reshape(B, S, N_HEADS_SB, HEAD_DIM),
                                   vs.reshape(B, S, N_HEADS_SB, HEAD_DIM))
    o_c = short_conv_mixer(xc, bc, cc, conv_w)
    g = jax.nn.sigmoid((gates + b_gate).astype(jnp.float32)).astype(x.dtype)
    g_a, g_b, g_c = jnp.split(g, 3, axis=-1)
    merged = (g_a * (o_a @ w_proj_moba) + g_b * (o_b @ w_proj_sb)
              + g_c * (o_c @ w_proj_conv))
    x = x + merged @ w_out
    h2 = rms_norm(x, g_ffn)
    x = x + (jax.nn.silu(h2 @ w_ffn_gate) * (h2 @ w_ffn_up)) @ w_ffn_down
    return x


def setup_inputs(seed: int = 0) -> dict:
    key = jax.random.key(seed)
    ks = jax.random.split(key, 16)

    def nrm(k, shape, fan_in):
        return jax.random.normal(k, shape, jnp.float32) * (fan_in ** -0.5)

    def gain(k, shape):
        return 1.0 + 0.02 * jax.random.normal(k, shape, jnp.float32)

    return {
        "x": jax.random.normal(ks[0], (BATCH, SEQ, D_MODEL), jnp.float32),
        "norm_mix_g": gain(ks[1], (DEPTH, D_MODEL)),
        "w_in": nrm(ks[2], (DEPTH, D_MODEL, IN_WIDTH), D_MODEL),
        "b_gate": 0.02 * jax.random.normal(ks[3], (DEPTH, 3 * D_MODEL), jnp.float32),
        "conv_w": nrm(ks[4], (DEPTH, CONV_K, WIDTH_CONV), CONV_K),
        "w_proj_moba": nrm(ks[5], (DEPTH, WIDTH_MOBA, D_MODEL), WIDTH_MOBA),
        "w_proj_sb": nrm(ks[6], (DEPTH, WIDTH_SB, D_MODEL), WIDTH_SB),
        "w_proj_conv": nrm(ks[7], (DEPTH, WIDTH_CONV, D_MODEL), WIDTH_CONV),
        "w_out": nrm(ks[8], (DEPTH, D_MODEL, D_MODEL), D_MODEL),
        "norm_ffn_g": gain(ks[9], (DEPTH, D_MODEL)),
        "w_ffn_gate": nrm(ks[10], (DEPTH, D_MODEL, D_FF), D_MODEL),
        "w_ffn_up": nrm(ks[11], (DEPTH, D_MODEL, D_FF), D_MODEL),
        "w_ffn_down": nrm(ks[12], (DEPTH, D_FF, D_MODEL), D_FF),
        "norm_final_g": gain(ks[13], (D_MODEL,)),
    }


def reference(x, norm_mix_g, w_in, b_gate, conv_w, w_proj_moba, w_proj_sb, w_proj_conv,
              w_out, norm_ffn_g, w_ffn_gate, w_ffn_up, w_ffn_down, norm_final_g):
    for l in range(DEPTH):
        x = hybrid_layer(x, norm_mix_g[l], w_in[l], b_gate[l], conv_w[l],
                         w_proj_moba[l], w_proj_sb[l], w_proj_conv[l], w_out[l],
                         norm_ffn_g[l], w_ffn_gate[l], w_ffn_up[l], w_ffn_down[l])
    return rms_norm(x, norm_final_g)
```

```python
import functools

import jax
import jax.numpy as jnp
import numpy as np
from jax import lax
from jax.experimental import pallas as pl
from jax.experimental.pallas import tpu as pltpu

D_MODEL = 1024
HEAD_DIM = 64
N_HEADS = 8
WIDTH = N_HEADS * HEAD_DIM
CONV_K = 3
KV_BLOCK = 256
Q_BLOCK = 128
MOBA_TOPK = 3
D_FF = 2816
RMS_EPS = 1e-6
REST_WIDTH = 3 * WIDTH + 3 * D_MODEL
PAIR = 2 * HEAD_DIM
SUBLANES = 8
RUN = KV_BLOCK // SUBLANES
SB_STOP = -150.0
VMEM_LIMIT = 56 * 1024 * 1024

_NT = (((1,), (1,)), ((), ()))
_F32 = jnp.float32
_BF16 = jnp.bfloat16


def _params(*sem):
    return pltpu.CompilerParams(dimension_semantics=sem, vmem_limit_bytes=VMEM_LIMIT)


def _const_spec(shape):
    zeros = (0,) * len(shape)
    return pl.BlockSpec(shape, lambda *_: zeros)


def _rms(x, g):
    y = x * lax.rsqrt(jnp.mean(x * x, axis=-1, keepdims=True) + RMS_EPS)
    return y * g


def _in_proj_kernel(x_ref, g_ref, perm_ref, wqa_ref, wka_ref, wva_ref, wqs_ref, wks_ref,
                    wvs_ref, wrest_ref,
                    qa_ref, ka_ref, va_ref, kmean_ref, qs_ref, ks_ref, vs_ref, rest_ref):
    i = pl.program_id(1)
    h = _rms(x_ref[0], g_ref[...]).astype(_BF16)
    qa_ref[0] = lax.dot_general(wqa_ref[...], h, _NT,
                                preferred_element_type=_F32).astype(_BF16)
    ka = jnp.dot(h, wka_ref[...], preferred_element_type=_F32)
    ka_ref[0, 0] = ka.astype(_BF16)
    kmean_ref[0, pl.ds(i, 1), :] = jnp.mean(ka, axis=0, keepdims=True)
    va_ref[0, 0] = lax.dot_general(wva_ref[...], h, _NT,
                                   preferred_element_type=_F32).astype(_BF16)
    qs_ref[0] = lax.dot_general(wqs_ref[...], h, _NT,
                                preferred_element_type=_F32).astype(_BF16)
    hp = jnp.dot(perm_ref[...], h, preferred_element_type=_F32).astype(_BF16)
    ks_ref[0, 0] = jnp.dot(hp, wks_ref[...], preferred_element_type=_F32).astype(_BF16)
    vs_ref[0, 0] = lax.dot_general(wvs_ref[...], hp, _NT,
                                   preferred_element_type=_F32).astype(_BF16)
    rest_ref[0] = jnp.dot(h, wrest_ref[...], preferred_element_type=_F32).astype(_BF16)


def _in_proj(x, g, perm, wqa, wka, wva, wqs, wks, wvs, wrest):
    b, s, d = x.shape
    nb = s // KV_BLOCK
    t = KV_BLOCK
    out_shape = (
        jax.ShapeDtypeStruct((b, WIDTH, s), _BF16),
        jax.ShapeDtypeStruct((b, nb, t, WIDTH), _BF16),
        jax.ShapeDtypeStruct((b, nb, WIDTH, t), _BF16),
        jax.ShapeDtypeStruct((b, nb, WIDTH), _F32),
        jax.ShapeDtypeStruct((b, WIDTH, s), _BF16),
        jax.ShapeDtypeStruct((b, nb, t, WIDTH), _BF16),
        jax.ShapeDtypeStruct((b, nb, WIDTH, t), _BF16),
        jax.ShapeDtypeStruct((b, s, REST_WIDTH), _BF16),
    )
    qt_spec = pl.BlockSpec((1, WIDTH, t), lambda bi, i: (bi, 0, i))
    k_spec = pl.BlockSpec((1, 1, t, WIDTH), lambda bi, i: (bi, i, 0, 0))
    vt_spec = pl.BlockSpec((1, 1, WIDTH, t), lambda bi, i: (bi, i, 0, 0))
    return pl.pallas_call(
        _in_proj_kernel,
        grid=(b, nb),
        in_specs=[
            pl.BlockSpec((1, t, d), lambda bi, i: (bi, i, 0)),
            _const_spec((1, d)),
            _const_spec((t, t)),
            _const_spec((WIDTH, d)), _const_spec((d, WIDTH)), _const_spec((WIDTH, d)),
            _const_spec((WIDTH, d)), _const_spec((d, WIDTH)), _const_spec((WIDTH, d)),
            _const_spec((d, REST_WIDTH)),
        ],
        out_specs=(
            qt_spec, k_spec, vt_spec,
            pl.BlockSpec((1, nb, WIDTH), lambda bi, i: (bi, 0, 0)),
            qt_spec, k_spec, vt_spec,
            pl.BlockSpec((1, t, REST_WIDTH), lambda bi, i: (bi, i, 0)),
        ),
        out_shape=out_shape,
        compiler_params=_params("arbitrary", "arbitrary"),
        name="in_proj",
    )(x, g, perm, wqa, wka, wva, wqs, wks, wvs, wrest)


def _head_rows(q_pair, hh):
    row = lax.broadcasted_iota(jnp.int32, q_pair.shape, 0)
    keep = (row >= hh * HEAD_DIM) & (row < (hh + 1) * HEAD_DIM)
    return jnp.where(keep, q_pair, jnp.zeros_like(q_pair))


def _moba_kernel(q_ref, k_ref, v_ref, kmean_ref, alibi_ref, o_ref, sel_ref):
    c = pl.program_id(2)
    blk = c // (KV_BLOCK // Q_BLOCK)
    nb = k_ref.shape[1]
    q_pair = q_ref[0]
    km = kmean_ref[0].astype(_BF16)
    nidx = lax.broadcasted_iota(jnp.int32, (nb, Q_BLOCK), 0)
    kpos = lax.broadcasted_iota(jnp.int32, (KV_BLOCK, Q_BLOCK), 0)
    qpos = lax.broadcasted_iota(jnp.int32, (KV_BLOCK, Q_BLOCK), 1)
    causal = kpos <= qpos + (c * Q_BLOCK - blk * KV_BLOCK)
    neg_inf = jnp.float32(-jnp.inf)

    heads = []
    for hh in range(2):
        qm = _head_rows(q_pair, hh)
        gate = jnp.dot(km, qm, preferred_element_type=_F32)
        g = jnp.where(nidx < blk, gate, neg_inf)
        sel = jnp.zeros((nb, Q_BLOCK), jnp.bool_)
        for r in range(MOBA_TOPK):
            mx = jnp.max(g, axis=0, keepdims=True)
            first = jnp.min(jnp.where(g == mx, nidx, nb), axis=0, keepdims=True)
            hit = nidx == first
            sel = sel | (hit & (jnp.full((nb, Q_BLOCK), r, jnp.int32) < blk))
            g = jnp.where(hit, neg_inf, g)
        sel_ref[hh] = jnp.where(sel, 0.0, neg_inf)
        heads.append(qm)

    slope = [alibi_ref[hh, KV_BLOCK:KV_BLOCK + 1, :] for hh in range(2)]
    ramp = [alibi_ref[hh, 0:KV_BLOCK, :] for hh in range(2)]

    def shift(j):
        return jnp.full((1, Q_BLOCK), j * KV_BLOCK - c * Q_BLOCK, jnp.int32).astype(_F32)

    state = []
    k_own = k_ref[0, blk]
    v_own = v_ref[0, blk]
    for hh in range(2):
        s = jnp.dot(k_own, heads[hh], preferred_element_type=_F32) + ramp[hh]
        s = jnp.where(causal, s, neg_inf)
        cj = slope[hh] * shift(blk)
        m = jnp.max(s, axis=0, keepdims=True) + cj
        p = jnp.exp(s - (m - cj))
        l = jnp.sum(p, axis=0, keepdims=True)
        acc = jnp.dot(v_own[hh * HEAD_DIM:(hh + 1) * HEAD_DIM, :], p.astype(_BF16),
                      preferred_element_type=_F32)
        state += [m, l, acc]

    def body(j, st):
        k_j = k_ref[0, j]
        v_j = v_ref[0, j]
        out = []
        for hh in range(2):
            m, l, acc = st[3 * hh:3 * hh + 3]
            s = jnp.dot(k_j, heads[hh], preferred_element_type=_F32) + ramp[hh]
            s = s + sel_ref[hh, pl.ds(j, 1), :]
            cj = slope[hh] * shift(j)
            m_new = jnp.maximum(m, jnp.max(s, axis=0, keepdims=True) + cj)
            alpha = jnp.exp(m - m_new)
            p = jnp.exp(s - (m_new - cj))
            l = alpha * l + jnp.sum(p, axis=0, keepdims=True)
            acc = alpha * acc + jnp.dot(v_j[hh * HEAD_DIM:(hh + 1) * HEAD_DIM, :],
                                        p.astype(_BF16), preferred_element_type=_F32)
            out += [m_new, l, acc]
        return tuple(out)

    st = lax.fori_loop(0, blk, body, tuple(state))
    o_t = jnp.concatenate([st[2] / st[1], st[5] / st[4]], axis=0)
    o_ref[0] = o_t.T.astype(_BF16)


def _moba(qa_t, ka, va_t, kmean, alibi):
    b, _, s = qa_t.shape
    nb = s // KV_BLOCK
    npair = N_HEADS // 2
    return pl.pallas_call(
        _moba_kernel,
        grid=(b, npair, s // Q_BLOCK),
        in_specs=[
            pl.BlockSpec((1, PAIR, Q_BLOCK), lambda bi, hp, c: (bi, hp, c)),
            pl.BlockSpec((1, nb, KV_BLOCK, PAIR), lambda bi, hp, c: (bi, 0, 0, hp)),
            pl.BlockSpec((1, nb, PAIR, KV_BLOCK), lambda bi, hp, c: (bi, 0, hp, 0)),
            pl.BlockSpec((1, nb, PAIR), lambda bi, hp, c: (bi, 0, hp)),
            pl.BlockSpec((2, KV_BLOCK + SUBLANES, Q_BLOCK), lambda bi, hp, c: (hp, 0, 0)),
        ],
        out_specs=pl.BlockSpec((1, Q_BLOCK, PAIR), lambda bi, hp, c: (bi, c, hp)),
        out_shape=jax.ShapeDtypeStruct((b, s, WIDTH), _BF16),
        scratch_shapes=[pltpu.VMEM((2, nb, Q_BLOCK), _F32)],
        compiler_params=_params("arbitrary", "arbitrary", "arbitrary"),
        name="moba",
    )(qa_t, ka, va_t, kmean, alibi)


def _suffix_over_sublanes(x):
    sub = lax.broadcasted_iota(jnp.int32, x.shape, 0)
    y = x
    for d in (1, 2, 4):
        up = pltpu.roll(y, SUBLANES - d, 0)
        y = y + jnp.where(sub < SUBLANES - d, up, 0.0)
    return y


def _sb_block(k_j, v_jh, qm, carry, acc, mask):
    z = jnp.dot(k_j, qm, preferred_element_type=_F32)
    nz = -z
    log_not = jnp.minimum(nz, 0.0) - jnp.log(1.0 + jnp.exp(jnp.minimum(z, nz)))
    if mask is not None:
        log_not = jnp.where(mask, log_not, 0.0)
    run = jnp.zeros((SUBLANES, Q_BLOCK), _F32)
    partial = [None] * RUN
    for r in reversed(range(RUN)):
        run = run + log_not[r * SUBLANES:(r + 1) * SUBLANES, :]
        partial[r] = run
    incl = _suffix_over_sublanes(run)
    base = (incl - run) + carry
    w = [jnp.exp(z[r * SUBLANES:(r + 1) * SUBLANES, :] + (partial[r] + base))
         for r in range(RUN)]
    w = jnp.concatenate(w, axis=0)
    if mask is not None:
        w = jnp.where(mask, w, 0.0)
    acc = acc + jnp.dot(v_jh, w.astype(_BF16), preferred_element_type=_F32)
    return carry + incl[0:1, :], acc


def _sb_kernel(q_ref, k_ref, v_ref, o_ref):
    i = pl.program_id(2)
    jd = i // (KV_BLOCK // Q_BLOCK)
    q_pair = q_ref[0]
    rho = lax.broadcasted_iota(jnp.int32, (KV_BLOCK, Q_BLOCK), 0)
    kpos = (rho & (SUBLANES - 1)) * RUN + (rho >> 3)
    qpos = lax.broadcasted_iota(jnp.int32, (KV_BLOCK, Q_BLOCK), 1)
    strict = kpos < qpos + (i * Q_BLOCK - jd * KV_BLOCK)

    outs = []
    for hh in range(2):
        qm = _head_rows(q_pair, hh)
        rows = slice(hh * HEAD_DIM, (hh + 1) * HEAD_DIM)
        carry, acc = _sb_block(k_ref[0, jd], v_ref[0, jd, rows, :], qm,
                               jnp.zeros((1, Q_BLOCK), _F32),
                               jnp.zeros((HEAD_DIM, Q_BLOCK), _F32), strict)

        def cond(st):
            j, carry, _ = st
            return jnp.logical_and(j >= 0, jnp.max(carry) > SB_STOP)

        def body(st, qm=qm, rows=rows):
            j, carry, acc = st
            carry, acc = _sb_block(k_ref[0, j], v_ref[0, j, rows, :], qm, carry, acc, None)
            return j - 1, carry, acc

        _, _, acc = lax.while_loop(cond, body, (jd - 1, carry, acc))
        outs.append(acc)
    o_ref[0] = jnp.concatenate(outs, axis=0).T.astype(_BF16)


def _stick_breaking(qs_t, ks, vs_t):
    b, _, s = qs_t.shape
    nb = s // KV_BLOCK
    npair = N_HEADS // 2
    return pl.pallas_call(
        _sb_kernel,
        grid=(b, npair, s // Q_BLOCK),
        in_specs=[
            pl.BlockSpec((1, PAIR, Q_BLOCK), lambda bi, hp, i: (bi, hp, i)),
            pl.BlockSpec((1, nb, KV_BLOCK, PAIR), lambda bi, hp, i: (bi, 0, 0, hp)),
            pl.BlockSpec((1, nb, PAIR, KV_BLOCK), lambda bi, hp, i: (bi, 0, hp, 0)),
        ],
        out_specs=pl.BlockSpec((1, Q_BLOCK, PAIR), lambda bi, hp, i: (bi, i, hp)),
        out_shape=jax.ShapeDtypeStruct((b, s, WIDTH), _BF16),
        compiler_params=_params("arbitrary", "arbitrary", "arbitrary"),
        name="stick_breaking",
    )(qs_t, ks, vs_t)


MIX_T = 512
HALO = 16


def _mix_kernel(oa_ref, ob_ref, rest_ref, halo_ref, x_ref, convw_ref, bg_ref,
                wpa_ref, wpb_ref, wpc_ref, wout_ref, o_ref):
    i = pl.program_id(1)
    r = rest_ref[0]
    xc = r[:, 0:WIDTH].astype(_F32)
    bc = r[:, WIDTH:2 * WIDTH].astype(_F32)
    cc = r[:, 2 * WIDTH:3 * WIDTH].astype(_F32)
    u = cc * xc
    hl = halo_ref[0]
    uh = hl[:, 2 * WIDTH:3 * WIDTH].astype(_F32) * hl[:, 0:WIDTH].astype(_F32)
    uh = jnp.where(i > 0, uh, 0.0)
    prev1 = uh[HALO - 1:HALO, :]
    prev2 = uh[HALO - 2:HALO - 1, :]
    row = lax.broadcasted_iota(jnp.int32, u.shape, 0)
    u1 = jnp.where(row == 0, prev1, pltpu.roll(u, 1, 0))
    u2 = jnp.where(row == 0, prev2, jnp.where(row == 1, prev1, pltpu.roll(u, 2, 0)))
    w = convw_ref[...]
    y = bc * (u2 * w[0:1, :] + u1 * w[1:2, :] + u * w[2:3, :])
    g = jax.nn.sigmoid(r[:, 3 * WIDTH:].astype(_F32) + bg_ref[...])
    d = D_MODEL
    merged = (g[:, 0:d] * jnp.dot(oa_ref[0], wpa_ref[...], preferred_element_type=_F32)
              + g[:, d:2 * d] * jnp.dot(ob_ref[0], wpb_ref[...], preferred_element_type=_F32)
              + g[:, 2 * d:] * jnp.dot(y.astype(_BF16), wpc_ref[...],
                                       preferred_element_type=_F32))
    o_ref[0] = x_ref[0] + jnp.dot(merged.astype(_BF16), wout_ref[...],
                                  preferred_element_type=_F32)


def _mix(oa, ob, rest, x, convw, bg, wpa, wpb, wpc, wout):
    b, s, d = x.shape
    t = MIX_T
    tile = lambda width: pl.BlockSpec((1, t, width), lambda bi, i: (bi, i, 0))
    halo = pl.BlockSpec((1, HALO, REST_WIDTH),
                        lambda bi, i: (bi, jnp.maximum(i * (t // HALO) - 1, 0), 0))
    return pl.pallas_call(
        _mix_kernel,
        grid=(b, s // t),
        in_specs=[tile(WIDTH), tile(WIDTH), tile(REST_WIDTH), halo, tile(d),
                  _const_spec((CONV_K, WIDTH)), _const_spec((1, 3 * d)),
                  _const_spec((WIDTH, d)), _const_spec((WIDTH, d)), _const_spec((WIDTH, d)),
                  _const_spec((d, d))],
        out_specs=tile(d),
        out_shape=jax.ShapeDtypeStruct((b, s, d), _F32),
        compiler_params=_params("arbitrary", "arbitrary"),
        name="mix",
    )(oa, ob, rest, rest, x, convw, bg, wpa, wpb, wpc, wout)


FFN_T = 512
FF_CHUNK = 256


def _ffn_up_kernel(x_ref, g_ref, wg_ref, wu_ref, o_ref):
    h = _rms(x_ref[...], g_ref[...]).astype(_BF16)
    for n in range(D_FF // FF_CHUNK):
        cols = slice(n * FF_CHUNK, (n + 1) * FF_CHUNK)
        a = jnp.dot(h, wg_ref[:, cols], preferred_element_type=_F32)
        bgate = jnp.dot(h, wu_ref[:, cols], preferred_element_type=_F32)
        o_ref[:, cols] = (a * jax.nn.sigmoid(a) * bgate).astype(_BF16)


def _ffn_down_kernel(a_ref, x_ref, wd_ref, g_ref, o_ref, *, final_norm):
    y = x_ref[...] + jnp.dot(a_ref[...], wd_ref[...], preferred_element_type=_F32)
    if final_norm:
        y = _rms(y, g_ref[...])
    o_ref[...] = y


def _ffn(x2d, g, wg, wu, wd, gfinal, final_norm):
    n, d = x2d.shape
    t = FFN_T
    act = pl.pallas_call(
        _ffn_up_kernel,
        grid=(n // t,),
        in_specs=[pl.BlockSpec((t, d), lambda i: (i, 0)), _const_spec((1, d)),
                  _const_spec((d, D_FF)), _const_spec((d, D_FF))],
        out_specs=pl.BlockSpec((t, D_FF), lambda i: (i, 0)),
        out_shape=jax.ShapeDtypeStruct((n, D_FF), _BF16),
        compiler_params=_params("arbitrary"),
        name="ffn_up",
    )(x2d, g, wg, wu)
    return pl.pallas_call(
        functools.partial(_ffn_down_kernel, final_norm=final_norm),
        grid=(n // t,),
        in_specs=[pl.BlockSpec((t, D_FF), lambda i: (i, 0)),
                  pl.BlockSpec((t, d), lambda i: (i, 0)),
                  _const_spec((D_FF, d)), _const_spec((1, d))],
        out_specs=pl.BlockSpec((t, d), lambda i: (i, 0)),
        out_shape=jax.ShapeDtypeStruct((n, d), _F32),
        compiler_params=_params("arbitrary"),
        name="ffn_down",
    )(act, x2d, wd, gfinal)


def _sb_permutation():
    rho = np.arange(KV_BLOCK)
    kappa = (rho % SUBLANES) * RUN + rho // SUBLANES
    p = np.zeros((KV_BLOCK, KV_BLOCK), np.float32)
    p[rho, kappa] = 1.0
    return jnp.asarray(p, _BF16)


def _alibi_table():
    slopes = np.exp2(-8.0 * (np.arange(N_HEADS, dtype=np.float32) + 1.0) / N_HEADS)
    rows = np.concatenate([np.arange(KV_BLOCK, dtype=np.float32), np.ones(SUBLANES, np.float32)])
    tab = slopes[:, None, None] * rows[None, :, None] * np.ones((1, 1, Q_BLOCK), np.float32)
    return jnp.asarray(tab.astype(np.float32))


def kernel(x, norm_mix_g, w_in, b_gate, conv_w, w_proj_moba, w_proj_sb, w_proj_conv, w_out,
           norm_ffn_g, w_ffn_gate, w_ffn_up, w_ffn_down, norm_final_g):
    depth = w_in.shape[0]
    b, s, d = x.shape
    scale = HEAD_DIM ** -0.5
    perm = _sb_permutation()
    alibi = _alibi_table()
    w = WIDTH
    for l in range(depth):
        wl = w_in[l]
        cols = lambda a, n=w: wl[:, a:a + n].astype(_BF16)
        wqa = (wl[:, 0:w] * scale).T.astype(_BF16)
        wka = cols(w)
        wva = wl[:, 2 * w:3 * w].T.astype(_BF16)
        wqs = (wl[:, 3 * w:4 * w] * scale).T.astype(_BF16)
        wks = cols(4 * w)
        wvs = wl[:, 5 * w:6 * w].T.astype(_BF16)
        wrest = cols(6 * w, REST_WIDTH)
        qa_t, ka, va_t, kmean, qs_t, ks, vs_t, rest = _in_proj(
            x, norm_mix_g[l][None, :], perm, wqa, wka, wva, wqs, wks, wvs, wrest)
        oa = _moba(qa_t, ka, va_t, kmean, alibi)
        ob = _stick_breaking(qs_t, ks, vs_t)
        x = _mix(oa, ob, rest, x, conv_w[l], b_gate[l][None, :],
                 w_proj_moba[l].astype(_BF16), w_proj_sb[l].astype(_BF16),
                 w_proj_conv[l].astype(_BF16), w_out[l].astype(_BF16))
        x = _ffn(x.reshape(b * s, d), norm_ffn_g[l][None, :], w_ffn_gate[l].astype(_BF16),
                 w_ffn_up[l].astype(_BF16), w_ffn_down[l].astype(_BF16),
                 norm_final_g[None, :], final_norm=(l == depth - 1)).reshape(b, s, d)
    return x
```

```python
import functools

import jax
import jax.numpy as jnp
import numpy as np
from jax import lax
from jax.experimental import pallas as pl
from jax.experimental.pallas import tpu as pltpu

D_MODEL = 1024
HEAD_DIM = 64
N_HEADS = 8
WIDTH = N_HEADS * HEAD_DIM
CONV_K = 3
KV_BLOCK = 256
MOBA_TOPK = 3
MOBA_Q = 256
MOBA_G = 2
SB_Q = KV_BLOCK
D_FF = 2816
RMS_EPS = 1e-6
REST_WIDTH = 3 * WIDTH + 3 * D_MODEL
PAIR = 2 * HEAD_DIM
SUBLANES = 8
LOG2E = 1.4426950408889634
RUN = KV_BLOCK // SUBLANES
SB_STOP = -150.0
VMEM_LIMIT = 56 * 1024 * 1024

_NT = (((1,), (1,)), ((), ()))
_F32 = jnp.float32
_BF16 = jnp.bfloat16


def _params(*sem):
    return pltpu.CompilerParams(dimension_semantics=sem, vmem_limit_bytes=VMEM_LIMIT)


def _const_spec(shape):
    zeros = (0,) * len(shape)
    return pl.BlockSpec(shape, lambda *_: zeros)


def _rms(x, g):
    y = x * lax.rsqrt(jnp.mean(x * x, axis=-1, keepdims=True) + RMS_EPS)
    return y * g


def _in_proj_kernel(x_ref, g_ref, perm_ref, wqa_ref, wka_ref, wva_ref, wqs_ref, wks_ref,
                    wvs_ref, wrest_ref,
                    qa_ref, ka_ref, va_ref, kmean_ref, qs_ref, ks_ref, vs_ref, rest_ref):
    i = pl.program_id(1)
    h = _rms(x_ref[0], g_ref[...]).astype(_BF16)
    qa_ref[0] = lax.dot_general(wqa_ref[...], h, _NT,
                                preferred_element_type=_F32).astype(_BF16)
    ka = jnp.dot(h, wka_ref[...], preferred_element_type=_F32)
    ka_ref[0, 0] = ka.astype(_BF16)
    kmean_ref[0, pl.ds(i, 1), :] = jnp.mean(ka, axis=0, keepdims=True)
    va_ref[0, 0] = lax.dot_general(wva_ref[...], h, _NT,
                                   preferred_element_type=_F32).astype(_BF16)
    qs_ref[0] = lax.dot_general(wqs_ref[...], h, _NT,
                                preferred_element_type=_F32).astype(_BF16)
    hp = jnp.dot(perm_ref[...], h, preferred_element_type=_F32).astype(_BF16)
    ks_ref[0, 0] = jnp.dot(hp, wks_ref[...], preferred_element_type=_F32).astype(_BF16)
    vs_ref[0, 0] = lax.dot_general(wvs_ref[...], hp, _NT,
                                   preferred_element_type=_F32).astype(_BF16)
    rest_ref[0] = jnp.dot(h, wrest_ref[...], preferred_element_type=_F32).astype(_BF16)


def _in_proj(x, g, perm, wqa, wka, wva, wqs, wks, wvs, wrest):
    b, s, d = x.shape
    nb = s // KV_BLOCK
    t = KV_BLOCK
    out_shape = (
        jax.ShapeDtypeStruct((b, WIDTH, s), _BF16),
        jax.ShapeDtypeStruct((b, nb, t, WIDTH), _BF16),
        jax.ShapeDtypeStruct((b, nb, WIDTH, t), _BF16),
        jax.ShapeDtypeStruct((b, nb, WIDTH), _F32),
        jax.ShapeDtypeStruct((b, WIDTH, s), _BF16),
        jax.ShapeDtypeStruct((b, nb, t, WIDTH), _BF16),
        jax.ShapeDtypeStruct((b, nb, WIDTH, t), _BF16),
        jax.ShapeDtypeStruct((b, s, REST_WIDTH), _BF16),
    )
    qt_spec = pl.BlockSpec((1, WIDTH, t), lambda bi, i: (bi, 0, i))
    k_spec = pl.BlockSpec((1, 1, t, WIDTH), lambda bi, i: (bi, i, 0, 0))
    vt_spec = pl.BlockSpec((1, 1, WIDTH, t), lambda bi, i: (bi, i, 0, 0))
    return pl.pallas_call(
        _in_proj_kernel,
        grid=(b, nb),
        in_specs=[
            pl.BlockSpec((1, t, d), lambda bi, i: (bi, i, 0)),
            _const_spec((1, d)),
            _const_spec((t, t)),
            _const_spec((WIDTH, d)), _const_spec((d, WIDTH)), _const_spec((WIDTH, d)),
            _const_spec((WIDTH, d)), _const_spec((d, WIDTH)), _const_spec((WIDTH, d)),
            _const_spec((d, REST_WIDTH)),
        ],
        out_specs=(
            qt_spec, k_spec, vt_spec,
            pl.BlockSpec((1, nb, WIDTH), lambda bi, i: (bi, 0, 0)),
            qt_spec, k_spec, vt_spec,
            pl.BlockSpec((1, t, REST_WIDTH), lambda bi, i: (bi, i, 0)),
        ),
        out_shape=out_shape,
        compiler_params=_params("arbitrary", "arbitrary"),
        name="in_proj",
    )(x, g, perm, wqa, wka, wva, wqs, wks, wvs, wrest)


def _head_rows(q_pair, hh):
    row = lax.broadcasted_iota(jnp.int32, q_pair.shape, 0)
    keep = (row >= hh * HEAD_DIM) & (row < (hh + 1) * HEAD_DIM)
    return jnp.where(keep, q_pair, jnp.zeros_like(q_pair))


def _moba_kernel(q_ref, k_ref, v_ref, kmean_ref, alibi_ref, o_ref, sel_ref, s_ref):
    blk = pl.program_id(2)
    nb = k_ref.shape[1]
    q_pair = q_ref[0]
    km = kmean_ref[0].astype(_BF16)
    nidx = lax.broadcasted_iota(jnp.int32, (nb, MOBA_Q), 0)
    kpos = lax.broadcasted_iota(jnp.int32, (KV_BLOCK, MOBA_Q), 0)
    qpos = lax.broadcasted_iota(jnp.int32, (KV_BLOCK, MOBA_Q), 1)
    causal = kpos <= qpos
    neg_inf = jnp.float32(-jnp.inf)

    heads = []
    for hh in range(2):
        qm = _head_rows(q_pair, hh)
        gate = jnp.dot(km, qm, preferred_element_type=_F32)
        g = jnp.where(nidx < blk, gate, neg_inf)
        sel = jnp.zeros((nb, MOBA_Q), jnp.bool_)
        for r in range(MOBA_TOPK):
            mx = jnp.max(g, axis=0, keepdims=True)
            first = jnp.min(jnp.where(g == mx, nidx, nb), axis=0, keepdims=True)
            hit = nidx == first
            sel = sel | (hit & (jnp.full((nb, MOBA_Q), r, jnp.int32) < blk))
            g = jnp.where(hit, neg_inf, g)
        sel_ref[hh] = jnp.where(sel, 0.0, neg_inf)
        heads.append(qm)

    slope = [alibi_ref[hh, KV_BLOCK:KV_BLOCK + 1, :] for hh in range(2)]
    ramp = [alibi_ref[hh, 0:KV_BLOCK, :] for hh in range(2)]

    def shift(j):
        return jnp.full((1, MOBA_Q), (j - blk) * KV_BLOCK, jnp.int32).astype(_F32)

    state = []
    k_own = k_ref[0, blk]
    for hh in range(2):
        s = jnp.dot(k_own, heads[hh], preferred_element_type=_F32) + ramp[hh]
        s = jnp.where(causal, s, neg_inf)
        m = jnp.max(s, axis=0, keepdims=True)
        p = jnp.exp2(s - m)
        l = jnp.sum(p, axis=0, keepdims=True)
        acc = jnp.dot(v_ref[0, blk, hh * HEAD_DIM:(hh + 1) * HEAD_DIM, :], p.astype(_BF16),
                      preferred_element_type=_F32)
        state += [m, l, acc]

    def row_bias(hh, j):
        return sel_ref[hh, pl.ds(j, 1), :] + slope[hh] * shift(j)

    def scores(t):
        slot = t % 2
        mx = []
        for hh in range(2):
            col = jnp.full((1, MOBA_Q), neg_inf, _F32)
            for g in range(MOBA_G):
                j = t * MOBA_G + g
                s = jnp.dot(k_ref[0, j], heads[hh], preferred_element_type=_F32) + ramp[hh]
                s_ref[slot, hh, g * KV_BLOCK:(g + 1) * KV_BLOCK, :] = s
                col = jnp.maximum(col, jnp.max(s, axis=0, keepdims=True) + row_bias(hh, j))
            mx.append(col)
        return mx

    def accumulate(t, mx, st):
        slot = t % 2
        out = []
        for hh in range(2):
            m, l, acc = st[3 * hh:3 * hh + 3]
            m_new = jnp.maximum(m, mx[hh])
            alpha = jnp.exp2(m - m_new)
            l = alpha * l
            acc = alpha * acc
            for g in range(MOBA_G):
                j = t * MOBA_G + g
                s = s_ref[slot, hh, g * KV_BLOCK:(g + 1) * KV_BLOCK, :]
                p = jnp.exp2(s - (m_new - row_bias(hh, j)))
                l = l + jnp.sum(p, axis=0, keepdims=True)
                acc = acc + jnp.dot(v_ref[0, j, hh * HEAD_DIM:(hh + 1) * HEAD_DIM, :],
                                    p.astype(_BF16), preferred_element_type=_F32)
            out += [m_new, l, acc]
        return out

    ntiles = (blk + MOBA_G - 1) // MOBA_G

    def body(i, carry):
        st, mx = carry[:6], carry[6:]
        st = accumulate(i, mx, st)
        return tuple(st) + tuple(scores(i + 1))

    carry = lax.fori_loop(0, jnp.maximum(ntiles - 1, 0), body, tuple(state) + tuple(scores(0)))
    st = accumulate(jnp.maximum(ntiles - 1, 0), carry[6:], carry[:6])
    o_t = jnp.concatenate([st[2] / st[1], st[5] / st[4]], axis=0)
    o_ref[0] = o_t.T.astype(_BF16)


def _moba(qa_t, ka, va_t, kmean, alibi):
    b, _, s = qa_t.shape
    nb = s // KV_BLOCK
    npair = N_HEADS // 2
    assert nb % MOBA_G == 0
    return pl.pallas_call(
        _moba_kernel,
        grid=(b, npair, s // MOBA_Q),
        in_specs=[
            pl.BlockSpec((1, PAIR, MOBA_Q), lambda bi, hp, c: (bi, hp, c)),
            pl.BlockSpec((1, nb, KV_BLOCK, PAIR), lambda bi, hp, c: (bi, 0, 0, hp)),
            pl.BlockSpec((1, nb, PAIR, KV_BLOCK), lambda bi, hp, c: (bi, 0, hp, 0)),
            pl.BlockSpec((1, nb, PAIR), lambda bi, hp, c: (bi, 0, hp)),
            pl.BlockSpec((2, KV_BLOCK + SUBLANES, MOBA_Q), lambda bi, hp, c: (hp, 0, 0)),
        ],
        out_specs=pl.BlockSpec((1, MOBA_Q, PAIR), lambda bi, hp, c: (bi, c, hp)),
        out_shape=jax.ShapeDtypeStruct((b, s, WIDTH), _BF16),
        scratch_shapes=[pltpu.VMEM((2, nb, MOBA_Q), _F32),
                        pltpu.VMEM((2, 2, MOBA_G * KV_BLOCK, MOBA_Q), _F32)],
        compiler_params=_params("arbitrary", "arbitrary", "arbitrary"),
        name="moba",
    )(qa_t, ka, va_t, kmean, alibi)


def _suffix_over_sublanes(x):
    sub = lax.broadcasted_iota(jnp.int32, x.shape, 0)
    y = x
    for d in (1, 2, 4):
        up = pltpu.roll(y, SUBLANES - d, 0)
        y = y + jnp.where(sub < SUBLANES - d, up, 0.0)
    return y


def _sb_scores(k_j, qm, mask):
    z = jnp.dot(k_j, qm, preferred_element_type=_F32)
    nz = -z
    log_not = jnp.minimum(nz, 0.0) - jnp.log(1.0 + jnp.exp(jnp.minimum(z, nz)))
    if mask is not None:
        log_not = jnp.where(mask, log_not, 0.0)
    run = jnp.zeros((SUBLANES, z.shape[1]), _F32)
    partial = [None] * RUN
    for r in reversed(range(RUN)):
        run = run + log_not[r * SUBLANES:(r + 1) * SUBLANES, :]
        partial[r] = run
    return z, partial, run, _suffix_over_sublanes(run)


def _sb_weights(scores, v_jh, carry, acc, mask):
    z, partial, run, incl = scores
    base = (incl - run) + carry
    w = [jnp.exp(z[r * SUBLANES:(r + 1) * SUBLANES, :] + (partial[r] + base))
         for r in range(RUN)]
    w = jnp.concatenate(w, axis=0)
    if mask is not None:
        w = jnp.where(mask, w, 0.0)
    acc = acc + jnp.dot(v_jh, w.astype(_BF16), preferred_element_type=_F32)
    return carry + incl[0:1, :], acc


def _sb_kernel(q_ref, k_ref, v_ref, o_ref):
    jd = pl.program_id(2)
    q_pair = q_ref[0]
    rho = lax.broadcasted_iota(jnp.int32, (KV_BLOCK, SB_Q), 0)
    kpos = (rho & (SUBLANES - 1)) * RUN + (rho >> 3)
    qpos = lax.broadcasted_iota(jnp.int32, (KV_BLOCK, SB_Q), 1)
    strict = kpos < qpos
    heads = [_head_rows(q_pair, hh) for hh in range(2)]
    rows = [slice(hh * HEAD_DIM, (hh + 1) * HEAD_DIM) for hh in range(2)]

    jp = jnp.maximum(jd - 1, 0)
    state = []
    for hh in range(2):
        diag = _sb_scores(k_ref[0, jd], heads[hh], strict)
        prev = _sb_scores(k_ref[0, jp], heads[hh], None)
        carry, acc = _sb_weights(diag, v_ref[0, jd, rows[hh], :],
                                 jnp.zeros((1, SB_Q), _F32),
                                 jnp.zeros((HEAD_DIM, SB_Q), _F32), strict)
        carry = jnp.where(jd > 0, carry, -jnp.inf)
        state += list(_sb_weights(prev, v_ref[0, jp, rows[hh], :], carry, acc, None))

    def cond(st):
        j, c0, _, c1, _ = st
        return jnp.logical_and(j >= 0, jnp.maximum(jnp.max(c0), jnp.max(c1)) > SB_STOP)

    def body(st):
        j = st[0]
        out = []
        for hh in range(2):
            carry, acc = st[1 + 2 * hh:3 + 2 * hh]
            out += list(_sb_weights(_sb_scores(k_ref[0, j], heads[hh], None),
                                    v_ref[0, j, rows[hh], :], carry, acc, None))
        return (j - 1, *out)

    st = lax.while_loop(cond, body, (jd - 2, *state))
    o_ref[0] = jnp.concatenate([st[2], st[4]], axis=0).T.astype(_BF16)


def _stick_breaking(qs_t, ks, vs_t):
    b, _, s = qs_t.shape
    nb = s // KV_BLOCK
    npair = N_HEADS // 2
    return pl.pallas_call(
        _sb_kernel,
        grid=(b, npair, s // SB_Q),
        in_specs=[
            pl.BlockSpec((1, PAIR, SB_Q), lambda bi, hp, i: (bi, hp, i)),
            pl.BlockSpec((1, nb, KV_BLOCK, PAIR), lambda bi, hp, i: (bi, 0, 0, hp)),
            pl.BlockSpec((1, nb, PAIR, KV_BLOCK), lambda bi, hp, i: (bi, 0, hp, 0)),
        ],
        out_specs=pl.BlockSpec((1, SB_Q, PAIR), lambda bi, hp, i: (bi, i, hp)),
        out_shape=jax.ShapeDtypeStruct((b, s, WIDTH), _BF16),
        compiler_params=_params("arbitrary", "arbitrary", "arbitrary"),
        name="stick_breaking",
    )(qs_t, ks, vs_t)


MIX_T = 512
HALO = 16


def _mix_kernel(oa_ref, ob_ref, rest_ref, halo_ref, x_ref, convw_ref, bg_ref,
                wpa_ref, wpb_ref, wpc_ref, wout_ref, o_ref):
    i = pl.program_id(1)
    r = rest_ref[0]
    xc = r[:, 0:WIDTH].astype(_F32)
    bc = r[:, WIDTH:2 * WIDTH].astype(_F32)
    cc = r[:, 2 * WIDTH:3 * WIDTH].astype(_F32)
    u = cc * xc
    hl = halo_ref[0]
    uh = hl[:, 2 * WIDTH:3 * WIDTH].astype(_F32) * hl[:, 0:WIDTH].astype(_F32)
    uh = jnp.where(i > 0, uh, 0.0)
    prev1 = uh[HALO - 1:HALO, :]
    prev2 = uh[HALO - 2:HALO - 1, :]
    row = lax.broadcasted_iota(jnp.int32, u.shape, 0)
    u1 = jnp.where(row == 0, prev1, pltpu.roll(u, 1, 0))
    u2 = jnp.where(row == 0, prev2, jnp.where(row == 1, prev1, pltpu.roll(u, 2, 0)))
    w = convw_ref[...]
    y = bc * (u2 * w[0:1, :] + u1 * w[1:2, :] + u * w[2:3, :])
    g = jax.nn.sigmoid(r[:, 3 * WIDTH:].astype(_F32) + bg_ref[...])
    d = D_MODEL
    merged = (g[:, 0:d] * jnp.dot(oa_ref[0], wpa_ref[...], preferred_element_type=_F32)
              + g[:, d:2 * d] * jnp.dot(ob_ref[0], wpb_ref[...], preferred_element_type=_F32)
              + g[:, 2 * d:] * jnp.dot(y.astype(_BF16), wpc_ref[...],
                                       preferred_element_type=_F32))
    o_ref[0] = x_ref[0] + jnp.dot(merged.astype(_BF16), wout_ref[...],
                                  preferred_element_type=_F32)


def _mix(oa, ob, rest, x, convw, bg, wpa, wpb, wpc, wout):
    b, s, d = x.shape
    t = MIX_T
    tile = lambda width: pl.BlockSpec((1, t, width), lambda bi, i: (bi, i, 0))
    halo = pl.BlockSpec((1, HALO, REST_WIDTH),
                        lambda bi, i: (bi, jnp.maximum(i * (t // HALO) - 1, 0), 0))
    return pl.pallas_call(
        _mix_kernel,
        grid=(b, s // t),
        in_specs=[tile(WIDTH), tile(WIDTH), tile(REST_WIDTH), halo, tile(d),
                  _const_spec((CONV_K, WIDTH)), _const_spec((1, 3 * d)),
                  _const_spec((WIDTH, d)), _const_spec((WIDTH, d)), _const_spec((WIDTH, d)),
                  _const_spec((d, d))],
        out_specs=tile(d),
        out_shape=jax.ShapeDtypeStruct((b, s, d), _F32),
        compiler_params=_params("arbitrary", "arbitrary"),
        name="mix",
    )(oa, ob, rest, rest, x, convw, bg, wpa, wpb, wpc, wout)


FFN_T = 512
FF_CHUNK = 256


def _ffn_up_kernel(x_ref, g_ref, wg_ref, wu_ref, o_ref):
    h = _rms(x_ref[...], g_ref[...]).astype(_BF16)
    for n in range(D_FF // FF_CHUNK):
        cols = slice(n * FF_CHUNK, (n + 1) * FF_CHUNK)
        a = jnp.dot(h, wg_ref[:, cols], preferred_element_type=_F32)
        bgate = jnp.dot(h, wu_ref[:, cols], preferred_element_type=_F32)
        o_ref[:, cols] = (a * jax.nn.sigmoid(a) * bgate).astype(_BF16)


def _ffn_down_kernel(a_ref, x_ref, wd_ref, g_ref, o_ref, *, final_norm):
    y = x_ref[...] + jnp.dot(a_ref[...], wd_ref[...], preferred_element_type=_F32)
    if final_norm:
        y = _rms(y, g_ref[...])
    o_ref[...] = y


def _ffn(x2d, g, wg, wu, wd, gfinal, final_norm):
    n, d = x2d.shape
    t = FFN_T
    act = pl.pallas_call(
        _ffn_up_kernel,
        grid=(n // t,),
        in_specs=[pl.BlockSpec((t, d), lambda i: (i, 0)), _const_spec((1, d)),
                  _const_spec((d, D_FF)), _const_spec((d, D_FF))],
        out_specs=pl.BlockSpec((t, D_FF), lambda i: (i, 0)),
        out_shape=jax.ShapeDtypeStruct((n, D_FF), _BF16),
        compiler_params=_params("arbitrary"),
        name="ffn_up",
    )(x2d, g, wg, wu)
    return pl.pallas_call(
        functools.partial(_ffn_down_kernel, final_norm=final_norm),
        grid=(n // t,),
        in_specs=[pl.BlockSpec((t, D_FF), lambda i: (i, 0)),
                  pl.BlockSpec((t, d), lambda i: (i, 0)),
                  _const_spec((D_FF, d)), _const_spec((1, d))],
        out_specs=pl.BlockSpec((t, d), lambda i: (i, 0)),
        out_shape=jax.ShapeDtypeStruct((n, d), _F32),
        compiler_params=_params("arbitrary"),
        name="ffn_down",
    )(act, x2d, wd, gfinal)


def _sb_permutation():
    rho = np.arange(KV_BLOCK)
    kappa = (rho % SUBLANES) * RUN + rho // SUBLANES
    p = np.zeros((KV_BLOCK, KV_BLOCK), np.float32)
    p[rho, kappa] = 1.0
    return jnp.asarray(p, _BF16)


def _alibi_table():
    slopes = np.exp2(-8.0 * (np.arange(N_HEADS, dtype=np.float64) + 1.0) / N_HEADS) * LOG2E
    rows = np.concatenate([np.arange(KV_BLOCK, dtype=np.float64), np.ones(SUBLANES)])
    tab = slopes[:, None, None] * rows[None, :, None] * np.ones((1, 1, MOBA_Q))
    return jnp.asarray(tab.astype(np.float32))


def kernel(x, norm_mix_g, w_in, b_gate, conv_w, w_proj_moba, w_proj_sb, w_proj_conv, w_out,
           norm_ffn_g, w_ffn_gate, w_ffn_up, w_ffn_down, norm_final_g):
    depth = w_in.shape[0]
    b, s, d = x.shape
    scale = HEAD_DIM ** -0.5
    perm = _sb_permutation()
    alibi = _alibi_table()
    w = WIDTH
    for l in range(depth):
        wl = w_in[l]
        cols = lambda a, n=w: wl[:, a:a + n].astype(_BF16)
        wqa = (wl[:, 0:w] * (scale * LOG2E)).T.astype(_BF16)
        wka = cols(w)
        wva = wl[:, 2 * w:3 * w].T.astype(_BF16)
        wqs = (wl[:, 3 * w:4 * w] * scale).T.astype(_BF16)
        wks = cols(4 * w)
        wvs = wl[:, 5 * w:6 * w].T.astype(_BF16)
        wrest = cols(6 * w, REST_WIDTH)
        qa_t, ka, va_t, kmean, qs_t, ks, vs_t, rest = _in_proj(
            x, norm_mix_g[l][None, :], perm, wqa, wka, wva, wqs, wks, wvs, wrest)
        oa = _moba(qa_t, ka, va_t, kmean, alibi)
        ob = _stick_breaking(qs_t, ks, vs_t)
        x = _mix(oa, ob, rest, x, conv_w[l], b_gate[l][None, :],
                 w_proj_moba[l].astype(_BF16), w_proj_sb[l].astype(_BF16),
                 w_proj_conv[l].astype(_BF16), w_out[l].astype(_BF16))
        x = _ffn(x.reshape(b * s, d), norm_ffn_g[l][None, :], w_ffn_gate[l].astype(_BF16),
                 w_ffn_up[l].astype(_BF16), w_ffn_down[l].astype(_BF16),
                 norm_final_g[None, :], final_norm=(l == depth - 1)).reshape(b, s, d)
    return x
```

```python
import functools

import jax
import jax.numpy as jnp
import numpy as np
from jax import lax
from jax.experimental import pallas as pl
from jax.experimental.pallas import tpu as pltpu

D_MODEL = 1024
HEAD_DIM = 64
N_HEADS = 8
WIDTH = N_HEADS * HEAD_DIM
CONV_K = 3
KV_BLOCK = 256
MOBA_TOPK = 3
MOBA_Q = 256
MOBA_G = 2
SB_Q = KV_BLOCK
MOBA_SKIP = -150.0
D_FF = 2816
RMS_EPS = 1e-6
REST_WIDTH = 3 * WIDTH + 3 * D_MODEL
PAIR = 2 * HEAD_DIM
SUBLANES = 8
LOG2E = 1.4426950408889634
RUN = KV_BLOCK // SUBLANES
SB_STOP = -150.0
VMEM_LIMIT = 56 * 1024 * 1024

_NT = (((1,), (1,)), ((), ()))
_F32 = jnp.float32
_BF16 = jnp.bfloat16


def _params(*sem):
    return pltpu.CompilerParams(dimension_semantics=sem, vmem_limit_bytes=VMEM_LIMIT)


def _const_spec(shape):
    zeros = (0,) * len(shape)
    return pl.BlockSpec(shape, lambda *_: zeros)


def _rms(x, g):
    y = x * lax.rsqrt(jnp.mean(x * x, axis=-1, keepdims=True) + RMS_EPS)
    return y * g


def _in_proj_kernel(x_ref, g_ref, perm_ref, wqa_ref, wka_ref, wva_ref, wqs_ref, wks_ref,
                    wvs_ref, wrest_ref,
                    qa_ref, ka_ref, va_ref, kmean_ref, kamax_ref, qs_ref, ks_ref, vs_ref,
                    rest_ref):
    i = pl.program_id(1)
    h = _rms(x_ref[0], g_ref[...]).astype(_BF16)
    qa_ref[0] = lax.dot_general(wqa_ref[...], h, _NT,
                                preferred_element_type=_F32).astype(_BF16)
    ka = jnp.dot(h, wka_ref[...], preferred_element_type=_F32)
    ka_b = ka.astype(_BF16)
    ka_ref[0, 0] = ka_b
    kmean_ref[0, pl.ds(i, 1), :] = jnp.mean(ka, axis=0, keepdims=True)
    kamax_ref[0, pl.ds(i, 1), :] = jnp.max(jnp.abs(ka_b.astype(_F32)), axis=0, keepdims=True)
    va_ref[0, 0] = lax.dot_general(wva_ref[...], h, _NT,
                                   preferred_element_type=_F32).astype(_BF16)
    qs_ref[0] = lax.dot_general(wqs_ref[...], h, _NT,
                                preferred_element_type=_F32).astype(_BF16)
    hp = jnp.dot(perm_ref[...], h, preferred_element_type=_F32).astype(_BF16)
    ks_ref[0, 0] = jnp.dot(hp, wks_ref[...], preferred_element_type=_F32).astype(_BF16)
    vs_ref[0, 0] = lax.dot_general(wvs_ref[...], hp, _NT,
                                   preferred_element_type=_F32).astype(_BF16)
    rest_ref[0] = jnp.dot(h, wrest_ref[...], preferred_element_type=_F32).astype(_BF16)


def _in_proj(x, g, perm, wqa, wka, wva, wqs, wks, wvs, wrest):
    b, s, d = x.shape
    nb = s // KV_BLOCK
    t = KV_BLOCK
    out_shape = (
        jax.ShapeDtypeStruct((b, WIDTH, s), _BF16),
        jax.ShapeDtypeStruct((b, nb, t, WIDTH), _BF16),
        jax.ShapeDtypeStruct((b, nb, WIDTH, t), _BF16),
        jax.ShapeDtypeStruct((b, nb, WIDTH), _F32),
        jax.ShapeDtypeStruct((b, nb, WIDTH), _F32),
        jax.ShapeDtypeStruct((b, WIDTH, s), _BF16),
        jax.ShapeDtypeStruct((b, nb, t, WIDTH), _BF16),
        jax.ShapeDtypeStruct((b, nb, WIDTH, t), _BF16),
        jax.ShapeDtypeStruct((b, s, REST_WIDTH), _BF16),
    )
    qt_spec = pl.BlockSpec((1, WIDTH, t), lambda bi, i: (bi, 0, i))
    k_spec = pl.BlockSpec((1, 1, t, WIDTH), lambda bi, i: (bi, i, 0, 0))
    vt_spec = pl.BlockSpec((1, 1, WIDTH, t), lambda bi, i: (bi, i, 0, 0))
    return pl.pallas_call(
        _in_proj_kernel,
        grid=(b, nb),
        in_specs=[
            pl.BlockSpec((1, t, d), lambda bi, i: (bi, i, 0)),
            _const_spec((1, d)),
            _const_spec((t, t)),
            _const_spec((WIDTH, d)), _const_spec((d, WIDTH)), _const_spec((WIDTH, d)),
            _const_spec((WIDTH, d)), _const_spec((d, WIDTH)), _const_spec((WIDTH, d)),
            _const_spec((d, REST_WIDTH)),
        ],
        out_specs=(
            qt_spec, k_spec, vt_spec,
            pl.BlockSpec((1, nb, WIDTH), lambda bi, i: (bi, 0, 0)),
            pl.BlockSpec((1, nb, WIDTH), lambda bi, i: (bi, 0, 0)),
            qt_spec, k_spec, vt_spec,
            pl.BlockSpec((1, t, REST_WIDTH), lambda bi, i: (bi, i, 0)),
        ),
        out_shape=out_shape,
        compiler_params=_params("arbitrary", "arbitrary"),
        name="in_proj",
    )(x, g, perm, wqa, wka, wva, wqs, wks, wvs, wrest)


def _head_rows(q_pair, hh):
    row = lax.broadcasted_iota(jnp.int32, q_pair.shape, 0)
    keep = (row >= hh * HEAD_DIM) & (row < (hh + 1) * HEAD_DIM)
    return jnp.where(keep, q_pair, jnp.zeros_like(q_pair))


def _moba_kernel(q_ref, k_ref, v_ref, kmean_ref, kamax_ref, alibi_ref, o_ref,
                 sel_ref, sa_ref, sb_ref, l_ref, acc_ref):
    blk = pl.program_id(2)
    nb = k_ref.shape[1]
    q_pair = q_ref[0]
    km = kmean_ref[0].astype(_BF16)
    kamax = kamax_ref[0].astype(_BF16)
    nidx = lax.broadcasted_iota(jnp.int32, (nb, MOBA_Q), 0)
    neg_inf = jnp.float32(-jnp.inf)
    slope = [alibi_ref[hh, 0, KV_BLOCK:KV_BLOCK + 1, :] for hh in range(2)]

    heads = []
    first_needed = jnp.int32(nb)
    for hh in range(2):
        qm = _head_rows(q_pair, hh)
        bound = jnp.dot(kamax, jnp.abs(qm), preferred_element_type=_F32)
        own = jnp.max(jnp.where(nidx == blk, bound, 0.0), axis=0, keepdims=True)
        far = slope[hh] * ((nidx + 1 - blk) * KV_BLOCK).astype(_F32)
        needed = (nidx < blk) & (bound + own + far > MOBA_SKIP)
        first_needed = jnp.minimum(first_needed, jnp.min(jnp.where(needed, nidx, nb)))
        gate = jnp.dot(km, qm, preferred_element_type=_F32)
        g = jnp.where(nidx < blk, gate, neg_inf)
        sel = nidx == blk
        for r in range(MOBA_TOPK):
            mx = jnp.max(g, axis=0, keepdims=True)
            first = jnp.min(jnp.where(g == mx, nidx, nb), axis=0, keepdims=True)
            hit = nidx == first
            sel = sel | (hit & (jnp.full((nb, MOBA_Q), r, jnp.int32) < blk))
            g = jnp.where(hit, neg_inf, g)
        sel_ref[hh] = jnp.where(sel, 0.0, neg_inf)
        heads.append(qm)

    def shift(j):
        return jnp.full((1, MOBA_Q), (j - blk) * KV_BLOCK, jnp.int32).astype(_F32)

    def row_bias(hh, j):
        return sel_ref[hh, pl.ds(j, 1), :] + slope[hh] * shift(j)

    def scores(t, dst_ref):
        mx = []
        for hh in range(2):
            col = jnp.full((1, MOBA_Q), neg_inf, _F32)
            for g in range(MOBA_G):
                j = t * MOBA_G + g
                ramp = alibi_ref[hh, (j == blk).astype(jnp.int32), 0:KV_BLOCK, :]
                s = jnp.dot(k_ref[0, j], heads[hh], preferred_element_type=_F32) + ramp
                dst_ref[hh, g * KV_BLOCK:(g + 1) * KV_BLOCK, :] = s
                col = jnp.maximum(col, jnp.max(s, axis=0, keepdims=True) + row_bias(hh, j))
            mx.append(col)
        return mx

    def accumulate(t, src_ref, mx, st):
        out = []
        for hh in range(2):
            m, l, acc = st[3 * hh:3 * hh + 3]
            m_new = jnp.maximum(m, mx[hh])
            alpha = jnp.exp2(m - m_new)
            l = alpha * l
            acc = alpha * acc
            for g in range(MOBA_G):
                j = t * MOBA_G + g
                s = src_ref[hh, g * KV_BLOCK:(g + 1) * KV_BLOCK, :]
                p = jnp.exp2(s - (m_new - row_bias(hh, j)))
                l = l + jnp.sum(p, axis=0, keepdims=True)
                acc = acc + jnp.dot(v_ref[0, j, hh * HEAD_DIM:(hh + 1) * HEAD_DIM, :],
                                    p.astype(_BF16), preferred_element_type=_F32)
            out += [m_new, l, acc]
        return out

    last = blk // MOBA_G
    first = jnp.minimum(first_needed // MOBA_G, last)
    ntiles = last - first + 1
    state = []
    for hh in range(2):
        state += [jnp.full((1, MOBA_Q), jnp.finfo(_F32).min, _F32),
                  jnp.zeros((1, MOBA_Q), _F32), jnp.zeros((HEAD_DIM, MOBA_Q), _F32)]

    def body(k, carry):
        st, mx_a = carry[:6], carry[6:]
        t = last - 2 * k
        mx_b = scores(t - 1, sb_ref)
        st = accumulate(t, sa_ref, mx_a, st)
        mx_a = scores(jnp.maximum(t - 2, 0), sa_ref)
        st = accumulate(t - 1, sb_ref, mx_b, st)
        return tuple(st) + tuple(mx_a)

    carry = lax.fori_loop(0, ntiles // 2, body, tuple(state) + tuple(scores(last, sa_ref)))
    st, mx_a = carry[:6], carry[6:]
    for hh in range(2):
        l_ref[hh], acc_ref[hh] = st[3 * hh + 1:3 * hh + 3]

    @pl.when(ntiles % 2 == 1)
    def _():
        odd = accumulate(first, sa_ref, mx_a, st)
        for hh in range(2):
            l_ref[hh], acc_ref[hh] = odd[3 * hh + 1:3 * hh + 3]

    o_t = jnp.concatenate([acc_ref[hh] / l_ref[hh] for hh in range(2)], axis=0)
    o_ref[0] = o_t.T.astype(_BF16)


def _moba(qa_t, ka, va_t, kmean, kamax, alibi):
    b, _, s = qa_t.shape
    nb = s // KV_BLOCK
    npair = N_HEADS // 2
    assert nb % MOBA_G == 0
    return pl.pallas_call(
        _moba_kernel,
        grid=(b, npair, s // MOBA_Q),
        in_specs=[
            pl.BlockSpec((1, PAIR, MOBA_Q), lambda bi, hp, c: (bi, hp, c)),
            pl.BlockSpec((1, nb, KV_BLOCK, PAIR), lambda bi, hp, c: (bi, 0, 0, hp)),
            pl.BlockSpec((1, nb, PAIR, KV_BLOCK), lambda bi, hp, c: (bi, 0, hp, 0)),
            pl.BlockSpec((1, nb, PAIR), lambda bi, hp, c: (bi, 0, hp)),
            pl.BlockSpec((1, nb, PAIR), lambda bi, hp, c: (bi, 0, hp)),
            pl.BlockSpec((2, 2, KV_BLOCK + SUBLANES, MOBA_Q), lambda bi, hp, c: (hp, 0, 0, 0)),
        ],
        out_specs=pl.BlockSpec((1, MOBA_Q, PAIR), lambda bi, hp, c: (bi, c, hp)),
        out_shape=jax.ShapeDtypeStruct((b, s, WIDTH), _BF16),
        scratch_shapes=[pltpu.VMEM((2, nb, MOBA_Q), _F32),
                        pltpu.VMEM((2, MOBA_G * KV_BLOCK, MOBA_Q), _F32),
                        pltpu.VMEM((2, MOBA_G * KV_BLOCK, MOBA_Q), _F32),
                        pltpu.VMEM((2, 1, MOBA_Q), _F32),
                        pltpu.VMEM((2, HEAD_DIM, MOBA_Q), _F32)],
        compiler_params=_params("arbitrary", "arbitrary", "arbitrary"),
        name="moba",
    )(qa_t, ka, va_t, kmean, kamax, alibi)


def _suffix_over_sublanes(x):
    sub = lax.broadcasted_iota(jnp.int32, x.shape, 0)
    y = x
    for d in (1, 2, 4):
        up = pltpu.roll(y, SUBLANES - d, 0)
        y = y + jnp.where(sub < SUBLANES - d, up, 0.0)
    return y


def _sb_scores(k_j, qm, mask):
    z = jnp.dot(k_j, qm, preferred_element_type=_F32)
    nz = -z
    log_not = jnp.minimum(nz, 0.0) - jnp.log(1.0 + jnp.exp(jnp.minimum(z, nz)))
    if mask is not None:
        log_not = jnp.where(mask, log_not, 0.0)
    run = jnp.zeros((SUBLANES, z.shape[1]), _F32)
    partial = [None] * RUN
    for r in reversed(range(RUN)):
        run = run + log_not[r * SUBLANES:(r + 1) * SUBLANES, :]
        partial[r] = run
    return z, partial, run, _suffix_over_sublanes(run)


def _sb_weights(scores, v_jh, carry, acc, mask):
    z, partial, run, incl = scores
    base = (incl - run) + carry
    w = [jnp.exp(z[r * SUBLANES:(r + 1) * SUBLANES, :] + (partial[r] + base))
         for r in range(RUN)]
    w = jnp.concatenate(w, axis=0)
    if mask is not None:
        w = jnp.where(mask, w, 0.0)
    acc = acc + jnp.dot(v_jh, w.astype(_BF16), preferred_element_type=_F32)
    return carry + incl[0:1, :], acc


def _sb_kernel(q_ref, k_ref, v_ref, o_ref):
    jd = pl.program_id(2)
    q_pair = q_ref[0]
    rho = lax.broadcasted_iota(jnp.int32, (KV_BLOCK, SB_Q), 0)
    kpos = (rho & (SUBLANES - 1)) * RUN + (rho >> 3)
    qpos = lax.broadcasted_iota(jnp.int32, (KV_BLOCK, SB_Q), 1)
    strict = kpos < qpos
    heads = [_head_rows(q_pair, hh) for hh in range(2)]
    rows = [slice(hh * HEAD_DIM, (hh + 1) * HEAD_DIM) for hh in range(2)]

    jp = jnp.maximum(jd - 1, 0)
    state = []
    for hh in range(2):
        diag = _sb_scores(k_ref[0, jd], heads[hh], strict)
        prev = _sb_scores(k_ref[0, jp], heads[hh], None)
        carry, acc = _sb_weights(diag, v_ref[0, jd, rows[hh], :],
                                 jnp.zeros((1, SB_Q), _F32),
                                 jnp.zeros((HEAD_DIM, SB_Q), _F32), strict)
        carry = jnp.where(jd > 0, carry, -jnp.inf)
        state += list(_sb_weights(prev, v_ref[0, jp, rows[hh], :], carry, acc, None))

    def cond(st):
        j, c0, _, c1, _ = st
        return jnp.logical_and(j >= 0, jnp.maximum(jnp.max(c0), jnp.max(c1)) > SB_STOP)

    def body(st):
        j = st[0]
        out = []
        for hh in range(2):
            carry, acc = st[1 + 2 * hh:3 + 2 * hh]
            out += list(_sb_weights(_sb_scores(k_ref[0, j], heads[hh], None),
                                    v_ref[0, j, rows[hh], :], carry, acc, None))
        return (j - 1, *out)

    st = lax.while_loop(cond, body, (jd - 2, *state))
    o_ref[0] = jnp.concatenate([st[2], st[4]], axis=0).T.astype(_BF16)


def _stick_breaking(qs_t, ks, vs_t):
    b, _, s = qs_t.shape
    nb = s // KV_BLOCK
    npair = N_HEADS // 2
    return pl.pallas_call(
        _sb_kernel,
        grid=(b, npair, s // SB_Q),
        in_specs=[
            pl.BlockSpec((1, PAIR, SB_Q), lambda bi, hp, i: (bi, hp, i)),
            pl.BlockSpec((1, nb, KV_BLOCK, PAIR), lambda bi, hp, i: (bi, 0, 0, hp)),
            pl.BlockSpec((1, nb, PAIR, KV_BLOCK), lambda bi, hp, i: (bi, 0, hp, 0)),
        ],
        out_specs=pl.BlockSpec((1, SB_Q, PAIR), lambda bi, hp, i: (bi, i, hp)),
        out_shape=jax.ShapeDtypeStruct((b, s, WIDTH), _BF16),
        compiler_params=_params("arbitrary", "arbitrary", "arbitrary"),
        name="stick_breaking",
    )(qs_t, ks, vs_t)


MIX_T = 512
HALO = 16


def _mix_kernel(oa_ref, ob_ref, rest_ref, halo_ref, x_ref, convw_ref, bg_ref,
                wpa_ref, wpb_ref, wpc_ref, wout_ref, o_ref):
    i = pl.program_id(1)
    r = rest_ref[0]
    xc = r[:, 0:WIDTH].astype(_F32)
    bc = r[:, WIDTH:2 * WIDTH].astype(_F32)
    cc = r[:, 2 * WIDTH:3 * WIDTH].astype(_F32)
    u = cc * xc
    hl = halo_ref[0]
    uh = hl[:, 2 * WIDTH:3 * WIDTH].astype(_F32) * hl[:, 0:WIDTH].astype(_F32)
    uh = jnp.where(i > 0, uh, 0.0)
    prev1 = uh[HALO - 1:HALO, :]
    prev2 = uh[HALO - 2:HALO - 1, :]
    row = lax.broadcasted_iota(jnp.int32, u.shape, 0)
    u1 = jnp.where(row == 0, prev1, pltpu.roll(u, 1, 0))
    u2 = jnp.where(row == 0, prev2, jnp.where(row == 1, prev1, pltpu.roll(u, 2, 0)))
    w = convw_ref[...]
    y = bc * (u2 * w[0:1, :] + u1 * w[1:2, :] + u * w[2:3, :])
    g = jax.nn.sigmoid(r[:, 3 * WIDTH:].astype(_F32) + bg_ref[...])
    d = D_MODEL
    merged = (g[:, 0:d] * jnp.dot(oa_ref[0], wpa_ref[...], preferred_element_type=_F32)
              + g[:, d:2 * d] * jnp.dot(ob_ref[0], wpb_ref[...], preferred_element_type=_F32)
              + g[:, 2 * d:] * jnp.dot(y.astype(_BF16), wpc_ref[...],
                                       preferred_element_type=_F32))
    o_ref[0] = x_ref[0] + jnp.dot(merged.astype(_BF16), wout_ref[...],
                                  preferred_element_type=_F32)


def _mix(oa, ob, rest, x, convw, bg, wpa, wpb, wpc, wout):
    b, s, d = x.shape
    t = MIX_T
    tile = lambda width: pl.BlockSpec((1, t, width), lambda bi, i: (bi, i, 0))
    halo = pl.BlockSpec((1, HALO, REST_WIDTH),
                        lambda bi, i: (bi, jnp.maximum(i * (t // HALO) - 1, 0), 0))
    return pl.pallas_call(
        _mix_kernel,
        grid=(b, s // t),
        in_specs=[tile(WIDTH), tile(WIDTH), tile(REST_WIDTH), halo, tile(d),
                  _const_spec((CONV_K, WIDTH)), _const_spec((1, 3 * d)),
                  _const_spec((WIDTH, d)), _const_spec((WIDTH, d)), _const_spec((WIDTH, d)),
                  _const_spec((d, d))],
        out_specs=tile(d),
        out_shape=jax.ShapeDtypeStruct((b, s, d), _F32),
        compiler_params=_params("arbitrary", "arbitrary"),
        name="mix",
    )(oa, ob, rest, rest, x, convw, bg, wpa, wpb, wpc, wout)


FFN_T = 512
FF_CHUNK = 256


def _ffn_up_kernel(x_ref, g_ref, wg_ref, wu_ref, o_ref):
    h = _rms(x_ref[...], g_ref[...]).astype(_BF16)
    for n in range(D_FF // FF_CHUNK):
        cols = slice(n * FF_CHUNK, (n + 1) * FF_CHUNK)
        a = jnp.dot(h, wg_ref[:, cols], preferred_element_type=_F32)
        bgate = jnp.dot(h, wu_ref[:, cols], preferred_element_type=_F32)
        o_ref[:, cols] = (a * jax.nn.sigmoid(a) * bgate).astype(_BF16)


def _ffn_down_kernel(a_ref, x_ref, wd_ref, g_ref, o_ref, *, final_norm):
    y = x_ref[...] + jnp.dot(a_ref[...], wd_ref[...], preferred_element_type=_F32)
    if final_norm:
        y = _rms(y, g_ref[...])
    o_ref[...] = y


def _ffn(x2d, g, wg, wu, wd, gfinal, final_norm):
    n, d = x2d.shape
    t = FFN_T
    act = pl.pallas_call(
        _ffn_up_kernel,
        grid=(n // t,),
        in_specs=[pl.BlockSpec((t, d), lambda i: (i, 0)), _const_spec((1, d)),
                  _const_spec((d, D_FF)), _const_spec((d, D_FF))],
        out_specs=pl.BlockSpec((t, D_FF), lambda i: (i, 0)),
        out_shape=jax.ShapeDtypeStruct((n, D_FF), _BF16),
        compiler_params=_params("arbitrary"),
        name="ffn_up",
    )(x2d, g, wg, wu)
    return pl.pallas_call(
        functools.partial(_ffn_down_kernel, final_norm=final_norm),
        grid=(n // t,),
        in_specs=[pl.BlockSpec((t, D_FF), lambda i: (i, 0)),
                  pl.BlockSpec((t, d), lambda i: (i, 0)),
                  _const_spec((D_FF, d)), _const_spec((1, d))],
        out_specs=pl.BlockSpec((t, d), lambda i: (i, 0)),
        out_shape=jax.ShapeDtypeStruct((n, d), _F32),
        compiler_params=_params("arbitrary"),
        name="ffn_down",
    )(act, x2d, wd, gfinal)


def _sb_permutation():
    rho = np.arange(KV_BLOCK)
    kappa = (rho % SUBLANES) * RUN + rho // SUBLANES
    p = np.zeros((KV_BLOCK, KV_BLOCK), np.float32)
    p[rho, kappa] = 1.0
    return jnp.asarray(p, _BF16)


def _alibi_table():
    slopes = np.exp2(-8.0 * (np.arange(N_HEADS, dtype=np.float64) + 1.0) / N_HEADS) * LOG2E
    rows = np.concatenate([np.arange(KV_BLOCK, dtype=np.float64), np.ones(SUBLANES)])
    tab = slopes[:, None, None] * rows[None, :, None] * np.ones((1, 1, MOBA_Q))
    key = np.arange(KV_BLOCK + SUBLANES)[:, None]
    future = (key > np.arange(MOBA_Q)[None, :]) & (key < KV_BLOCK)
    tab = np.stack([tab, np.where(future[None], -np.inf, tab)], axis=1)
    return jnp.asarray(tab.astype(np.float32))


def kernel(x, norm_mix_g, w_in, b_gate, conv_w, w_proj_moba, w_proj_sb, w_proj_conv, w_out,
           norm_ffn_g, w_ffn_gate, w_ffn_up, w_ffn_down, norm_final_g):
    depth = w_in.shape[0]
    b, s, d = x.shape
    scale = HEAD_DIM ** -0.5
    perm = _sb_permutation()
    alibi = _alibi_table()
    w = WIDTH
    for l in range(depth):
        wl = w_in[l]
        cols = lambda a, n=w: wl[:, a:a + n].astype(_BF16)
        wqa = (wl[:, 0:w] * (scale * LOG2E)).T.astype(_BF16)
        wka = cols(w)
        wva = wl[:, 2 * w:3 * w].T.astype(_BF16)
        wqs = (wl[:, 3 * w:4 * w] * scale).T.astype(_BF16)
        wks = cols(4 * w)
        wvs = wl[:, 5 * w:6 * w].T.astype(_BF16)
        wrest = cols(6 * w, REST_WIDTH)
        qa_t, ka, va_t, kmean, kamax, qs_t, ks, vs_t, rest = _in_proj(
            x, norm_mix_g[l][None, :], perm, wqa, wka, wva, wqs, wks, wvs, wrest)
        oa = _moba(qa_t, ka, va_t, kmean, kamax, alibi)
        ob = _stick_breaking(qs_t, ks, vs_t)
        x = _mix(oa, ob, rest, x, conv_w[l], b_gate[l][None, :],
                 w_proj_moba[l].astype(_BF16), w_proj_sb[l].astype(_BF16),
                 w_proj_conv[l].astype(_BF16), w_out[l].astype(_BF16))
        x = _ffn(x.reshape(b * s, d), norm_ffn_g[l][None, :], w_ffn_gate[l].astype(_BF16),
                 w_ffn_up[l].astype(_BF16), w_ffn_down[l].astype(_BF16),
                 norm_final_g[None, :], final_norm=(l == depth - 1)).reshape(b, s, d)
    return x
```

```python
import functools

import jax
import jax.numpy as jnp
import numpy as np
from jax import lax
from jax.experimental import pallas as pl
from jax.experimental.pallas import tpu as pltpu

D_MODEL = 1024
HEAD_DIM = 64
N_HEADS = 8
WIDTH = N_HEADS * HEAD_DIM
CONV_K = 3
KV_BLOCK = 256
MOBA_TOPK = 3
MOBA_Q = 256
MOBA_G = 2
SB_Q = KV_BLOCK
MOBA_SKIP = -150.0
D_FF = 2816
RMS_EPS = 1e-6
REST_WIDTH = 3 * WIDTH + 3 * D_MODEL
PAIR = 2 * HEAD_DIM
SUBLANES = 8
LOG2E = 1.4426950408889634
RUN = KV_BLOCK // SUBLANES
SB_STOP = 1e-30
VMEM_LIMIT = 56 * 1024 * 1024

_NT = (((1,), (1,)), ((), ()))
_F32 = jnp.float32
_BF16 = jnp.bfloat16


def _params(*sem):
    return pltpu.CompilerParams(dimension_semantics=sem, vmem_limit_bytes=VMEM_LIMIT)


def _const_spec(shape):
    zeros = (0,) * len(shape)
    return pl.BlockSpec(shape, lambda *_: zeros)


def _rms(x, g):
    y = x * lax.rsqrt(jnp.mean(x * x, axis=-1, keepdims=True) + RMS_EPS)
    return y * g


def _in_proj_kernel(x_ref, g_ref, perm_ref, wqa_ref, wka_ref, wva_ref, wqs_ref, wks_ref,
                    wvs_ref, wrest_ref,
                    qa_ref, ka_ref, va_ref, kmean_ref, kamax_ref, qs_ref, ks_ref, vs_ref,
                    rest_ref):
    i = pl.program_id(1)
    h = _rms(x_ref[0], g_ref[...]).astype(_BF16)
    qa_ref[0] = lax.dot_general(wqa_ref[...], h, _NT,
                                preferred_element_type=_F32).astype(_BF16)
    ka = jnp.dot(h, wka_ref[...], preferred_element_type=_F32)
    ka_b = ka.astype(_BF16)
    ka_ref[0, 0] = ka_b
    kmean_ref[0, pl.ds(i, 1), :] = jnp.mean(ka, axis=0, keepdims=True)
    kamax_ref[0, pl.ds(i, 1), :] = jnp.max(jnp.abs(ka_b.astype(_F32)), axis=0, keepdims=True)
    va_ref[0, 0] = lax.dot_general(wva_ref[...], h, _NT,
                                   preferred_element_type=_F32).astype(_BF16)
    qs_ref[0] = lax.dot_general(wqs_ref[...], h, _NT,
                                preferred_element_type=_F32).astype(_BF16)
    hp = jnp.dot(perm_ref[...], h, preferred_element_type=_F32).astype(_BF16)
    ks_ref[0, 0] = jnp.dot(hp, wks_ref[...], preferred_element_type=_F32).astype(_BF16)
    vs_ref[0, 0] = lax.dot_general(wvs_ref[...], hp, _NT,
                                   preferred_element_type=_F32).astype(_BF16)
    rest_ref[0] = jnp.dot(h, wrest_ref[...], preferred_element_type=_F32).astype(_BF16)


def _in_proj(x, g, perm, wqa, wka, wva, wqs, wks, wvs, wrest):
    b, s, d = x.shape
    nb = s // KV_BLOCK
    t = KV_BLOCK
    out_shape = (
        jax.ShapeDtypeStruct((b, WIDTH, s), _BF16),
        jax.ShapeDtypeStruct((b, nb, t, WIDTH), _BF16),
        jax.ShapeDtypeStruct((b, nb, WIDTH, t), _BF16),
        jax.ShapeDtypeStruct((b, nb, WIDTH), _F32),
        jax.ShapeDtypeStruct((b, nb, WIDTH), _F32),
        jax.ShapeDtypeStruct((b, WIDTH, s), _BF16),
        jax.ShapeDtypeStruct((b, nb, t, WIDTH), _BF16),
        jax.ShapeDtypeStruct((b, nb, WIDTH, t), _BF16),
        jax.ShapeDtypeStruct((b, s, REST_WIDTH), _BF16),
    )
    qt_spec = pl.BlockSpec((1, WIDTH, t), lambda bi, i: (bi, 0, i))
    k_spec = pl.BlockSpec((1, 1, t, WIDTH), lambda bi, i: (bi, i, 0, 0))
    vt_spec = pl.BlockSpec((1, 1, WIDTH, t), lambda bi, i: (bi, i, 0, 0))
    return pl.pallas_call(
        _in_proj_kernel,
        grid=(b, nb),
        in_specs=[
            pl.BlockSpec((1, t, d), lambda bi, i: (bi, i, 0)),
            _const_spec((1, d)),
            _const_spec((t, t)),
            _const_spec((WIDTH, d)), _const_spec((d, WIDTH)), _const_spec((WIDTH, d)),
            _const_spec((WIDTH, d)), _const_spec((d, WIDTH)), _const_spec((WIDTH, d)),
            _const_spec((d, REST_WIDTH)),
        ],
        out_specs=(
            qt_spec, k_spec, vt_spec,
            pl.BlockSpec((1, nb, WIDTH), lambda bi, i: (bi, 0, 0)),
            pl.BlockSpec((1, nb, WIDTH), lambda bi, i: (bi, 0, 0)),
            qt_spec, k_spec, vt_spec,
            pl.BlockSpec((1, t, REST_WIDTH), lambda bi, i: (bi, i, 0)),
        ),
        out_shape=out_shape,
        compiler_params=_params("arbitrary", "arbitrary"),
        name="in_proj",
    )(x, g, perm, wqa, wka, wva, wqs, wks, wvs, wrest)


def _head_rows(q_pair, hh):
    row = lax.broadcasted_iota(jnp.int32, q_pair.shape, 0)
    keep = (row >= hh * HEAD_DIM) & (row < (hh + 1) * HEAD_DIM)
    return jnp.where(keep, q_pair, jnp.zeros_like(q_pair))


def _moba_kernel(q_ref, k_ref, v_ref, kmean_ref, kamax_ref, alibi_ref, o_ref,
                 sel_ref, sa_ref, sb_ref, l_ref, acc_ref):
    blk = pl.program_id(2)
    nb = k_ref.shape[1]
    q_pair = q_ref[0]
    km = kmean_ref[0].astype(_BF16)
    kamax = kamax_ref[0].astype(_BF16)
    nidx = lax.broadcasted_iota(jnp.int32, (nb, MOBA_Q), 0)
    neg_inf = jnp.float32(-jnp.inf)
    slope = [alibi_ref[hh, 0, KV_BLOCK:KV_BLOCK + 1, :] for hh in range(2)]

    heads = []
    first_needed = jnp.int32(nb)
    for hh in range(2):
        qm = _head_rows(q_pair, hh)
        bound = jnp.dot(kamax, jnp.abs(qm), preferred_element_type=_F32)
        own = jnp.max(jnp.where(nidx == blk, bound, 0.0), axis=0, keepdims=True)
        far = slope[hh] * ((nidx + 1 - blk) * KV_BLOCK).astype(_F32)
        needed = (nidx < blk) & (bound + own + far > MOBA_SKIP)
        first_needed = jnp.minimum(first_needed, jnp.min(jnp.where(needed, nidx, nb)))
        gate = jnp.dot(km, qm, preferred_element_type=_F32)
        g = jnp.where(nidx < blk, gate, neg_inf)
        sel = nidx == blk
        for r in range(MOBA_TOPK):
            mx = jnp.max(g, axis=0, keepdims=True)
            first = jnp.min(jnp.where(g == mx, nidx, nb), axis=0, keepdims=True)
            hit = nidx == first
            sel = sel | (hit & (jnp.full((nb, MOBA_Q), r, jnp.int32) < blk))
            g = jnp.where(hit, neg_inf, g)
        sel_ref[hh] = jnp.where(sel, 0.0, neg_inf)
        heads.append(qm)

    def shift(j):
        return jnp.full((1, MOBA_Q), (j - blk) * KV_BLOCK, jnp.int32).astype(_F32)

    def row_bias(hh, j):
        return sel_ref[hh, pl.ds(j, 1), :] + slope[hh] * shift(j)

    def scores(t, dst_ref):
        mx = []
        for hh in range(2):
            col = jnp.full((1, MOBA_Q), neg_inf, _F32)
            for g in range(MOBA_G):
                j = t * MOBA_G + g
                ramp = alibi_ref[hh, (j == blk).astype(jnp.int32), 0:KV_BLOCK, :]
                s = jnp.dot(k_ref[0, j], heads[hh], preferred_element_type=_F32) + ramp
                dst_ref[hh, g * KV_BLOCK:(g + 1) * KV_BLOCK, :] = s
                col = jnp.maximum(col, jnp.max(s, axis=0, keepdims=True) + row_bias(hh, j))
            mx.append(col)
        return mx

    def accumulate(t, src_ref, mx, st):
        out = []
        for hh in range(2):
            m, l = st[2 * hh:2 * hh + 2]
            m_new = jnp.maximum(m, mx[hh])
            alpha = jnp.exp2(m - m_new)
            l = alpha * l
            acc = alpha * acc_ref[hh]
            for g in range(MOBA_G):
                j = t * MOBA_G + g
                s = src_ref[hh, g * KV_BLOCK:(g + 1) * KV_BLOCK, :]
                p = jnp.exp2(s - (m_new - row_bias(hh, j)))
                l = l + jnp.sum(p, axis=0, keepdims=True)
                acc = acc + jnp.dot(v_ref[0, j, hh * HEAD_DIM:(hh + 1) * HEAD_DIM, :],
                                    p.astype(_BF16), preferred_element_type=_F32)
            acc_ref[hh] = acc
            out += [m_new, l]
        return out

    last = blk // MOBA_G
    first = jnp.minimum(first_needed // MOBA_G, last)
    ntiles = last - first + 1
    state = []
    for hh in range(2):
        state += [jnp.full((1, MOBA_Q), jnp.finfo(_F32).min, _F32), jnp.zeros((1, MOBA_Q), _F32)]
        acc_ref[hh] = jnp.zeros((HEAD_DIM, MOBA_Q), _F32)

    def body(k, carry):
        st, mx_a = carry[:4], carry[4:]
        t = last - 2 * k
        mx_b = scores(t - 1, sb_ref)
        st = accumulate(t, sa_ref, mx_a, st)
        mx_a = scores(jnp.maximum(t - 2, 0), sa_ref)
        st = accumulate(t - 1, sb_ref, mx_b, st)
        return tuple(st) + tuple(mx_a)

    carry = lax.fori_loop(0, ntiles // 2, body, tuple(state) + tuple(scores(last, sa_ref)))
    st, mx_a = carry[:4], carry[4:]
    for hh in range(2):
        l_ref[hh] = st[2 * hh + 1]

    @pl.when(ntiles % 2 == 1)
    def _():
        odd = accumulate(first, sa_ref, mx_a, st)
        for hh in range(2):
            l_ref[hh] = odd[2 * hh + 1]

    o_t = jnp.concatenate([acc_ref[hh] / l_ref[hh] for hh in range(2)], axis=0)
    o_ref[0] = o_t.T.astype(_BF16)


def _moba(qa_t, ka, va_t, kmean, kamax, alibi):
    b, _, s = qa_t.shape
    nb = s // KV_BLOCK
    npair = N_HEADS // 2
    assert nb % MOBA_G == 0
    return pl.pallas_call(
        _moba_kernel,
        grid=(b, npair, s // MOBA_Q),
        in_specs=[
            pl.BlockSpec((1, PAIR, MOBA_Q), lambda bi, hp, c: (bi, hp, c)),
            pl.BlockSpec((1, nb, KV_BLOCK, PAIR), lambda bi, hp, c: (bi, 0, 0, hp)),
            pl.BlockSpec((1, nb, PAIR, KV_BLOCK), lambda bi, hp, c: (bi, 0, hp, 0)),
            pl.BlockSpec((1, nb, PAIR), lambda bi, hp, c: (bi, 0, hp)),
            pl.BlockSpec((1, nb, PAIR), lambda bi, hp, c: (bi, 0, hp)),
            pl.BlockSpec((2, 2, KV_BLOCK + SUBLANES, MOBA_Q), lambda bi, hp, c: (hp, 0, 0, 0)),
        ],
        out_specs=pl.BlockSpec((1, MOBA_Q, PAIR), lambda bi, hp, c: (bi, c, hp)),
        out_shape=jax.ShapeDtypeStruct((b, s, WIDTH), _BF16),
        scratch_shapes=[pltpu.VMEM((2, nb, MOBA_Q), _F32),
                        pltpu.VMEM((2, MOBA_G * KV_BLOCK, MOBA_Q), _F32),
                        pltpu.VMEM((2, MOBA_G * KV_BLOCK, MOBA_Q), _F32),
                        pltpu.VMEM((2, 1, MOBA_Q), _F32),
                        pltpu.VMEM((2, HEAD_DIM, MOBA_Q), _F32)],
        compiler_params=_params("arbitrary", "arbitrary", "arbitrary"),
        name="moba",
    )(qa_t, ka, va_t, kmean, kamax, alibi)


def _suffix_product_over_sublanes(x):
    sub = lax.broadcasted_iota(jnp.int32, x.shape, 0)
    y = x
    for d in (1, 2, 4):
        up = pltpu.roll(y, SUBLANES - d, 0)
        y = y * jnp.where(sub < SUBLANES - d, up, 1.0)
    return y


def _sb_scores(k_j, qm, mask):
    z = jnp.dot(k_j, qm, preferred_element_type=_F32)
    keep = 1.0 / (1.0 + jnp.exp2(z))
    if mask is not None:
        keep = jnp.where(mask, keep, 1.0)
    run = jnp.ones((SUBLANES, z.shape[1]), _F32)
    diff = [None] * RUN
    for r in reversed(range(RUN)):
        nxt = run * keep[r * SUBLANES:(r + 1) * SUBLANES, :]
        diff[r] = run - nxt
        run = nxt
    sub = lax.broadcasted_iota(jnp.int32, run.shape, 0)
    shifted = jnp.where(sub < SUBLANES - 1, pltpu.roll(run, SUBLANES - 1, 0), 1.0)
    return diff, run, _suffix_product_over_sublanes(shifted)


def _sb_weights(scores, v_jh, carry, acc):
    diff, run, later = scores
    base = later * carry
    w = jnp.concatenate([d * base for d in diff], axis=0)
    acc = acc + jnp.dot(v_jh, w.astype(_BF16), preferred_element_type=_F32)
    return carry * (later[0:1, :] * run[0:1, :]), acc


def _sb_kernel(q_ref, k_ref, v_ref, o_ref):
    jd = pl.program_id(2)
    q_pair = q_ref[0]
    rho = lax.broadcasted_iota(jnp.int32, (KV_BLOCK, SB_Q), 0)
    kpos = (rho & (SUBLANES - 1)) * RUN + (rho >> 3)
    qpos = lax.broadcasted_iota(jnp.int32, (KV_BLOCK, SB_Q), 1)
    strict = kpos < qpos
    heads = [_head_rows(q_pair, hh) for hh in range(2)]
    rows = [slice(hh * HEAD_DIM, (hh + 1) * HEAD_DIM) for hh in range(2)]

    jp = jnp.maximum(jd - 1, 0)
    state = []
    diags = [_sb_scores(k_ref[0, jd], heads[hh], strict) for hh in range(2)]
    prevs = [_sb_scores(k_ref[0, jp], heads[hh], None) for hh in range(2)]
    for hh in range(2):
        carry, acc = _sb_weights(diags[hh], v_ref[0, jd, rows[hh], :],
                                 jnp.ones((1, SB_Q), _F32),
                                 jnp.zeros((HEAD_DIM, SB_Q), _F32))
        carry = jnp.where(jd > 0, carry, 0.0)
        state += list(_sb_weights(prevs[hh], v_ref[0, jp, rows[hh], :], carry, acc))

    def cond(st):
        j, c0, _, c1, _ = st
        return jnp.logical_and(j >= 0, jnp.maximum(jnp.max(c0), jnp.max(c1)) > SB_STOP)

    def body(st):
        j = st[0]
        out = []
        for hh in range(2):
            carry, acc = st[1 + 2 * hh:3 + 2 * hh]
            out += list(_sb_weights(_sb_scores(k_ref[0, j], heads[hh], None),
                                    v_ref[0, j, rows[hh], :], carry, acc))
        return (j - 1, *out)

    st = lax.while_loop(cond, body, (jd - 2, *state))
    o_ref[0] = jnp.concatenate([st[2], st[4]], axis=0).T.astype(_BF16)


def _stick_breaking(qs_t, ks, vs_t):
    b, _, s = qs_t.shape
    nb = s // KV_BLOCK
    npair = N_HEADS // 2
    return pl.pallas_call(
        _sb_kernel,
        grid=(b, npair, s // SB_Q),
        in_specs=[
            pl.BlockSpec((1, PAIR, SB_Q), lambda bi, hp, i: (bi, hp, i)),
            pl.BlockSpec((1, nb, KV_BLOCK, PAIR), lambda bi, hp, i: (bi, 0, 0, hp)),
            pl.BlockSpec((1, nb, PAIR, KV_BLOCK), lambda bi, hp, i: (bi, 0, hp, 0)),
        ],
        out_specs=pl.BlockSpec((1, SB_Q, PAIR), lambda bi, hp, i: (bi, i, hp)),
        out_shape=jax.ShapeDtypeStruct((b, s, WIDTH), _BF16),
        compiler_params=_params("arbitrary", "arbitrary", "arbitrary"),
        name="stick_breaking",
    )(qs_t, ks, vs_t)


MIX_T = 512
HALO = 16


def _mix_kernel(oa_ref, ob_ref, rest_ref, halo_ref, x_ref, convw_ref, bg_ref,
                wpa_ref, wpb_ref, wpc_ref, wout_ref, o_ref):
    i = pl.program_id(1)
    r = rest_ref[0]
    xc = r[:, 0:WIDTH].astype(_F32)
    bc = r[:, WIDTH:2 * WIDTH].astype(_F32)
    cc = r[:, 2 * WIDTH:3 * WIDTH].astype(_F32)
    u = cc * xc
    hl = halo_ref[0]
    uh = hl[:, 2 * WIDTH:3 * WIDTH].astype(_F32) * hl[:, 0:WIDTH].astype(_F32)
    uh = jnp.where(i > 0, uh, 0.0)
    prev1 = uh[HALO - 1:HALO, :]
    prev2 = uh[HALO - 2:HALO - 1, :]
    row = lax.broadcasted_iota(jnp.int32, u.shape, 0)
    u1 = jnp.where(row == 0, prev1, pltpu.roll(u, 1, 0))
    u2 = jnp.where(row == 0, prev2, jnp.where(row == 1, prev1, pltpu.roll(u, 2, 0)))
    w = convw_ref[...]
    y = bc * (u2 * w[0:1, :] + u1 * w[1:2, :] + u * w[2:3, :])
    g = jax.nn.sigmoid(r[:, 3 * WIDTH:].astype(_F32) + bg_ref[...])
    d = D_MODEL
    merged = (g[:, 0:d] * jnp.dot(oa_ref[0], wpa_ref[...], preferred_element_type=_F32)
              + g[:, d:2 * d] * jnp.dot(ob_ref[0], wpb_ref[...], preferred_element_type=_F32)
              + g[:, 2 * d:] * jnp.dot(y.astype(_BF16), wpc_ref[...],
                                       preferred_element_type=_F32))
    o_ref[0] = x_ref[0] + jnp.dot(merged.astype(_BF16), wout_ref[...],
                                  preferred_element_type=_F32)


def _mix(oa, ob, rest, x, convw, bg, wpa, wpb, wpc, wout):
    b, s, d = x.shape
    t = MIX_T
    tile = lambda width: pl.BlockSpec((1, t, width), lambda bi, i: (bi, i, 0))
    halo = pl.BlockSpec((1, HALO, REST_WIDTH),
                        lambda bi, i: (bi, jnp.maximum(i * (t // HALO) - 1, 0), 0))
    return pl.pallas_call(
        _mix_kernel,
        grid=(b, s // t),
        in_specs=[tile(WIDTH), tile(WIDTH), tile(REST_WIDTH), halo, tile(d),
                  _const_spec((CONV_K, WIDTH)), _const_spec((1, 3 * d)),
                  _const_spec((WIDTH, d)), _const_spec((WIDTH, d)), _const_spec((WIDTH, d)),
                  _const_spec((d, d))],
        out_specs=tile(d),
        out_shape=jax.ShapeDtypeStruct((b, s, d), _F32),
        compiler_params=_params("arbitrary", "arbitrary"),
        name="mix",
    )(oa, ob, rest, rest, x, convw, bg, wpa, wpb, wpc, wout)


FFN_T = 512
FF_CHUNK = 256


def _ffn_up_kernel(x_ref, g_ref, wg_ref, wu_ref, o_ref):
    h = _rms(x_ref[...], g_ref[...]).astype(_BF16)
    for n in range(D_FF // FF_CHUNK):
        cols = slice(n * FF_CHUNK, (n + 1) * FF_CHUNK)
        a = jnp.dot(h, wg_ref[:, cols], preferred_element_type=_F32)
        bgate = jnp.dot(h, wu_ref[:, cols], preferred_element_type=_F32)
        o_ref[:, cols] = (a * jax.nn.sigmoid(a) * bgate).astype(_BF16)


def _ffn_down_kernel(a_ref, x_ref, wd_ref, g_ref, o_ref, *, final_norm):
    y = x_ref[...] + jnp.dot(a_ref[...], wd_ref[...], preferred_element_type=_F32)
    if final_norm:
        y = _rms(y, g_ref[...])
    o_ref[...] = y


def _ffn(x2d, g, wg, wu, wd, gfinal, final_norm):
    n, d = x2d.shape
    t = FFN_T
    act = pl.pallas_call(
        _ffn_up_kernel,
        grid=(n // t,),
        in_specs=[pl.BlockSpec((t, d), lambda i: (i, 0)), _const_spec((1, d)),
                  _const_spec((d, D_FF)), _const_spec((d, D_FF))],
        out_specs=pl.BlockSpec((t, D_FF), lambda i: (i, 0)),
        out_shape=jax.ShapeDtypeStruct((n, D_FF), _BF16),
        compiler_params=_params("arbitrary"),
        name="ffn_up",
    )(x2d, g, wg, wu)
    return pl.pallas_call(
        functools.partial(_ffn_down_kernel, final_norm=final_norm),
        grid=(n // t,),
        in_specs=[pl.BlockSpec((t, D_FF), lambda i: (i, 0)),
                  pl.BlockSpec((t, d), lambda i: (i, 0)),
                  _const_spec((D_FF, d)), _const_spec((1, d))],
        out_specs=pl.BlockSpec((t, d), lambda i: (i, 0)),
        out_shape=jax.ShapeDtypeStruct((n, d), _F32),
        compiler_params=_params("arbitrary"),
        name="ffn_down",
    )(act, x2d, wd, gfinal)


def _sb_permutation():
    rho = np.arange(KV_BLOCK)
    kappa = (rho % SUBLANES) * RUN + rho // SUBLANES
    p = np.zeros((KV_BLOCK, KV_BLOCK), np.float32)
    p[rho, kappa] = 1.0
    return jnp.asarray(p, _BF16)


def _alibi_table():
    slopes = np.exp2(-8.0 * (np.arange(N_HEADS, dtype=np.float64) + 1.0) / N_HEADS) * LOG2E
    rows = np.concatenate([np.arange(KV_BLOCK, dtype=np.float64), np.ones(SUBLANES)])
    tab = slopes[:, None, None] * rows[None, :, None] * np.ones((1, 1, MOBA_Q))
    key = np.arange(KV_BLOCK + SUBLANES)[:, None]
    future = (key > np.arange(MOBA_Q)[None, :]) & (key < KV_BLOCK)
    tab = np.stack([tab, np.where(future[None], -np.inf, tab)], axis=1)
    return jnp.asarray(tab.astype(np.float32))


def kernel(x, norm_mix_g, w_in, b_gate, conv_w, w_proj_moba, w_proj_sb, w_proj_conv, w_out,
           norm_ffn_g, w_ffn_gate, w_ffn_up, w_ffn_down, norm_final_g):
    depth = w_in.shape[0]
    b, s, d = x.shape
    scale = HEAD_DIM ** -0.5
    perm = _sb_permutation()
    alibi = _alibi_table()
    w = WIDTH
    for l in range(depth):
        wl = w_in[l]
        cols = lambda a, n=w: wl[:, a:a + n].astype(_BF16)
        wqa = (wl[:, 0:w] * (scale * LOG2E)).T.astype(_BF16)
        wka = cols(w)
        wva = wl[:, 2 * w:3 * w].T.astype(_BF16)
        wqs = (wl[:, 3 * w:4 * w] * (scale * LOG2E)).T.astype(_BF16)
        wks = cols(4 * w)
        wvs = wl[:, 5 * w:6 * w].T.astype(_BF16)
        wrest = cols(6 * w, REST_WIDTH)
        qa_t, ka, va_t, kmean, kamax, qs_t, ks, vs_t, rest = _in_proj(
            x, norm_mix_g[l][None, :], perm, wqa, wka, wva, wqs, wks, wvs, wrest)
        oa = _moba(qa_t, ka, va_t, kmean, kamax, alibi)
        ob = _stick_breaking(qs_t, ks, vs_t)
        x = _mix(oa, ob, rest, x, conv_w[l], b_gate[l][None, :],
                 w_proj_moba[l].astype(_BF16), w_proj_sb[l].astype(_BF16),
                 w_proj_conv[l].astype(_BF16), w_out[l].astype(_BF16))
        x = _ffn(x.reshape(b * s, d), norm_ffn_g[l][None, :], w_ffn_gate[l].astype(_BF16),
                 w_ffn_up[l].astype(_BF16), w_ffn_down[l].astype(_BF16),
                 norm_final_g[None, :], final_norm=(l == depth - 1)).reshape(b, s, d)
    return x
```

```python
import functools

import jax
import jax.numpy as jnp
import numpy as np
from jax import lax
from jax.experimental import pallas as pl
from jax.experimental.pallas import tpu as pltpu

D_MODEL = 1024
HEAD_DIM = 64
N_HEADS = 8
WIDTH = N_HEADS * HEAD_DIM
CONV_K = 3
KV_BLOCK = 256
MOBA_TOPK = 3
MOBA_Q = 256
MOBA_G = 2
SB_Q = KV_BLOCK
MOBA_SKIP = -150.0
D_FF = 2816
RMS_EPS = 1e-6
REST_WIDTH = 3 * WIDTH + 3 * D_MODEL
PAIR = 2 * HEAD_DIM
SUBLANES = 8
LOG2E = 1.4426950408889634
RUN = KV_BLOCK // SUBLANES
SB_STOP = 1e-30
VMEM_LIMIT = 56 * 1024 * 1024

_NT = (((1,), (1,)), ((), ()))
_TN = (((0,), (0,)), ((), ()))
_F32 = jnp.float32
_BF16 = jnp.bfloat16


def _params(*sem):
    return pltpu.CompilerParams(dimension_semantics=sem, vmem_limit_bytes=VMEM_LIMIT)


def _const_spec(shape):
    zeros = (0,) * len(shape)
    return pl.BlockSpec(shape, lambda *_: zeros)


def _rms(x, g):
    y = x * lax.rsqrt(jnp.mean(x * x, axis=-1, keepdims=True) + RMS_EPS)
    return y * g


def _in_proj_kernel(x_ref, g_ref, perm_ref, wqa_ref, wka_ref, wva_ref, wqs_ref, wks_ref,
                    wvs_ref, wrest_ref,
                    qa_ref, ka_ref, va_ref, kmean_ref, kamax_ref, qs_ref, ks_ref, vs_ref,
                    rest_ref):
    i = pl.program_id(1)
    h = _rms(x_ref[0], g_ref[...]).astype(_BF16)
    qa_ref[0] = lax.dot_general(wqa_ref[...], h, _NT,
                                preferred_element_type=_F32).astype(_BF16)
    ka = jnp.dot(h, wka_ref[...], preferred_element_type=_F32)
    ka_b = ka.astype(_BF16)
    ka_ref[0, 0] = ka_b
    kmean_ref[0, pl.ds(i, 1), :] = jnp.mean(ka, axis=0, keepdims=True)
    kamax_ref[0, pl.ds(i, 1), :] = jnp.max(jnp.abs(ka_b.astype(_F32)), axis=0, keepdims=True)
    va_ref[0, 0] = lax.dot_general(wva_ref[...], h, _NT,
                                   preferred_element_type=_F32).astype(_BF16)
    qs_ref[0] = lax.dot_general(wqs_ref[...], h, _NT,
                                preferred_element_type=_F32).astype(_BF16)
    hp = jnp.dot(perm_ref[...], h, preferred_element_type=_F32).astype(_BF16)
    ks_ref[0, 0] = jnp.dot(hp, wks_ref[...], preferred_element_type=_F32).astype(_BF16)
    vs_ref[0, 0] = lax.dot_general(wvs_ref[...], hp, _NT,
                                   preferred_element_type=_F32).astype(_BF16)
    rest_ref[0] = jnp.dot(h, wrest_ref[...], preferred_element_type=_F32).astype(_BF16)


def _in_proj(x, g, perm, wqa, wka, wva, wqs, wks, wvs, wrest):
    b, s, d = x.shape
    nb = s // KV_BLOCK
    t = KV_BLOCK
    out_shape = (
        jax.ShapeDtypeStruct((b, WIDTH, s), _BF16),
        jax.ShapeDtypeStruct((b, nb, t, WIDTH), _BF16),
        jax.ShapeDtypeStruct((b, nb, WIDTH, t), _BF16),
        jax.ShapeDtypeStruct((b, nb, WIDTH), _F32),
        jax.ShapeDtypeStruct((b, nb, WIDTH), _F32),
        jax.ShapeDtypeStruct((b, WIDTH, s), _BF16),
        jax.ShapeDtypeStruct((b, nb, t, WIDTH), _BF16),
        jax.ShapeDtypeStruct((b, nb, WIDTH, t), _BF16),
        jax.ShapeDtypeStruct((b, s, REST_WIDTH), _BF16),
    )
    qt_spec = pl.BlockSpec((1, WIDTH, t), lambda bi, i: (bi, 0, i))
    k_spec = pl.BlockSpec((1, 1, t, WIDTH), lambda bi, i: (bi, i, 0, 0))
    vt_spec = pl.BlockSpec((1, 1, WIDTH, t), lambda bi, i: (bi, i, 0, 0))
    return pl.pallas_call(
        _in_proj_kernel,
        grid=(b, nb),
        in_specs=[
            pl.BlockSpec((1, t, d), lambda bi, i: (bi, i, 0)),
            _const_spec((1, d)),
            _const_spec((t, t)),
            _const_spec((WIDTH, d)), _const_spec((d, WIDTH)), _const_spec((WIDTH, d)),
            _const_spec((WIDTH, d)), _const_spec((d, WIDTH)), _const_spec((WIDTH, d)),
            _const_spec((d, REST_WIDTH)),
        ],
        out_specs=(
            qt_spec, k_spec, vt_spec,
            pl.BlockSpec((1, nb, WIDTH), lambda bi, i: (bi, 0, 0)),
            pl.BlockSpec((1, nb, WIDTH), lambda bi, i: (bi, 0, 0)),
            qt_spec, k_spec, vt_spec,
            pl.BlockSpec((1, t, REST_WIDTH), lambda bi, i: (bi, i, 0)),
        ),
        out_shape=out_shape,
        compiler_params=_params("arbitrary", "arbitrary"),
        name="in_proj",
    )(x, g, perm, wqa, wka, wva, wqs, wks, wvs, wrest)


def _head_rows(q_pair, hh):
    row = lax.broadcasted_iota(jnp.int32, q_pair.shape, 0)
    keep = (row >= hh * HEAD_DIM) & (row < (hh + 1) * HEAD_DIM)
    return jnp.where(keep, q_pair, jnp.zeros_like(q_pair))


def _moba_kernel(q_ref, k_ref, v_ref, kmean_ref, kamax_ref, alibi_ref, o_ref,
                 sel_ref, sa_ref, sb_ref, l_ref, acc_ref):
    blk = pl.program_id(2)
    nb = k_ref.shape[1]
    q_pair = q_ref[0]
    km = kmean_ref[0].astype(_BF16)
    kamax = kamax_ref[0].astype(_BF16)
    nidx = lax.broadcasted_iota(jnp.int32, (nb, MOBA_Q), 0)
    neg_inf = jnp.float32(-jnp.inf)
    slope = [alibi_ref[hh, 0, KV_BLOCK:KV_BLOCK + 1, :] for hh in range(2)]

    heads = []
    first_needed = jnp.int32(nb)
    for hh in range(2):
        qm = _head_rows(q_pair, hh)
        bound = jnp.dot(kamax, jnp.abs(qm), preferred_element_type=_F32)
        own = jnp.max(jnp.where(nidx == blk, bound, 0.0), axis=0, keepdims=True)
        far = slope[hh] * ((nidx + 1 - blk) * KV_BLOCK).astype(_F32)
        needed = (nidx < blk) & (bound + own + far > MOBA_SKIP)
        first_needed = jnp.minimum(first_needed, jnp.min(jnp.where(needed, nidx, nb)))
        gate = jnp.dot(km, qm, preferred_element_type=_F32)
        g = jnp.where(nidx < blk, gate, neg_inf)
        sel = nidx == blk
        for r in range(MOBA_TOPK):
            mx = jnp.max(g, axis=0, keepdims=True)
            first = jnp.min(jnp.where(g == mx, nidx, nb), axis=0, keepdims=True)
            hit = nidx == first
            sel = sel | (hit & (jnp.full((nb, MOBA_Q), r, jnp.int32) < blk))
            g = jnp.where(hit, neg_inf, g)
        sel_ref[hh] = jnp.where(sel, 0.0, neg_inf)
        heads.append(qm)

    def shift(j):
        return jnp.full((1, MOBA_Q), (j - blk) * KV_BLOCK, jnp.int32).astype(_F32)

    def row_bias(hh, j):
        return sel_ref[hh, pl.ds(j, 1), :] + slope[hh] * shift(j)

    def scores(t, dst_ref):
        mx = []
        for hh in range(2):
            col = jnp.full((1, MOBA_Q), neg_inf, _F32)
            for g in range(MOBA_G):
                j = t * MOBA_G + g
                ramp = alibi_ref[hh, (j == blk).astype(jnp.int32), 0:KV_BLOCK, :]
                s = jnp.dot(k_ref[0, j], heads[hh], preferred_element_type=_F32) + ramp
                dst_ref[hh, g * KV_BLOCK:(g + 1) * KV_BLOCK, :] = s
                col = jnp.maximum(col, jnp.max(s, axis=0, keepdims=True) + row_bias(hh, j))
            mx.append(col)
        return mx

    def accumulate(t, src_ref, mx, st):
        out = []
        for hh in range(2):
            m, l = st[2 * hh:2 * hh + 2]
            m_new = jnp.maximum(m, mx[hh])
            alpha = jnp.exp2(m - m_new)
            l = alpha * l
            acc = alpha * acc_ref[hh]
            for g in range(MOBA_G):
                j = t * MOBA_G + g
                s = src_ref[hh, g * KV_BLOCK:(g + 1) * KV_BLOCK, :]
                p = jnp.exp2(s - (m_new - row_bias(hh, j)))
                l = l + jnp.sum(p, axis=0, keepdims=True)
                acc = acc + jnp.dot(v_ref[0, j, hh * HEAD_DIM:(hh + 1) * HEAD_DIM, :],
                                    p.astype(_BF16), preferred_element_type=_F32)
            acc_ref[hh] = acc
            out += [m_new, l]
        return out

    last = blk // MOBA_G
    first = jnp.minimum(first_needed // MOBA_G, last)
    ntiles = last - first + 1
    state = []
    for hh in range(2):
        state += [jnp.full((1, MOBA_Q), jnp.finfo(_F32).min, _F32), jnp.zeros((1, MOBA_Q), _F32)]
        acc_ref[hh] = jnp.zeros((HEAD_DIM, MOBA_Q), _F32)

    mx_a = scores(last, sa_ref)
    mx_b = scores(jnp.maximum(last - 1, 0), sb_ref)
    state = accumulate(last, sa_ref, mx_a, state)

    def body(k, carry):
        st, mx_b = carry[:4], carry[4:]
        t = last - 1 - 2 * k
        mx_a = scores(t - 1, sa_ref)
        st = accumulate(t, sb_ref, mx_b, st)
        mx_b = scores(jnp.maximum(t - 2, 0), sb_ref)
        st = accumulate(t - 1, sa_ref, mx_a, st)
        return tuple(st) + tuple(mx_b)

    carry = lax.fori_loop(0, (ntiles - 1) // 2, body, tuple(state) + tuple(mx_b))
    st, mx_b = carry[:4], carry[4:]
    for hh in range(2):
        l_ref[hh] = st[2 * hh + 1]

    @pl.when(ntiles % 2 == 0)
    def _():
        odd = accumulate(first, sb_ref, mx_b, st)
        for hh in range(2):
            l_ref[hh] = odd[2 * hh + 1]

    o_t = jnp.concatenate([acc_ref[hh] / l_ref[hh] for hh in range(2)], axis=0)
    o_ref[0] = o_t.astype(_BF16)


def _moba(qa_t, ka, va_t, kmean, kamax, alibi):
    b, _, s = qa_t.shape
    nb = s // KV_BLOCK
    npair = N_HEADS // 2
    assert nb % MOBA_G == 0
    return pl.pallas_call(
        _moba_kernel,
        grid=(b, npair, s // MOBA_Q),
        in_specs=[
            pl.BlockSpec((1, PAIR, MOBA_Q), lambda bi, hp, c: (bi, hp, c)),
            pl.BlockSpec((1, nb, KV_BLOCK, PAIR), lambda bi, hp, c: (bi, 0, 0, hp)),
            pl.BlockSpec((1, nb, PAIR, KV_BLOCK), lambda bi, hp, c: (bi, 0, hp, 0)),
            pl.BlockSpec((1, nb, PAIR), lambda bi, hp, c: (bi, 0, hp)),
            pl.BlockSpec((1, nb, PAIR), lambda bi, hp, c: (bi, 0, hp)),
            pl.BlockSpec((2, 2, KV_BLOCK + SUBLANES, MOBA_Q), lambda bi, hp, c: (hp, 0, 0, 0)),
        ],
        out_specs=pl.BlockSpec((1, PAIR, MOBA_Q), lambda bi, hp, c: (bi, hp, c)),
        out_shape=jax.ShapeDtypeStruct((b, WIDTH, s), _BF16),
        scratch_shapes=[pltpu.VMEM((2, nb, MOBA_Q), _F32),
                        pltpu.VMEM((2, MOBA_G * KV_BLOCK, MOBA_Q), _F32),
                        pltpu.VMEM((2, MOBA_G * KV_BLOCK, MOBA_Q), _F32),
                        pltpu.VMEM((2, 1, MOBA_Q), _F32),
                        pltpu.VMEM((2, HEAD_DIM, MOBA_Q), _F32)],
        compiler_params=_params("arbitrary", "arbitrary", "arbitrary"),
        name="moba",
    )(qa_t, ka, va_t, kmean, kamax, alibi)


def _suffix_product_over_sublanes(x):
    sub = lax.broadcasted_iota(jnp.int32, x.shape, 0)
    y = x
    for d in (1, 2, 4):
        up = pltpu.roll(y, SUBLANES - d, 0)
        y = y * jnp.where(sub < SUBLANES - d, up, 1.0)
    return y


def _sb_scores(k_j, qm, mask):
    z = jnp.dot(k_j, qm, preferred_element_type=_F32)
    keep = 1.0 / (1.0 + jnp.exp2(z))
    if mask is not None:
        keep = jnp.where(mask, keep, 1.0)
    run = jnp.ones((SUBLANES, z.shape[1]), _F32)
    diff = [None] * RUN
    for r in reversed(range(RUN)):
        nxt = run * keep[r * SUBLANES:(r + 1) * SUBLANES, :]
        diff[r] = run - nxt
        run = nxt
    sub = lax.broadcasted_iota(jnp.int32, run.shape, 0)
    shifted = jnp.where(sub < SUBLANES - 1, pltpu.roll(run, SUBLANES - 1, 0), 1.0)
    return diff, run, _suffix_product_over_sublanes(shifted)


def _sb_weights(scores, v_jh, carry, acc):
    diff, run, later = scores
    base = later * carry
    w = jnp.concatenate([d * base for d in diff], axis=0)
    acc = acc + jnp.dot(v_jh, w.astype(_BF16), preferred_element_type=_F32)
    return carry * (later[0:1, :] * run[0:1, :]), acc


def _sb_kernel(q_ref, k_ref, v_ref, o_ref):
    jd = pl.program_id(2)
    q_pair = q_ref[0]
    rho = lax.broadcasted_iota(jnp.int32, (KV_BLOCK, SB_Q), 0)
    kpos = (rho & (SUBLANES - 1)) * RUN + (rho >> 3)
    qpos = lax.broadcasted_iota(jnp.int32, (KV_BLOCK, SB_Q), 1)
    strict = kpos < qpos
    heads = [_head_rows(q_pair, hh) for hh in range(2)]
    rows = [slice(hh * HEAD_DIM, (hh + 1) * HEAD_DIM) for hh in range(2)]

    jp = jnp.maximum(jd - 1, 0)
    state = []
    diags = [_sb_scores(k_ref[0, jd], heads[hh], strict) for hh in range(2)]
    prevs = [_sb_scores(k_ref[0, jp], heads[hh], None) for hh in range(2)]
    for hh in range(2):
        carry, acc = _sb_weights(diags[hh], v_ref[0, jd, rows[hh], :],
                                 jnp.ones((1, SB_Q), _F32),
                                 jnp.zeros((HEAD_DIM, SB_Q), _F32))
        carry = jnp.where(jd > 0, carry, 0.0)
        state += list(_sb_weights(prevs[hh], v_ref[0, jp, rows[hh], :], carry, acc))

    def cond(st):
        j, c0, _, c1, _ = st
        return jnp.logical_and(j >= 0, jnp.maximum(jnp.max(c0), jnp.max(c1)) > SB_STOP)

    def body(st):
        j = st[0]
        out = []
        for hh in range(2):
            carry, acc = st[1 + 2 * hh:3 + 2 * hh]
            out += list(_sb_weights(_sb_scores(k_ref[0, j], heads[hh], None),
                                    v_ref[0, j, rows[hh], :], carry, acc))
        return (j - 1, *out)

    st = lax.while_loop(cond, body, (jd - 2, *state))
    o_ref[0] = jnp.concatenate([st[2], st[4]], axis=0).astype(_BF16)


def _stick_breaking(qs_t, ks, vs_t):
    b, _, s = qs_t.shape
    nb = s // KV_BLOCK
    npair = N_HEADS // 2
    return pl.pallas_call(
        _sb_kernel,
        grid=(b, npair, s // SB_Q),
        in_specs=[
            pl.BlockSpec((1, PAIR, SB_Q), lambda bi, hp, i: (bi, hp, i)),
            pl.BlockSpec((1, nb, KV_BLOCK, PAIR), lambda bi, hp, i: (bi, 0, 0, hp)),
            pl.BlockSpec((1, nb, PAIR, KV_BLOCK), lambda bi, hp, i: (bi, 0, hp, 0)),
        ],
        out_specs=pl.BlockSpec((1, PAIR, SB_Q), lambda bi, hp, i: (bi, hp, i)),
        out_shape=jax.ShapeDtypeStruct((b, WIDTH, s), _BF16),
        compiler_params=_params("arbitrary", "arbitrary", "arbitrary"),
        name="stick_breaking",
    )(qs_t, ks, vs_t)


MIX_T = 512
HALO = 16


def _mix_kernel(oa_ref, ob_ref, rest_ref, halo_ref, x_ref, convw_ref, bg_ref,
                wpa_ref, wpb_ref, wpc_ref, wout_ref, o_ref):
    i = pl.program_id(1)
    r = rest_ref[0]
    xc = r[:, 0:WIDTH].astype(_F32)
    bc = r[:, WIDTH:2 * WIDTH].astype(_F32)
    cc = r[:, 2 * WIDTH:3 * WIDTH].astype(_F32)
    u = cc * xc
    hl = halo_ref[0]
    uh = hl[:, 2 * WIDTH:3 * WIDTH].astype(_F32) * hl[:, 0:WIDTH].astype(_F32)
    uh = jnp.where(i > 0, uh, 0.0)
    prev1 = uh[HALO - 1:HALO, :]
    prev2 = uh[HALO - 2:HALO - 1, :]
    row = lax.broadcasted_iota(jnp.int32, u.shape, 0)
    u1 = jnp.where(row == 0, prev1, pltpu.roll(u, 1, 0))
    u2 = jnp.where(row == 0, prev2, jnp.where(row == 1, prev1, pltpu.roll(u, 2, 0)))
    w = convw_ref[...]
    y = bc * (u2 * w[0:1, :] + u1 * w[1:2, :] + u * w[2:3, :])
    g = jax.nn.sigmoid(r[:, 3 * WIDTH:].astype(_F32) + bg_ref[...])
    d = D_MODEL
    merged = (g[:, 0:d] * lax.dot_general(oa_ref[0], wpa_ref[...], _TN,
                                          preferred_element_type=_F32)
              + g[:, d:2 * d] * lax.dot_general(ob_ref[0], wpb_ref[...], _TN,
                                                preferred_element_type=_F32)
              + g[:, 2 * d:] * jnp.dot(y.astype(_BF16), wpc_ref[...],
                                       preferred_element_type=_F32))
    o_ref[0] = x_ref[0] + jnp.dot(merged.astype(_BF16), wout_ref[...],
                                  preferred_element_type=_F32)


def _mix(oa, ob, rest, x, convw, bg, wpa, wpb, wpc, wout):
    b, s, d = x.shape
    t = MIX_T
    tile = lambda width: pl.BlockSpec((1, t, width), lambda bi, i: (bi, i, 0))
    tile_t = pl.BlockSpec((1, WIDTH, t), lambda bi, i: (bi, 0, i))
    halo = pl.BlockSpec((1, HALO, REST_WIDTH),
                        lambda bi, i: (bi, jnp.maximum(i * (t // HALO) - 1, 0), 0))
    return pl.pallas_call(
        _mix_kernel,
        grid=(b, s // t),
        in_specs=[tile_t, tile_t, tile(REST_WIDTH), halo, tile(d),
                  _const_spec((CONV_K, WIDTH)), _const_spec((1, 3 * d)),
                  _const_spec((WIDTH, d)), _const_spec((WIDTH, d)), _const_spec((WIDTH, d)),
                  _const_spec((d, d))],
        out_specs=tile(d),
        out_shape=jax.ShapeDtypeStruct((b, s, d), _F32),
        compiler_params=_params("arbitrary", "arbitrary"),
        name="mix",
    )(oa, ob, rest, rest, x, convw, bg, wpa, wpb, wpc, wout)


FFN_T = 512
FF_CHUNK = 256


def _ffn_up_kernel(x_ref, g_ref, wg_ref, wu_ref, o_ref):
    h = _rms(x_ref[...], g_ref[...]).astype(_BF16)
    for n in range(D_FF // FF_CHUNK):
        cols = slice(n * FF_CHUNK, (n + 1) * FF_CHUNK)
        a = jnp.dot(h, wg_ref[:, cols], preferred_element_type=_F32)
        bgate = jnp.dot(h, wu_ref[:, cols], preferred_element_type=_F32)
        o_ref[:, cols] = (a * jax.nn.sigmoid(a) * bgate).astype(_BF16)


def _ffn_down_kernel(a_ref, x_ref, wd_ref, g_ref, o_ref, *, final_norm):
    y = x_ref[...] + jnp.dot(a_ref[...], wd_ref[...], preferred_element_type=_F32)
    if final_norm:
        y = _rms(y, g_ref[...])
    o_ref[...] = y


def _ffn(x2d, g, wg, wu, wd, gfinal, final_norm):
    n, d = x2d.shape
    t = FFN_T
    act = pl.pallas_call(
        _ffn_up_kernel,
        grid=(n // t,),
        in_specs=[pl.BlockSpec((t, d), lambda i: (i, 0)), _const_spec((1, d)),
                  _const_spec((d, D_FF)), _const_spec((d, D_FF))],
        out_specs=pl.BlockSpec((t, D_FF), lambda i: (i, 0)),
        out_shape=jax.ShapeDtypeStruct((n, D_FF), _BF16),
        compiler_params=_params("arbitrary"),
        name="ffn_up",
    )(x2d, g, wg, wu)
    return pl.pallas_call(
        functools.partial(_ffn_down_kernel, final_norm=final_norm),
        grid=(n // t,),
        in_specs=[pl.BlockSpec((t, D_FF), lambda i: (i, 0)),
                  pl.BlockSpec((t, d), lambda i: (i, 0)),
                  _const_spec((D_FF, d)), _const_spec((1, d))],
        out_specs=pl.BlockSpec((t, d), lambda i: (i, 0)),
        out_shape=jax.ShapeDtypeStruct((n, d), _F32),
        compiler_params=_params("arbitrary"),
        name="ffn_down",
    )(act, x2d, wd, gfinal)


def _sb_permutation():
    rho = np.arange(KV_BLOCK)
    kappa = (rho % SUBLANES) * RUN + rho // SUBLANES
    p = np.zeros((KV_BLOCK, KV_BLOCK), np.float32)
    p[rho, kappa] = 1.0
    return jnp.asarray(p, _BF16)


def _alibi_table():
    slopes = np.exp2(-8.0 * (np.arange(N_HEADS, dtype=np.float64) + 1.0) / N_HEADS) * LOG2E
    rows = np.concatenate([np.arange(KV_BLOCK, dtype=np.float64), np.ones(SUBLANES)])
    tab = slopes[:, None, None] * rows[None, :, None] * np.ones((1, 1, MOBA_Q))
    key = np.arange(KV_BLOCK + SUBLANES)[:, None]
    future = (key > np.arange(MOBA_Q)[None, :]) & (key < KV_BLOCK)
    tab = np.stack([tab, np.where(future[None], -np.inf, tab)], axis=1)
    return jnp.asarray(tab.astype(np.float32))


def kernel(x, norm_mix_g, w_in, b_gate, conv_w, w_proj_moba, w_proj_sb, w_proj_conv, w_out,
           norm_ffn_g, w_ffn_gate, w_ffn_up, w_ffn_down, norm_final_g):
    depth = w_in.shape[0]
    b, s, d = x.shape
    scale = HEAD_DIM ** -0.5
    perm = _sb_permutation()
    alibi = _alibi_table()
    w = WIDTH
    for l in range(depth):
        wl = w_in[l]
        cols = lambda a, n=w: wl[:, a:a + n].astype(_BF16)
        wqa = (wl[:, 0:w] * (scale * LOG2E)).T.astype(_BF16)
        wka = cols(w)
        wva = wl[:, 2 * w:3 * w].T.astype(_BF16)
        wqs = (wl[:, 3 * w:4 * w] * (scale * LOG2E)).T.astype(_BF16)
        wks = cols(4 * w)
        wvs = wl[:, 5 * w:6 * w].T.astype(_BF16)
        wrest = cols(6 * w, REST_WIDTH)
        qa_t, ka, va_t, kmean, kamax, qs_t, ks, vs_t, rest = _in_proj(
            x, norm_mix_g[l][None, :], perm, wqa, wka, wva, wqs, wks, wvs, wrest)
        oa = _moba(qa_t, ka, va_t, kmean, kamax, alibi)
        ob = _stick_breaking(qs_t, ks, vs_t)
        x = _mix(oa, ob, rest, x, conv_w[l], b_gate[l][None, :],
                 w_proj_moba[l].astype(_BF16), w_proj_sb[l].astype(_BF16),
                 w_proj_conv[l].astype(_BF16), w_out[l].astype(_BF16))
        x = _ffn(x.reshape(b * s, d), norm_ffn_g[l][None, :], w_ffn_gate[l].astype(_BF16),
                 w_ffn_up[l].astype(_BF16), w_ffn_down[l].astype(_BF16),
                 norm_final_g[None, :], final_norm=(l == depth - 1)).reshape(b, s, d)
    return x
```

```python
import functools

import jax
import jax.numpy as jnp
import numpy as np
from jax import lax
from jax.experimental import pallas as pl
from jax.experimental.pallas import tpu as pltpu

D_MODEL = 1024
HEAD_DIM = 64
N_HEADS = 8
WIDTH = N_HEADS * HEAD_DIM
CONV_K = 3
KV_BLOCK = 256
MOBA_TOPK = 3
MOBA_Q = 256
MOBA_G = 2
SB_Q = KV_BLOCK
MOBA_SKIP = -150.0
D_FF = 2816
RMS_EPS = 1e-6
REST_WIDTH = 3 * WIDTH + 3 * D_MODEL
PAIR = 2 * HEAD_DIM
V_ROWS = HEAD_DIM + 16
N_AUG = 4
SUBLANES = 8
LOG2E = 1.4426950408889634
RUN = KV_BLOCK // SUBLANES
SB_STOP = 1e-30
VMEM_LIMIT = 56 * 1024 * 1024

_NT = (((1,), (1,)), ((), ()))
_TN = (((0,), (0,)), ((), ()))
_F32 = jnp.float32
_BF16 = jnp.bfloat16


def _params(*sem):
    return pltpu.CompilerParams(dimension_semantics=sem, vmem_limit_bytes=VMEM_LIMIT)


def _const_spec(shape):
    zeros = (0,) * len(shape)
    return pl.BlockSpec(shape, lambda *_: zeros)


def _rms(x, g):
    y = x * lax.rsqrt(jnp.mean(x * x, axis=-1, keepdims=True) + RMS_EPS)
    return y * g


def _in_proj_kernel(x_ref, g_ref, perm_ref, kpos_ref, wqa_ref, wka_ref, wva_ref, wqs_ref, wks_ref,
                    wvs_ref, wrest_ref,
                    qa_ref, ka_ref, va_ref, kmean_ref, kamax_ref, qs_ref, ks_ref, vs_ref,
                    rest_ref):
    i = pl.program_id(1)
    h = _rms(x_ref[0], g_ref[...]).astype(_BF16)
    qa_ref[0] = lax.dot_general(wqa_ref[...], h, _NT,
                                preferred_element_type=_F32).astype(_BF16)
    ka = jnp.dot(h, wka_ref[...], preferred_element_type=_F32)
    lane = lax.broadcasted_iota(jnp.int32, (KV_BLOCK, PAIR), 1)
    tiles = []
    for pair in range(N_HEADS // 2):
        both = ka[:, pair * PAIR:(pair + 1) * PAIR]
        tiles += [jnp.where(lane < HEAD_DIM, both, 0.0),
                  jnp.where(lane < HEAD_DIM, pltpu.roll(both, HEAD_DIM, 1), 0.0)]
    ka_wide = jnp.concatenate(tiles, axis=1)
    ka_ref[0, 0] = (ka_wide + kpos_ref[...]).astype(_BF16)
    kmean_ref[0, pl.ds(i, 1), :] = jnp.mean(ka_wide, axis=0, keepdims=True)
    kamax_ref[0, pl.ds(i, 1), :] = jnp.max(jnp.abs(ka_wide.astype(_BF16).astype(_F32)),
                                           axis=0, keepdims=True)
    va_t = lax.dot_general(wva_ref[...], h, _NT, preferred_element_type=_F32).astype(_BF16)
    extra = (lax.broadcasted_iota(jnp.int32, (V_ROWS - HEAD_DIM, KV_BLOCK), 0) == 0)
    extra = extra.astype(_F32).astype(_BF16)
    va_ref[0, 0] = jnp.concatenate(
        [piece for hd in range(N_HEADS)
         for piece in (va_t[hd * HEAD_DIM:(hd + 1) * HEAD_DIM, :], extra)], axis=0)
    qs_ref[0] = lax.dot_general(wqs_ref[...], h, _NT,
                                preferred_element_type=_F32).astype(_BF16)
    hp = jnp.dot(perm_ref[...], h, preferred_element_type=_F32).astype(_BF16)
    ks_ref[0, 0] = jnp.dot(hp, wks_ref[...], preferred_element_type=_F32).astype(_BF16)
    vs_ref[0, 0] = lax.dot_general(wvs_ref[...], hp, _NT,
                                   preferred_element_type=_F32).astype(_BF16)
    rest_ref[0] = jnp.dot(h, wrest_ref[...], preferred_element_type=_F32).astype(_BF16)


def _in_proj(x, g, perm, kpos, wqa, wka, wva, wqs, wks, wvs, wrest):
    b, s, d = x.shape
    nb = s // KV_BLOCK
    t = KV_BLOCK
    kw = N_HEADS * PAIR
    out_shape = (
        jax.ShapeDtypeStruct((b, WIDTH, s), _BF16),
        jax.ShapeDtypeStruct((b, nb, t, kw), _BF16),
        jax.ShapeDtypeStruct((b, nb, N_HEADS * V_ROWS, t), _BF16),
        jax.ShapeDtypeStruct((b, nb, kw), _F32),
        jax.ShapeDtypeStruct((b, nb, kw), _F32),
        jax.ShapeDtypeStruct((b, WIDTH, s), _BF16),
        jax.ShapeDtypeStruct((b, nb, t, WIDTH), _BF16),
        jax.ShapeDtypeStruct((b, nb, WIDTH, t), _BF16),
        jax.ShapeDtypeStruct((b, s, REST_WIDTH), _BF16),
    )
    qt_spec = pl.BlockSpec((1, WIDTH, t), lambda bi, i: (bi, 0, i))
    k_spec = pl.BlockSpec((1, 1, t, WIDTH), lambda bi, i: (bi, i, 0, 0))
    vt_spec = pl.BlockSpec((1, 1, WIDTH, t), lambda bi, i: (bi, i, 0, 0))
    return pl.pallas_call(
        _in_proj_kernel,
        grid=(b, nb),
        in_specs=[
            pl.BlockSpec((1, t, d), lambda bi, i: (bi, i, 0)),
            _const_spec((1, d)),
            _const_spec((t, t)),
            _const_spec((t, kw)),
            _const_spec((WIDTH, d)), _const_spec((d, WIDTH)), _const_spec((WIDTH, d)),
            _const_spec((WIDTH, d)), _const_spec((d, WIDTH)), _const_spec((WIDTH, d)),
            _const_spec((d, REST_WIDTH)),
        ],
        out_specs=(
            qt_spec,
            pl.BlockSpec((1, 1, t, kw), lambda bi, i: (bi, i, 0, 0)),
            pl.BlockSpec((1, 1, N_HEADS * V_ROWS, t), lambda bi, i: (bi, i, 0, 0)),
            pl.BlockSpec((1, nb, kw), lambda bi, i: (bi, 0, 0)),
            pl.BlockSpec((1, nb, kw), lambda bi, i: (bi, 0, 0)),
            qt_spec, k_spec, vt_spec,
            pl.BlockSpec((1, t, REST_WIDTH), lambda bi, i: (bi, i, 0)),
        ),
        out_shape=out_shape,
        compiler_params=_params("arbitrary", "arbitrary"),
        name="in_proj",
    )(x, g, perm, kpos, wqa, wka, wva, wqs, wks, wvs, wrest)


def _head_rows(q_pair, hh):
    row = lax.broadcasted_iota(jnp.int32, q_pair.shape, 0)
    keep = (row >= hh * HEAD_DIM) & (row < (hh + 1) * HEAD_DIM)
    return jnp.where(keep, q_pair, jnp.zeros_like(q_pair))


def _moba_kernel(q_ref, qaug_ref, k_ref, v_ref, kmean_ref, kamax_ref, slope_ref, causal_ref,
                 o_ref, sel_ref, sa_ref, sb_ref, acc_ref):
    blk = pl.program_id(2)
    nb = k_ref.shape[1]
    nidx = lax.broadcasted_iota(jnp.int32, (nb, MOBA_Q), 0)
    neg_inf = jnp.float32(-jnp.inf)
    slope = [slope_ref[hh, 0:1, :] for hh in range(2)]
    lanes = [slice(hh * PAIR, (hh + 1) * PAIR) for hh in range(2)]
    vrows = [slice(hh * V_ROWS, (hh + 1) * V_ROWS) for hh in range(2)]

    heads = []
    first_needed = jnp.int32(nb)
    for hh in range(2):
        qm = jnp.concatenate([q_ref[0, hh * HEAD_DIM:(hh + 1) * HEAD_DIM, :], qaug_ref[hh]], axis=0)
        km = kmean_ref[0, :, lanes[hh]].astype(_BF16)
        kamax = kamax_ref[0, :, lanes[hh]].astype(_BF16)
        bound = jnp.dot(kamax, jnp.abs(qm), preferred_element_type=_F32)
        own = jnp.max(jnp.where(nidx == blk, bound, 0.0), axis=0, keepdims=True)
        far = slope[hh] * ((nidx + 1 - blk) * KV_BLOCK).astype(_F32)
        needed = (nidx < blk) & (bound + own + far > MOBA_SKIP)
        first_needed = jnp.minimum(first_needed, jnp.min(jnp.where(needed, nidx, nb)))
        gate = jnp.dot(km, qm, preferred_element_type=_F32)
        g = jnp.where(nidx < blk, gate, neg_inf)
        sel = nidx == blk
        for r in range(MOBA_TOPK):
            mx = jnp.max(g, axis=0, keepdims=True)
            first = jnp.min(jnp.where(g == mx, nidx, nb), axis=0, keepdims=True)
            hit = nidx == first
            sel = sel | (hit & (jnp.full((nb, MOBA_Q), r, jnp.int32) < blk))
            g = jnp.where(hit, neg_inf, g)
        sel_ref[hh] = jnp.where(sel, 0.0, neg_inf)
        heads.append(qm)

    def shift(j):
        return jnp.full((1, MOBA_Q), (j - blk) * KV_BLOCK, jnp.int32).astype(_F32)

    def row_bias(hh, j):
        return sel_ref[hh, pl.ds(j, 1), :] + slope[hh] * shift(j)

    def scores(t, dst_ref, nearest=False):
        mx = []
        for hh in range(2):
            col = jnp.full((1, MOBA_Q), neg_inf, _F32)
            for g in range(MOBA_G):
                j = t * MOBA_G + g
                s = jnp.dot(k_ref[0, j, :, lanes[hh]], heads[hh], preferred_element_type=_F32)
                if nearest:
                    s = s + causal_ref[(j == blk).astype(jnp.int32)]
                dst_ref[hh, g * KV_BLOCK:(g + 1) * KV_BLOCK, :] = s
                col = jnp.maximum(col, jnp.max(s, axis=0, keepdims=True) + row_bias(hh, j))
            mx.append(col)
        return mx

    def accumulate(t, src_ref, mx, st):
        out = []
        for hh in range(2):
            m_new = jnp.maximum(st[hh], mx[hh])
            acc = jnp.exp2(st[hh] - m_new) * acc_ref[hh]
            for g in range(MOBA_G):
                j = t * MOBA_G + g
                s = src_ref[hh, g * KV_BLOCK:(g + 1) * KV_BLOCK, :]
                p = jnp.exp2(s - (m_new - row_bias(hh, j)))
                acc = acc + jnp.dot(v_ref[0, j, vrows[hh], :], p.astype(_BF16),
                                    preferred_element_type=_F32)
            acc_ref[hh] = acc
            out.append(m_new)
        return out

    last = blk // MOBA_G
    first = jnp.minimum(first_needed // MOBA_G, last)
    ntiles = last - first + 1
    state = [jnp.full((1, MOBA_Q), jnp.finfo(_F32).min, _F32) for _ in range(2)]
    for hh in range(2):
        acc_ref[hh] = jnp.zeros((V_ROWS, MOBA_Q), _F32)

    mx_a = scores(last, sa_ref, nearest=True)
    mx_b = scores(jnp.maximum(last - 1, 0), sb_ref)
    state = accumulate(last, sa_ref, mx_a, state)

    def body(k, carry):
        st, mx_b = carry[:2], carry[2:]
        t = last - 1 - 2 * k
        mx_a = scores(t - 1, sa_ref)
        st = accumulate(t, sb_ref, mx_b, st)
        mx_b = scores(jnp.maximum(t - 2, 0), sb_ref)
        st = accumulate(t - 1, sa_ref, mx_a, st)
        return tuple(st) + tuple(mx_b)

    carry = lax.fori_loop(0, (ntiles - 1) // 2, body, tuple(state) + tuple(mx_b))

    @pl.when(ntiles % 2 == 0)
    def _():
        accumulate(first, sb_ref, carry[2:], carry[:2])

    o_ref[0] = jnp.concatenate(
        [acc_ref[hh, 0:HEAD_DIM, :] / acc_ref[hh, HEAD_DIM:HEAD_DIM + 1, :] for hh in range(2)],
        axis=0).astype(_BF16)


def _moba(qa_t, qaug, ka, va_t, kmean, kamax, slopes, causal):
    b, _, s = qa_t.shape
    nb = s // KV_BLOCK
    npair = N_HEADS // 2
    assert nb % MOBA_G == 0
    return pl.pallas_call(
        _moba_kernel,
        grid=(b, npair, s // MOBA_Q),
        in_specs=[
            pl.BlockSpec((1, PAIR, MOBA_Q), lambda bi, hp, c: (bi, hp, c)),
            pl.BlockSpec((2, HEAD_DIM, MOBA_Q), lambda bi, hp, c: (hp, 0, 0)),
            pl.BlockSpec((1, nb, KV_BLOCK, 2 * PAIR), lambda bi, hp, c: (bi, 0, 0, hp)),
            pl.BlockSpec((1, nb, 2 * V_ROWS, KV_BLOCK), lambda bi, hp, c: (bi, 0, hp, 0)),
            pl.BlockSpec((1, nb, 2 * PAIR), lambda bi, hp, c: (bi, 0, hp)),
            pl.BlockSpec((1, nb, 2 * PAIR), lambda bi, hp, c: (bi, 0, hp)),
            pl.BlockSpec((2, SUBLANES, MOBA_Q), lambda bi, hp, c: (hp, 0, 0)),
            _const_spec((2, KV_BLOCK, MOBA_Q)),
        ],
        out_specs=pl.BlockSpec((1, PAIR, MOBA_Q), lambda bi, hp, c: (bi, hp, c)),
        out_shape=jax.ShapeDtypeStruct((b, WIDTH, s), _BF16),
        scratch_shapes=[pltpu.VMEM((2, nb, MOBA_Q), _F32),
                        pltpu.VMEM((2, MOBA_G * KV_BLOCK, MOBA_Q), _F32),
                        pltpu.VMEM((2, MOBA_G * KV_BLOCK, MOBA_Q), _F32),
                        pltpu.VMEM((2, V_ROWS, MOBA_Q), _F32)],
        compiler_params=_params("arbitrary", "arbitrary", "arbitrary"),
        name="moba",
    )(qa_t, qaug, ka, va_t, kmean, kamax, slopes, causal)


def _suffix_product_over_sublanes(x):
    sub = lax.broadcasted_iota(jnp.int32, x.shape, 0)
    y = x
    for d in (1, 2, 4):
        up = pltpu.roll(y, SUBLANES - d, 0)
        y = y * jnp.where(sub < SUBLANES - d, up, 1.0)
    return y


def _sb_scores(k_j, qm, mask):
    z = jnp.dot(k_j, qm, preferred_element_type=_F32)
    keep = 1.0 / (1.0 + jnp.exp2(z))
    if mask is not None:
        keep = jnp.where(mask, keep, 1.0)
    run = jnp.ones((SUBLANES, z.shape[1]), _F32)
    diff = [None] * RUN
    for r in reversed(range(RUN)):
        nxt = run * keep[r * SUBLANES:(r + 1) * SUBLANES, :]
        diff[r] = run - nxt
        run = nxt
    sub = lax.broadcasted_iota(jnp.int32, run.shape, 0)
    shifted = jnp.where(sub < SUBLANES - 1, pltpu.roll(run, SUBLANES - 1, 0), 1.0)
    return diff, run, _suffix_product_over_sublanes(shifted)


def _sb_weights(scores, v_jh, carry, acc):
    diff, run, later = scores
    base = later * carry
    w = jnp.concatenate([d * base for d in diff], axis=0)
    acc = acc + jnp.dot(v_jh, w.astype(_BF16), preferred_element_type=_F32)
    return carry * (later[0:1, :] * run[0:1, :]), acc


def _sb_kernel(q_ref, k_ref, v_ref, o_ref):
    jd = pl.program_id(2)
    q_pair = q_ref[0]
    rho = lax.broadcasted_iota(jnp.int32, (KV_BLOCK, SB_Q), 0)
    kpos = (rho & (SUBLANES - 1)) * RUN + (rho >> 3)
    qpos = lax.broadcasted_iota(jnp.int32, (KV_BLOCK, SB_Q), 1)
    strict = kpos < qpos
    heads = [_head_rows(q_pair, hh) for hh in range(2)]
    rows = [slice(hh * HEAD_DIM, (hh + 1) * HEAD_DIM) for hh in range(2)]

    jp = jnp.maximum(jd - 1, 0)
    state = []
    diags = [_sb_scores(k_ref[0, jd], heads[hh], strict) for hh in range(2)]
    prevs = [_sb_scores(k_ref[0, jp], heads[hh], None) for hh in range(2)]
    for hh in range(2):
        carry, acc = _sb_weights(diags[hh], v_ref[0, jd, rows[hh], :],
                                 jnp.ones((1, SB_Q), _F32),
                                 jnp.zeros((HEAD_DIM, SB_Q), _F32))
        carry = jnp.where(jd > 0, carry, 0.0)
        state += list(_sb_weights(prevs[hh], v_ref[0, jp, rows[hh], :], carry, acc))

    def cond(st):
        j, c0, _, c1, _ = st
        return jnp.logical_and(j >= 0, jnp.maximum(jnp.max(c0), jnp.max(c1)) > SB_STOP)

    def body(st):
        j = st[0]
        out = []
        for hh in range(2):
            carry, acc = st[1 + 2 * hh:3 + 2 * hh]
            out += list(_sb_weights(_sb_scores(k_ref[0, j], heads[hh], None),
                                    v_ref[0, j, rows[hh], :], carry, acc))
        return (j - 1, *out)

    st = lax.while_loop(cond, body, (jd - 2, *state))
    o_ref[0] = jnp.concatenate([st[2], st[4]], axis=0).astype(_BF16)


def _stick_breaking(qs_t, ks, vs_t):
    b, _, s = qs_t.shape
    nb = s // KV_BLOCK
    npair = N_HEADS // 2
    return pl.pallas_call(
        _sb_kernel,
        grid=(b, npair, s // SB_Q),
        in_specs=[
            pl.BlockSpec((1, PAIR, SB_Q), lambda bi, hp, i: (bi, hp, i)),
            pl.BlockSpec((1, nb, KV_BLOCK, PAIR), lambda bi, hp, i: (bi, 0, 0, hp)),
            pl.BlockSpec((1, nb, PAIR, KV_BLOCK), lambda bi, hp, i: (bi, 0, hp, 0)),
        ],
        out_specs=pl.BlockSpec((1, PAIR, SB_Q), lambda bi, hp, i: (bi, hp, i)),
        out_shape=jax.ShapeDtypeStruct((b, WIDTH, s), _BF16),
        compiler_params=_params("arbitrary", "arbitrary", "arbitrary"),
        name="stick_breaking",
    )(qs_t, ks, vs_t)


MIX_T = 512
HALO = 16


def _mix_kernel(oa_ref, ob_ref, rest_ref, halo_ref, x_ref, convw_ref, bg_ref,
                wpa_ref, wpb_ref, wpc_ref, wout_ref, o_ref):
    i = pl.program_id(1)
    r = rest_ref[0]
    xc = r[:, 0:WIDTH].astype(_F32)
    bc = r[:, WIDTH:2 * WIDTH].astype(_F32)
    cc = r[:, 2 * WIDTH:3 * WIDTH].astype(_F32)
    u = cc * xc
    hl = halo_ref[0]
    uh = hl[:, 2 * WIDTH:3 * WIDTH].astype(_F32) * hl[:, 0:WIDTH].astype(_F32)
    uh = jnp.where(i > 0, uh, 0.0)
    prev1 = uh[HALO - 1:HALO, :]
    prev2 = uh[HALO - 2:HALO - 1, :]
    row = lax.broadcasted_iota(jnp.int32, u.shape, 0)
    u1 = jnp.where(row == 0, prev1, pltpu.roll(u, 1, 0))
    u2 = jnp.where(row == 0, prev2, jnp.where(row == 1, prev1, pltpu.roll(u, 2, 0)))
    w = convw_ref[...]
    y = bc * (u2 * w[0:1, :] + u1 * w[1:2, :] + u * w[2:3, :])
    g = jax.nn.sigmoid(r[:, 3 * WIDTH:].astype(_F32) + bg_ref[...])
    d = D_MODEL
    merged = (g[:, 0:d] * lax.dot_general(oa_ref[0], wpa_ref[...], _TN,
                                          preferred_element_type=_F32)
              + g[:, d:2 * d] * lax.dot_general(ob_ref[0], wpb_ref[...], _TN,
                                                preferred_element_type=_F32)
              + g[:, 2 * d:] * jnp.dot(y.astype(_BF16), wpc_ref[...],
                                       preferred_element_type=_F32))
    o_ref[0] = x_ref[0] + jnp.dot(merged.astype(_BF16), wout_ref[...],
                                  preferred_element_type=_F32)


def _mix(oa, ob, rest, x, convw, bg, wpa, wpb, wpc, wout):
    b, s, d = x.shape
    t = MIX_T
    tile = lambda width: pl.BlockSpec((1, t, width), lambda bi, i: (bi, i, 0))
    tile_t = pl.BlockSpec((1, WIDTH, t), lambda bi, i: (bi, 0, i))
    halo = pl.BlockSpec((1, HALO, REST_WIDTH),
                        lambda bi, i: (bi, jnp.maximum(i * (t // HALO) - 1, 0), 0))
    return pl.pallas_call(
        _mix_kernel,
        grid=(b, s // t),
        in_specs=[tile_t, tile_t, tile(REST_WIDTH), halo, tile(d),
                  _const_spec((CONV_K, WIDTH)), _const_spec((1, 3 * d)),
                  _const_spec((WIDTH, d)), _const_spec((WIDTH, d)), _const_spec((WIDTH, d)),
                  _const_spec((d, d))],
        out_specs=tile(d),
        out_shape=jax.ShapeDtypeStruct((b, s, d), _F32),
        compiler_params=_params("arbitrary", "arbitrary"),
        name="mix",
    )(oa, ob, rest, rest, x, convw, bg, wpa, wpb, wpc, wout)


FFN_T = 512
FF_CHUNK = 256


def _ffn_up_kernel(x_ref, g_ref, wg_ref, wu_ref, o_ref):
    h = _rms(x_ref[...], g_ref[...]).astype(_BF16)
    for n in range(D_FF // FF_CHUNK):
        cols = slice(n * FF_CHUNK, (n + 1) * FF_CHUNK)
        a = jnp.dot(h, wg_ref[:, cols], preferred_element_type=_F32)
        bgate = jnp.dot(h, wu_ref[:, cols], preferred_element_type=_F32)
        o_ref[:, cols] = (a * jax.nn.sigmoid(a) * bgate).astype(_BF16)


def _ffn_down_kernel(a_ref, x_ref, wd_ref, g_ref, o_ref, *, final_norm):
    y = x_ref[...] + jnp.dot(a_ref[...], wd_ref[...], preferred_element_type=_F32)
    if final_norm:
        y = _rms(y, g_ref[...])
    o_ref[...] = y


def _ffn(x2d, g, wg, wu, wd, gfinal, final_norm):
    n, d = x2d.shape
    t = FFN_T
    act = pl.pallas_call(
        _ffn_up_kernel,
        grid=(n // t,),
        in_specs=[pl.BlockSpec((t, d), lambda i: (i, 0)), _const_spec((1, d)),
                  _const_spec((d, D_FF)), _const_spec((d, D_FF))],
        out_specs=pl.BlockSpec((t, D_FF), lambda i: (i, 0)),
        out_shape=jax.ShapeDtypeStruct((n, D_FF), _BF16),
        compiler_params=_params("arbitrary"),
        name="ffn_up",
    )(x2d, g, wg, wu)
    return pl.pallas_call(
        functools.partial(_ffn_down_kernel, final_norm=final_norm),
        grid=(n // t,),
        in_specs=[pl.BlockSpec((t, D_FF), lambda i: (i, 0)),
                  pl.BlockSpec((t, d), lambda i: (i, 0)),
                  _const_spec((D_FF, d)), _const_spec((1, d))],
        out_specs=pl.BlockSpec((t, d), lambda i: (i, 0)),
        out_shape=jax.ShapeDtypeStruct((n, d), _F32),
        compiler_params=_params("arbitrary"),
        name="ffn_down",
    )(act, x2d, wd, gfinal)


def _sb_permutation():
    rho = np.arange(KV_BLOCK)
    kappa = (rho % SUBLANES) * RUN + rho // SUBLANES
    p = np.zeros((KV_BLOCK, KV_BLOCK), np.float32)
    p[rho, kappa] = 1.0
    return jnp.asarray(p, _BF16)


def _moba_tables():
    slopes = np.exp2(-8.0 * (np.arange(N_HEADS, dtype=np.float64) + 1.0) / N_HEADS) * LOG2E
    kpos = np.zeros((KV_BLOCK, N_HEADS * PAIR), np.float32)
    qaug = np.zeros((N_HEADS, HEAD_DIM, MOBA_Q), np.float32)
    for hd in range(N_HEADS):
        rest = np.float64(slopes[hd])
        for a in range(N_AUG):
            piece = np.float64(np.float32(rest).astype(_BF16))
            kpos[:, hd * PAIR + HEAD_DIM + a] = np.arange(KV_BLOCK)
            qaug[hd, a, :] = piece
            rest = rest - piece
    future = np.arange(KV_BLOCK)[:, None] > np.arange(MOBA_Q)[None, :]
    causal = np.stack([np.zeros((KV_BLOCK, MOBA_Q)), np.where(future, -np.inf, 0.0)])
    slope_tab = slopes[:, None, None] * np.ones((1, SUBLANES, MOBA_Q))
    return (jnp.asarray(kpos), jnp.asarray(qaug, _BF16),
            jnp.asarray(slope_tab.astype(np.float32)), jnp.asarray(causal.astype(np.float32)))


def kernel(x, norm_mix_g, w_in, b_gate, conv_w, w_proj_moba, w_proj_sb, w_proj_conv, w_out,
           norm_ffn_g, w_ffn_gate, w_ffn_up, w_ffn_down, norm_final_g):
    depth = w_in.shape[0]
    b, s, d = x.shape
    scale = HEAD_DIM ** -0.5
    perm = _sb_permutation()
    kpos, qaug, slopes, causal = _moba_tables()
    w = WIDTH
    for l in range(depth):
        wl = w_in[l]
        cols = lambda a, n=w: wl[:, a:a + n].astype(_BF16)
        wqa = (wl[:, 0:w] * (scale * LOG2E)).T.astype(_BF16)
        wka = cols(w)
        wva = wl[:, 2 * w:3 * w].T.astype(_BF16)
        wqs = (wl[:, 3 * w:4 * w] * (scale * LOG2E)).T.astype(_BF16)
        wks = cols(4 * w)
        wvs = wl[:, 5 * w:6 * w].T.astype(_BF16)
        wrest = cols(6 * w, REST_WIDTH)
        qa_t, ka, va_t, kmean, kamax, qs_t, ks, vs_t, rest = _in_proj(
            x, norm_mix_g[l][None, :], perm, kpos, wqa, wka, wva, wqs, wks, wvs, wrest)
        oa = _moba(qa_t, qaug, ka, va_t, kmean, kamax, slopes, causal)
        ob = _stick_breaking(qs_t, ks, vs_t)
        x = _mix(oa, ob, rest, x, conv_w[l], b_gate[l][None, :],
                 w_proj_moba[l].astype(_BF16), w_proj_sb[l].astype(_BF16),
                 w_proj_conv[l].astype(_BF16), w_out[l].astype(_BF16))
        x = _ffn(x.reshape(b * s, d), norm_ffn_g[l][None, :], w_ffn_gate[l].astype(_BF16),
                 w_ffn_up[l].astype(_BF16), w_ffn_down[l].astype(_BF16),
                 norm_final_g[None, :], final_norm=(l == depth - 1)).reshape(b, s, d)
    return x
```

```python
import functools

import jax
import jax.numpy as jnp
import numpy as np
from jax import lax
from jax.experimental import pallas as pl
from jax.experimental.pallas import tpu as pltpu

D_MODEL = 1024
HEAD_DIM = 64
N_HEADS = 8
WIDTH = N_HEADS * HEAD_DIM
CONV_K = 3
KV_BLOCK = 256
MOBA_TOPK = 3
MOBA_Q = 256
MOBA_G = 2
SB_Q = KV_BLOCK
MOBA_SKIP = -150.0
D_FF = 2816
RMS_EPS = 1e-6
REST_WIDTH = 3 * WIDTH + 3 * D_MODEL
PAIR = 2 * HEAD_DIM
V_ROWS = HEAD_DIM + 16
N_AUG = 4
SUBLANES = 8
LOG2E = 1.4426950408889634
RUN = KV_BLOCK // SUBLANES
SB_STOP = 1e-30
VMEM_LIMIT = 56 * 1024 * 1024

_TN = (((0,), (0,)), ((), ()))
_F32 = jnp.float32
_BF16 = jnp.bfloat16


def _params(*sem):
    return pltpu.CompilerParams(dimension_semantics=sem, vmem_limit_bytes=VMEM_LIMIT)


def _const_spec(shape):
    zeros = (0,) * len(shape)
    return pl.BlockSpec(shape, lambda *_: zeros)


def _layer_spec(layer, shape):
    index = (layer,) + (0,) * len(shape)
    return pl.BlockSpec((1,) + tuple(shape), lambda *_: index)


def _rms(x, g):
    y = x * lax.rsqrt(jnp.mean(x * x, axis=-1, keepdims=True) + RMS_EPS)
    return y * g


def _in_proj_kernel(x_ref, g_ref, perm_ref, kpos_ref, w_ref,
                    qa_ref, ka_ref, va_ref, kmean_ref, kamax_ref, qs_ref, ks_ref, vs_ref,
                    rest_ref):
    i = pl.program_id(1)
    h = _rms(x_ref[0], g_ref[0]).astype(_BF16)

    def proj(lhs, k):
        return jnp.dot(lhs, w_ref[0, :, k * WIDTH:(k + 1) * WIDTH], preferred_element_type=_F32)

    qa_ref[0] = proj(h, 0).T.astype(_BF16)
    ka = proj(h, 1)
    lane = lax.broadcasted_iota(jnp.int32, (KV_BLOCK, PAIR), 1)
    tiles = []
    for pair in range(N_HEADS // 2):
        both = ka[:, pair * PAIR:(pair + 1) * PAIR]
        tiles += [jnp.where(lane < HEAD_DIM, both, 0.0),
                  jnp.where(lane < HEAD_DIM, pltpu.roll(both, HEAD_DIM, 1), 0.0)]
    ka_wide = jnp.concatenate(tiles, axis=1)
    ka_ref[0, 0] = (ka_wide + kpos_ref[...]).astype(_BF16)
    kmean_ref[0, pl.ds(i, 1), :] = jnp.mean(ka_wide, axis=0, keepdims=True)
    kamax_ref[0, pl.ds(i, 1), :] = jnp.max(jnp.abs(ka_wide.astype(_BF16).astype(_F32)),
                                           axis=0, keepdims=True)
    va_t = proj(h, 2).T.astype(_BF16)
    extra = (lax.broadcasted_iota(jnp.int32, (V_ROWS - HEAD_DIM, KV_BLOCK), 0) == 0)
    extra = extra.astype(_F32).astype(_BF16)
    va_ref[0, 0] = jnp.concatenate(
        [piece for hd in range(N_HEADS)
         for piece in (va_t[hd * HEAD_DIM:(hd + 1) * HEAD_DIM, :], extra)], axis=0)
    qs_ref[0] = proj(h, 3).T.astype(_BF16)
    hp = jnp.dot(perm_ref[...], h, preferred_element_type=_F32).astype(_BF16)
    ks_ref[0, 0] = proj(hp, 4).astype(_BF16)
    vs_ref[0, 0] = proj(hp, 5).T.astype(_BF16)
    rest_ref[0] = jnp.dot(h, w_ref[0, :, 6 * WIDTH:], preferred_element_type=_F32).astype(_BF16)


def _in_proj(layer, x, g, perm, kpos, w_all):
    b, s, d = x.shape
    nb = s // KV_BLOCK
    t = KV_BLOCK
    kw = N_HEADS * PAIR
    out_shape = (
        jax.ShapeDtypeStruct((b, WIDTH, s), _BF16),
        jax.ShapeDtypeStruct((b, nb, t, kw), _BF16),
        jax.ShapeDtypeStruct((b, nb, N_HEADS * V_ROWS, t), _BF16),
        jax.ShapeDtypeStruct((b, nb, kw), _F32),
        jax.ShapeDtypeStruct((b, nb, kw), _F32),
        jax.ShapeDtypeStruct((b, WIDTH, s), _BF16),
        jax.ShapeDtypeStruct((b, nb, t, WIDTH), _BF16),
        jax.ShapeDtypeStruct((b, nb, WIDTH, t), _BF16),
        jax.ShapeDtypeStruct((b, s, REST_WIDTH), _BF16),
    )
    qt_spec = pl.BlockSpec((1, WIDTH, t), lambda bi, i: (bi, 0, i))
    k_spec = pl.BlockSpec((1, 1, t, WIDTH), lambda bi, i: (bi, i, 0, 0))
    vt_spec = pl.BlockSpec((1, 1, WIDTH, t), lambda bi, i: (bi, i, 0, 0))
    return pl.pallas_call(
        _in_proj_kernel,
        grid=(b, nb),
        in_specs=[
            pl.BlockSpec((1, t, d), lambda bi, i: (bi, i, 0)),
            _layer_spec(layer, (1, d)),
            _const_spec((t, t)),
            _const_spec((t, kw)),
            _layer_spec(layer, (d, 6 * WIDTH + REST_WIDTH)),
        ],
        out_specs=(
            qt_spec,
            pl.BlockSpec((1, 1, t, kw), lambda bi, i: (bi, i, 0, 0)),
            pl.BlockSpec((1, 1, N_HEADS * V_ROWS, t), lambda bi, i: (bi, i, 0, 0)),
            pl.BlockSpec((1, nb, kw), lambda bi, i: (bi, 0, 0)),
            pl.BlockSpec((1, nb, kw), lambda bi, i: (bi, 0, 0)),
            qt_spec, k_spec, vt_spec,
            pl.BlockSpec((1, t, REST_WIDTH), lambda bi, i: (bi, i, 0)),
        ),
        out_shape=out_shape,
        compiler_params=_params("arbitrary", "arbitrary"),
        name="in_proj",
    )(x, g, perm, kpos, w_all)


def _head_rows(q_pair, hh):
    row = lax.broadcasted_iota(jnp.int32, q_pair.shape, 0)
    keep = (row >= hh * HEAD_DIM) & (row < (hh + 1) * HEAD_DIM)
    return jnp.where(keep, q_pair, jnp.zeros_like(q_pair))


def _moba_kernel(q_ref, qaug_ref, k_ref, v_ref, kmean_ref, kamax_ref, slope_ref, causal_ref,
                 o_ref, sel_ref, sa_ref, sb_ref, acc_ref):
    blk = pl.program_id(2)
    nb = k_ref.shape[1]
    nidx = lax.broadcasted_iota(jnp.int32, (nb, MOBA_Q), 0)
    neg_inf = jnp.float32(-jnp.inf)
    slope = [slope_ref[hh, 0:1, :] for hh in range(2)]
    lanes = [slice(hh * PAIR, (hh + 1) * PAIR) for hh in range(2)]
    vrows = [slice(hh * V_ROWS, (hh + 1) * V_ROWS) for hh in range(2)]

    heads = []
    first_needed = jnp.int32(nb)
    for hh in range(2):
        qm = jnp.concatenate([q_ref[0, hh * HEAD_DIM:(hh + 1) * HEAD_DIM, :], qaug_ref[hh]], axis=0)
        km = kmean_ref[0, :, lanes[hh]].astype(_BF16)
        kamax = kamax_ref[0, :, lanes[hh]].astype(_BF16)
        bound = jnp.dot(kamax, jnp.abs(qm), preferred_element_type=_F32)
        own = jnp.max(jnp.where(nidx == blk, bound, 0.0), axis=0, keepdims=True)
        far = slope[hh] * ((nidx + 1 - blk) * KV_BLOCK).astype(_F32)
        needed = (nidx < blk) & (bound + own + far > MOBA_SKIP)
        first_needed = jnp.minimum(first_needed, jnp.min(jnp.where(needed, nidx, nb)))
        gate = jnp.dot(km, qm, preferred_element_type=_F32)
        g = jnp.where(nidx < blk, gate, neg_inf)
        sel = nidx == blk
        for r in range(MOBA_TOPK):
            mx = jnp.max(g, axis=0, keepdims=True)
            first = jnp.min(jnp.where(g == mx, nidx, nb), axis=0, keepdims=True)
            hit = nidx == first
            sel = sel | (hit & (jnp.full((nb, MOBA_Q), r, jnp.int32) < blk))
            g = jnp.where(hit, neg_inf, g)
        sel_ref[hh] = jnp.where(sel, 0.0, neg_inf)
        heads.append(qm)

    def shift(j):
        return jnp.full((1, MOBA_Q), (j - blk) * KV_BLOCK, jnp.int32).astype(_F32)

    def row_bias(hh, j):
        return sel_ref[hh, pl.ds(j, 1), :] + slope[hh] * shift(j)

    def scores(t, dst_ref, nearest=False):
        mx = []
        for hh in range(2):
            col = jnp.full((1, MOBA_Q), neg_inf, _F32)
            for g in range(MOBA_G):
                j = t * MOBA_G + g
                s = jnp.dot(k_ref[0, j, :, lanes[hh]], heads[hh], preferred_element_type=_F32)
                if nearest:
                    s = s + causal_ref[(j == blk).astype(jnp.int32)]
                dst_ref[hh, g * KV_BLOCK:(g + 1) * KV_BLOCK, :] = s
                col = jnp.maximum(col, jnp.max(s, axis=0, keepdims=True) + row_bias(hh, j))
            mx.append(col)
        return mx

    def accumulate(t, src_ref, mx, st):
        out = []
        for hh in range(2):
            m_new = jnp.maximum(st[hh], mx[hh])
            acc = jnp.exp2(st[hh] - m_new) * acc_ref[hh]
            for g in range(MOBA_G):
                j = t * MOBA_G + g
                s = src_ref[hh, g * KV_BLOCK:(g + 1) * KV_BLOCK, :]
                p = jnp.exp2(s - (m_new - row_bias(hh, j)))
                acc = acc + jnp.dot(v_ref[0, j, vrows[hh], :], p.astype(_BF16),
                                    preferred_element_type=_F32)
            acc_ref[hh] = acc
            out.append(m_new)
        return out

    last = blk // MOBA_G
    first = jnp.minimum(first_needed // MOBA_G, last)
    ntiles = last - first + 1
    state = [jnp.full((1, MOBA_Q), jnp.finfo(_F32).min, _F32) for _ in range(2)]
    for hh in range(2):
        acc_ref[hh] = jnp.zeros((V_ROWS, MOBA_Q), _F32)

    mx_a = scores(last, sa_ref, nearest=True)
    mx_b = scores(jnp.maximum(last - 1, 0), sb_ref)
    state = accumulate(last, sa_ref, mx_a, state)

    def body(k, carry):
        st, mx_b = carry[:2], carry[2:]
        t = last - 1 - 2 * k
        mx_a = scores(t - 1, sa_ref)
        st = accumulate(t, sb_ref, mx_b, st)
        mx_b = scores(jnp.maximum(t - 2, 0), sb_ref)
        st = accumulate(t - 1, sa_ref, mx_a, st)
        return tuple(st) + tuple(mx_b)

    carry = lax.fori_loop(0, (ntiles - 1) // 2, body, tuple(state) + tuple(mx_b))

    @pl.when(ntiles % 2 == 0)
    def _():
        accumulate(first, sb_ref, carry[2:], carry[:2])

    o_ref[0] = jnp.concatenate(
        [acc_ref[hh, 0:HEAD_DIM, :] / acc_ref[hh, HEAD_DIM:HEAD_DIM + 1, :] for hh in range(2)],
        axis=0).astype(_BF16)


def _moba(qa_t, qaug, ka, va_t, kmean, kamax, slopes, causal):
    b, _, s = qa_t.shape
    nb = s // KV_BLOCK
    npair = N_HEADS // 2
    assert nb % MOBA_G == 0
    return pl.pallas_call(
        _moba_kernel,
        grid=(b, npair, s // MOBA_Q),
        in_specs=[
            pl.BlockSpec((1, PAIR, MOBA_Q), lambda bi, hp, c: (bi, hp, c)),
            pl.BlockSpec((2, HEAD_DIM, MOBA_Q), lambda bi, hp, c: (hp, 0, 0)),
            pl.BlockSpec((1, nb, KV_BLOCK, 2 * PAIR), lambda bi, hp, c: (bi, 0, 0, hp)),
            pl.BlockSpec((1, nb, 2 * V_ROWS, KV_BLOCK), lambda bi, hp, c: (bi, 0, hp, 0)),
            pl.BlockSpec((1, nb, 2 * PAIR), lambda bi, hp, c: (bi, 0, hp)),
            pl.BlockSpec((1, nb, 2 * PAIR), lambda bi, hp, c: (bi, 0, hp)),
            pl.BlockSpec((2, SUBLANES, MOBA_Q), lambda bi, hp, c: (hp, 0, 0)),
            _const_spec((2, KV_BLOCK, MOBA_Q)),
        ],
        out_specs=pl.BlockSpec((1, PAIR, MOBA_Q), lambda bi, hp, c: (bi, hp, c)),
        out_shape=jax.ShapeDtypeStruct((b, WIDTH, s), _BF16),
        scratch_shapes=[pltpu.VMEM((2, nb, MOBA_Q), _F32),
                        pltpu.VMEM((2, MOBA_G * KV_BLOCK, MOBA_Q), _F32),
                        pltpu.VMEM((2, MOBA_G * KV_BLOCK, MOBA_Q), _F32),
                        pltpu.VMEM((2, V_ROWS, MOBA_Q), _F32)],
        compiler_params=_params("arbitrary", "arbitrary", "arbitrary"),
        name="moba",
    )(qa_t, qaug, ka, va_t, kmean, kamax, slopes, causal)


def _suffix_product_over_sublanes(x):
    sub = lax.broadcasted_iota(jnp.int32, x.shape, 0)
    y = x
    for d in (1, 2, 4):
        up = pltpu.roll(y, SUBLANES - d, 0)
        y = y * jnp.where(sub < SUBLANES - d, up, 1.0)
    return y


def _sb_scores(k_j, qm, mask):
    z = jnp.dot(k_j, qm, preferred_element_type=_F32)
    keep = 1.0 / (1.0 + jnp.exp2(z))
    if mask is not None:
        keep = jnp.where(mask, keep, 1.0)
    run = jnp.ones((SUBLANES, z.shape[1]), _F32)
    diff = [None] * RUN
    for r in reversed(range(RUN)):
        nxt = run * keep[r * SUBLANES:(r + 1) * SUBLANES, :]
        diff[r] = run - nxt
        run = nxt
    sub = lax.broadcasted_iota(jnp.int32, run.shape, 0)
    shifted = jnp.where(sub < SUBLANES - 1, pltpu.roll(run, SUBLANES - 1, 0), 1.0)
    return diff, run, _suffix_product_over_sublanes(shifted)


def _sb_weights(scores, v_jh, carry, acc):
    diff, run, later = scores
    base = later * carry
    w = jnp.concatenate([d * base for d in diff], axis=0)
    acc = acc + jnp.dot(v_jh, w.astype(_BF16), preferred_element_type=_F32)
    return carry * (later[0:1, :] * run[0:1, :]), acc


def _sb_kernel(q_ref, k_ref, v_ref, o_ref):
    jd = pl.program_id(2)
    q_pair = q_ref[0]
    rho = lax.broadcasted_iota(jnp.int32, (KV_BLOCK, SB_Q), 0)
    kpos = (rho & (SUBLANES - 1)) * RUN + (rho >> 3)
    qpos = lax.broadcasted_iota(jnp.int32, (KV_BLOCK, SB_Q), 1)
    strict = kpos < qpos
    heads = [_head_rows(q_pair, hh) for hh in range(2)]
    rows = [slice(hh * HEAD_DIM, (hh + 1) * HEAD_DIM) for hh in range(2)]

    jp = jnp.maximum(jd - 1, 0)
    state = []
    diags = [_sb_scores(k_ref[0, jd], heads[hh], strict) for hh in range(2)]
    prevs = [_sb_scores(k_ref[0, jp], heads[hh], None) for hh in range(2)]
    for hh in range(2):
        carry, acc = _sb_weights(diags[hh], v_ref[0, jd, rows[hh], :],
                                 jnp.ones((1, SB_Q), _F32),
                                 jnp.zeros((HEAD_DIM, SB_Q), _F32))
        carry = jnp.where(jd > 0, carry, 0.0)
        state += list(_sb_weights(prevs[hh], v_ref[0, jp, rows[hh], :], carry, acc))

    def cond(st):
        j, c0, _, c1, _ = st
        return jnp.logical_and(j >= 0, jnp.maximum(jnp.max(c0), jnp.max(c1)) > SB_STOP)

    def body(st):
        j = st[0]
        out = []
        for hh in range(2):
            carry, acc = st[1 + 2 * hh:3 + 2 * hh]
            out += list(_sb_weights(_sb_scores(k_ref[0, j], heads[hh], None),
                                    v_ref[0, j, rows[hh], :], carry, acc))
        return (j - 1, *out)

    st = lax.while_loop(cond, body, (jd - 2, *state))
    o_ref[0] = jnp.concatenate([st[2], st[4]], axis=0).astype(_BF16)


def _stick_breaking(qs_t, ks, vs_t):
    b, _, s = qs_t.shape
    nb = s // KV_BLOCK
    npair = N_HEADS // 2
    return pl.pallas_call(
        _sb_kernel,
        grid=(b, npair, s // SB_Q),
        in_specs=[
            pl.BlockSpec((1, PAIR, SB_Q), lambda bi, hp, i: (bi, hp, i)),
            pl.BlockSpec((1, nb, KV_BLOCK, PAIR), lambda bi, hp, i: (bi, 0, 0, hp)),
            pl.BlockSpec((1, nb, PAIR, KV_BLOCK), lambda bi, hp, i: (bi, 0, hp, 0)),
        ],
        out_specs=pl.BlockSpec((1, PAIR, SB_Q), lambda bi, hp, i: (bi, hp, i)),
        out_shape=jax.ShapeDtypeStruct((b, WIDTH, s), _BF16),
        compiler_params=_params("arbitrary", "arbitrary", "arbitrary"),
        name="stick_breaking",
    )(qs_t, ks, vs_t)


MIX_T = 512
HALO = 16


def _mix_kernel(oa_ref, ob_ref, rest_ref, halo_ref, x_ref, convw_ref, bg_ref,
                wpa_ref, wpb_ref, wpc_ref, wout_ref, o_ref):
    i = pl.program_id(1)
    r = rest_ref[0]
    xc = r[:, 0:WIDTH].astype(_F32)
    bc = r[:, WIDTH:2 * WIDTH].astype(_F32)
    cc = r[:, 2 * WIDTH:3 * WIDTH].astype(_F32)
    u = cc * xc
    hl = halo_ref[0]
    uh = hl[:, 2 * WIDTH:3 * WIDTH].astype(_F32) * hl[:, 0:WIDTH].astype(_F32)
    uh = jnp.where(i > 0, uh, 0.0)
    prev1 = uh[HALO - 1:HALO, :]
    prev2 = uh[HALO - 2:HALO - 1, :]
    row = lax.broadcasted_iota(jnp.int32, u.shape, 0)
    u1 = jnp.where(row == 0, prev1, pltpu.roll(u, 1, 0))
    u2 = jnp.where(row == 0, prev2, jnp.where(row == 1, prev1, pltpu.roll(u, 2, 0)))
    w = convw_ref[0]
    y = bc * (u2 * w[0:1, :] + u1 * w[1:2, :] + u * w[2:3, :])
    g = jax.nn.sigmoid(r[:, 3 * WIDTH:].astype(_F32) + bg_ref[0])
    d = D_MODEL
    merged = (g[:, 0:d] * lax.dot_general(oa_ref[0], wpa_ref[0], _TN,
                                          preferred_element_type=_F32)
              + g[:, d:2 * d] * lax.dot_general(ob_ref[0], wpb_ref[0], _TN,
                                                preferred_element_type=_F32)
              + g[:, 2 * d:] * jnp.dot(y.astype(_BF16), wpc_ref[0],
                                       preferred_element_type=_F32))
    o_ref[0] = x_ref[0] + jnp.dot(merged.astype(_BF16), wout_ref[0],
                                  preferred_element_type=_F32)


def _mix(layer, oa, ob, rest, x, convw, bg, wpa, wpb, wpc, wout):
    b, s, d = x.shape
    t = MIX_T
    tile = lambda width: pl.BlockSpec((1, t, width), lambda bi, i: (bi, i, 0))
    tile_t = pl.BlockSpec((1, WIDTH, t), lambda bi, i: (bi, 0, i))
    halo = pl.BlockSpec((1, HALO, REST_WIDTH),
                        lambda bi, i: (bi, jnp.maximum(i * (t // HALO) - 1, 0), 0))
    return pl.pallas_call(
        _mix_kernel,
        grid=(b, s // t),
        in_specs=[tile_t, tile_t, tile(REST_WIDTH), halo, tile(d),
                  _layer_spec(layer, (CONV_K, WIDTH)), _layer_spec(layer, (1, 3 * d)),
                  _layer_spec(layer, (WIDTH, d)), _layer_spec(layer, (WIDTH, d)),
                  _layer_spec(layer, (WIDTH, d)), _layer_spec(layer, (d, d))],
        out_specs=tile(d),
        out_shape=jax.ShapeDtypeStruct((b, s, d), _F32),
        compiler_params=_params("arbitrary", "arbitrary"),
        name="mix",
    )(oa, ob, rest, rest, x, convw, bg, wpa, wpb, wpc, wout)


FFN_T = 512
FF_CHUNK = 256


def _ffn_up_kernel(x_ref, g_ref, wg_ref, wu_ref, o_ref):
    h = _rms(x_ref[...], g_ref[0]).astype(_BF16)
    for n in range(D_FF // FF_CHUNK):
        cols = slice(n * FF_CHUNK, (n + 1) * FF_CHUNK)
        a = jnp.dot(h, wg_ref[0, :, cols], preferred_element_type=_F32)
        bgate = jnp.dot(h, wu_ref[0, :, cols], preferred_element_type=_F32)
        o_ref[:, cols] = (a * jax.nn.sigmoid(a) * bgate).astype(_BF16)


def _ffn_down_kernel(a_ref, x_ref, wd_ref, g_ref, o_ref, *, final_norm):
    y = x_ref[...] + jnp.dot(a_ref[...], wd_ref[0], preferred_element_type=_F32)
    if final_norm:
        y = _rms(y, g_ref[...])
    o_ref[...] = y


def _ffn(layer, x2d, g, wg, wu, wd, gfinal, final_norm):
    n, d = x2d.shape
    t = FFN_T
    act = pl.pallas_call(
        _ffn_up_kernel,
        grid=(n // t,),
        in_specs=[pl.BlockSpec((t, d), lambda i: (i, 0)), _layer_spec(layer, (1, d)),
                  _layer_spec(layer, (d, D_FF)), _layer_spec(layer, (d, D_FF))],
        out_specs=pl.BlockSpec((t, D_FF), lambda i: (i, 0)),
        out_shape=jax.ShapeDtypeStruct((n, D_FF), _BF16),
        compiler_params=_params("arbitrary"),
        name="ffn_up",
    )(x2d, g, wg, wu)
    return pl.pallas_call(
        functools.partial(_ffn_down_kernel, final_norm=final_norm),
        grid=(n // t,),
        in_specs=[pl.BlockSpec((t, D_FF), lambda i: (i, 0)),
                  pl.BlockSpec((t, d), lambda i: (i, 0)),
                  _layer_spec(layer, (D_FF, d)), _const_spec((1, d))],
        out_specs=pl.BlockSpec((t, d), lambda i: (i, 0)),
        out_shape=jax.ShapeDtypeStruct((n, d), _F32),
        compiler_params=_params("arbitrary"),
        name="ffn_down",
    )(act, x2d, wd, gfinal)


def _sb_permutation():
    rho = np.arange(KV_BLOCK)
    kappa = (rho % SUBLANES) * RUN + rho // SUBLANES
    p = np.zeros((KV_BLOCK, KV_BLOCK), np.float32)
    p[rho, kappa] = 1.0
    return jnp.asarray(p, _BF16)


def _moba_tables():
    slopes = np.exp2(-8.0 * (np.arange(N_HEADS, dtype=np.float64) + 1.0) / N_HEADS) * LOG2E
    kpos = np.zeros((KV_BLOCK, N_HEADS * PAIR), np.float32)
    qaug = np.zeros((N_HEADS, HEAD_DIM, MOBA_Q), np.float32)
    for hd in range(N_HEADS):
        rest = np.float64(slopes[hd])
        for a in range(N_AUG):
            piece = np.float64(np.float32(rest).astype(_BF16))
            kpos[:, hd * PAIR + HEAD_DIM + a] = np.arange(KV_BLOCK)
            qaug[hd, a, :] = piece
            rest = rest - piece
    future = np.arange(KV_BLOCK)[:, None] > np.arange(MOBA_Q)[None, :]
    causal = np.stack([np.zeros((KV_BLOCK, MOBA_Q)), np.where(future, -np.inf, 0.0)])
    slope_tab = slopes[:, None, None] * np.ones((1, SUBLANES, MOBA_Q))
    return (jnp.asarray(kpos), jnp.asarray(qaug, _BF16),
            jnp.asarray(slope_tab.astype(np.float32)), jnp.asarray(causal.astype(np.float32)))


def kernel(x, norm_mix_g, w_in, b_gate, conv_w, w_proj_moba, w_proj_sb, w_proj_conv, w_out,
           norm_ffn_g, w_ffn_gate, w_ffn_up, w_ffn_down, norm_final_g):
    depth = w_in.shape[0]
    b, s, d = x.shape
    scale = HEAD_DIM ** -0.5
    perm = _sb_permutation()
    kpos, qaug, slopes, causal = _moba_tables()
    col_scale = np.ones((w_in.shape[2],), np.float32)
    col_scale[0:WIDTH] = scale * LOG2E
    col_scale[3 * WIDTH:4 * WIDTH] = scale * LOG2E
    w_all = (w_in * col_scale).astype(_BF16)
    bf16 = lambda p: p.astype(_BF16)
    wpa, wpb, wpc, wout = bf16(w_proj_moba), bf16(w_proj_sb), bf16(w_proj_conv), bf16(w_out)
    wg, wu, wd = bf16(w_ffn_gate), bf16(w_ffn_up), bf16(w_ffn_down)
    g_mix, g_ffn, bg = norm_mix_g[:, None, :], norm_ffn_g[:, None, :], b_gate[:, None, :]
    for l in range(depth):
        qa_t, ka, va_t, kmean, kamax, qs_t, ks, vs_t, rest = _in_proj(
            l, x, g_mix, perm, kpos, w_all)
        oa = _moba(qa_t, qaug, ka, va_t, kmean, kamax, slopes, causal)
        ob = _stick_breaking(qs_t, ks, vs_t)
        x = _mix(l, oa, ob, rest, x, conv_w, bg, wpa, wpb, wpc, wout)
        x = _ffn(l, x.reshape(b * s, d), g_ffn, wg, wu, wd, norm_final_g[None, :],
                 final_norm=(l == depth - 1)).reshape(b, s, d)
    return x
```

```python
import functools

import jax
import jax.numpy as jnp
import numpy as np
from jax import lax
from jax.experimental import pallas as pl
from jax.experimental.pallas import tpu as pltpu

D_MODEL = 1024
HEAD_DIM = 64
N_HEADS = 8
WIDTH = N_HEADS * HEAD_DIM
CONV_K = 3
KV_BLOCK = 256
MOBA_TOPK = 3
MOBA_Q = 256
MOBA_G = 2
SB_Q = KV_BLOCK
SB_TILES = 2
MOBA_SKIP = -150.0
D_FF = 2816
RMS_EPS = 1e-6
REST_WIDTH = 3 * WIDTH + 3 * D_MODEL
PAIR = 2 * HEAD_DIM
V_ROWS = HEAD_DIM + 16
N_AUG = 4
SUBLANES = 8
LOG2E = 1.4426950408889634
RUN = KV_BLOCK // SUBLANES
SB_STOP = 1e-30
VMEM_LIMIT = 56 * 1024 * 1024

_TN = (((0,), (0,)), ((), ()))
_F32 = jnp.float32
_BF16 = jnp.bfloat16


def _params(*sem):
    return pltpu.CompilerParams(dimension_semantics=sem, vmem_limit_bytes=VMEM_LIMIT)


def _const_spec(shape):
    zeros = (0,) * len(shape)
    return pl.BlockSpec(shape, lambda *_: zeros)


def _layer_spec(layer, shape):
    index = (layer,) + (0,) * len(shape)
    return pl.BlockSpec((1,) + tuple(shape), lambda *_: index)


def _rms(x, g):
    y = x * lax.rsqrt(jnp.mean(x * x, axis=-1, keepdims=True) + RMS_EPS)
    return y * g


def _in_proj_kernel(x_ref, g_ref, perm_ref, kpos_ref, w_ref,
                    qa_ref, ka_ref, va_ref, kmean_ref, kamax_ref, qs_ref, ks_ref, vs_ref,
                    rest_ref):
    i = pl.program_id(1)
    h = _rms(x_ref[0], g_ref[0]).astype(_BF16)

    def proj(lhs, k):
        return jnp.dot(lhs, w_ref[0, :, k * WIDTH:(k + 1) * WIDTH], preferred_element_type=_F32)

    qa_ref[0] = proj(h, 0).T.astype(_BF16)
    ka = proj(h, 1)
    lane = lax.broadcasted_iota(jnp.int32, (KV_BLOCK, PAIR), 1)
    tiles = []
    for pair in range(N_HEADS // 2):
        both = ka[:, pair * PAIR:(pair + 1) * PAIR]
        tiles += [jnp.where(lane < HEAD_DIM, both, 0.0),
                  jnp.where(lane < HEAD_DIM, pltpu.roll(both, HEAD_DIM, 1), 0.0)]
    ka_wide = jnp.concatenate(tiles, axis=1)
    ka_ref[0, 0] = (ka_wide + kpos_ref[...]).astype(_BF16)
    kmean_ref[0, pl.ds(i, 1), :] = jnp.mean(ka_wide, axis=0, keepdims=True)
    kamax_ref[0, pl.ds(i, 1), :] = jnp.max(jnp.abs(ka_wide.astype(_BF16).astype(_F32)),
                                           axis=0, keepdims=True)
    va_t = proj(h, 2).T.astype(_BF16)
    extra = (lax.broadcasted_iota(jnp.int32, (V_ROWS - HEAD_DIM, KV_BLOCK), 0) == 0)
    extra = extra.astype(_F32).astype(_BF16)
    va_ref[0, 0] = jnp.concatenate(
        [piece for hd in range(N_HEADS)
         for piece in (va_t[hd * HEAD_DIM:(hd + 1) * HEAD_DIM, :], extra)], axis=0)
    qs_ref[0] = proj(h, 3).T.astype(_BF16)
    hp = jnp.dot(perm_ref[...], h, preferred_element_type=_F32).astype(_BF16)
    ks_ref[0, 0] = proj(hp, 4).astype(_BF16)
    vs_ref[0, 0] = proj(hp, 5).T.astype(_BF16)
    rest_ref[0] = jnp.dot(h, w_ref[0, :, 6 * WIDTH:], preferred_element_type=_F32).astype(_BF16)


def _in_proj(layer, x, g, perm, kpos, w_all):
    b, s, d = x.shape
    nb = s // KV_BLOCK
    t = KV_BLOCK
    kw = N_HEADS * PAIR
    out_shape = (
        jax.ShapeDtypeStruct((b, WIDTH, s), _BF16),
        jax.ShapeDtypeStruct((b, nb, t, kw), _BF16),
        jax.ShapeDtypeStruct((b, nb, N_HEADS * V_ROWS, t), _BF16),
        jax.ShapeDtypeStruct((b, nb, kw), _F32),
        jax.ShapeDtypeStruct((b, nb, kw), _F32),
        jax.ShapeDtypeStruct((b, WIDTH, s), _BF16),
        jax.ShapeDtypeStruct((b, nb, t, WIDTH), _BF16),
        jax.ShapeDtypeStruct((b, nb, WIDTH, t), _BF16),
        jax.ShapeDtypeStruct((b, s, REST_WIDTH), _BF16),
    )
    qt_spec = pl.BlockSpec((1, WIDTH, t), lambda bi, i: (bi, 0, i))
    k_spec = pl.BlockSpec((1, 1, t, WIDTH), lambda bi, i: (bi, i, 0, 0))
    vt_spec = pl.BlockSpec((1, 1, WIDTH, t), lambda bi, i: (bi, i, 0, 0))
    return pl.pallas_call(
        _in_proj_kernel,
        grid=(b, nb),
        in_specs=[
            pl.BlockSpec((1, t, d), lambda bi, i: (bi, i, 0)),
            _layer_spec(layer, (1, d)),
            _const_spec((t, t)),
            _const_spec((t, kw)),
            _layer_spec(layer, (d, 6 * WIDTH + REST_WIDTH)),
        ],
        out_specs=(
            qt_spec,
            pl.BlockSpec((1, 1, t, kw), lambda bi, i: (bi, i, 0, 0)),
            pl.BlockSpec((1, 1, N_HEADS * V_ROWS, t), lambda bi, i: (bi, i, 0, 0)),
            pl.BlockSpec((1, nb, kw), lambda bi, i: (bi, 0, 0)),
            pl.BlockSpec((1, nb, kw), lambda bi, i: (bi, 0, 0)),
            qt_spec, k_spec, vt_spec,
            pl.BlockSpec((1, t, REST_WIDTH), lambda bi, i: (bi, i, 0)),
        ),
        out_shape=out_shape,
        compiler_params=_params("arbitrary", "arbitrary"),
        name="in_proj",
    )(x, g, perm, kpos, w_all)


def _head_rows(q_pair, hh):
    row = lax.broadcasted_iota(jnp.int32, q_pair.shape, 0)
    keep = (row >= hh * HEAD_DIM) & (row < (hh + 1) * HEAD_DIM)
    return jnp.where(keep, q_pair, jnp.zeros_like(q_pair))


def _moba_kernel(q_ref, qaug_ref, k_ref, v_ref, kmean_ref, kamax_ref, slope_ref, causal_ref,
                 o_ref, sel_ref, sa_ref, sb_ref, acc_ref):
    blk = pl.program_id(2)
    nb = k_ref.shape[1]
    nidx = lax.broadcasted_iota(jnp.int32, (nb, MOBA_Q), 0)
    neg_inf = jnp.float32(-jnp.inf)
    slope = [slope_ref[hh, 0:1, :] for hh in range(2)]
    lanes = [slice(hh * PAIR, (hh + 1) * PAIR) for hh in range(2)]
    vrows = [slice(hh * V_ROWS, (hh + 1) * V_ROWS) for hh in range(2)]

    heads = []
    first_needed = jnp.int32(nb)
    for hh in range(2):
        qm = jnp.concatenate([q_ref[0, hh * HEAD_DIM:(hh + 1) * HEAD_DIM, :], qaug_ref[hh]], axis=0)
        km = kmean_ref[0, :, lanes[hh]].astype(_BF16)
        kamax = kamax_ref[0, :, lanes[hh]].astype(_BF16)
        bound = jnp.dot(kamax, jnp.abs(qm), preferred_element_type=_F32)
        own = jnp.max(jnp.where(nidx == blk, bound, 0.0), axis=0, keepdims=True)
        far = slope[hh] * ((nidx + 1 - blk) * KV_BLOCK).astype(_F32)
        needed = (nidx < blk) & (bound + own + far > MOBA_SKIP)
        first_needed = jnp.minimum(first_needed, jnp.min(jnp.where(needed, nidx, nb)))
        gate = jnp.dot(km, qm, preferred_element_type=_F32)
        g = jnp.where(nidx < blk, gate, neg_inf)
        sel = nidx == blk
        for r in range(MOBA_TOPK):
            mx = jnp.max(g, axis=0, keepdims=True)
            first = jnp.min(jnp.where(g == mx, nidx, nb), axis=0, keepdims=True)
            hit = nidx == first
            sel = sel | (hit & (jnp.full((nb, MOBA_Q), r, jnp.int32) < blk))
            g = jnp.where(hit, neg_inf, g)
        sel_ref[hh] = jnp.where(sel, 0.0, neg_inf)
        heads.append(qm)

    def shift(j):
        return jnp.full((1, MOBA_Q), (j - blk) * KV_BLOCK, jnp.int32).astype(_F32)

    def row_bias(hh, j):
        return sel_ref[hh, pl.ds(j, 1), :] + slope[hh] * shift(j)

    def scores(t, dst_ref, nearest=False):
        mx = []
        for hh in range(2):
            col = jnp.full((1, MOBA_Q), neg_inf, _F32)
            for g in range(MOBA_G):
                j = t * MOBA_G + g
                s = jnp.dot(k_ref[0, j, :, lanes[hh]], heads[hh], preferred_element_type=_F32)
                if nearest:
                    s = s + causal_ref[(j == blk).astype(jnp.int32)]
                dst_ref[hh, g * KV_BLOCK:(g + 1) * KV_BLOCK, :] = s
                col = jnp.maximum(col, jnp.max(s, axis=0, keepdims=True) + row_bias(hh, j))
            mx.append(col)
        return mx

    def accumulate(t, src_ref, mx, st):
        out = []
        for hh in range(2):
            m_new = jnp.maximum(st[hh], mx[hh])
            acc = jnp.exp2(st[hh] - m_new) * acc_ref[hh]
            for g in range(MOBA_G):
                j = t * MOBA_G + g
                s = src_ref[hh, g * KV_BLOCK:(g + 1) * KV_BLOCK, :]
                p = jnp.exp2(s - (m_new - row_bias(hh, j)))
                acc = acc + jnp.dot(v_ref[0, j, vrows[hh], :], p.astype(_BF16),
                                    preferred_element_type=_F32)
            acc_ref[hh] = acc
            out.append(m_new)
        return out

    last = blk // MOBA_G
    first = jnp.minimum(first_needed // MOBA_G, last)
    ntiles = last - first + 1
    state = [jnp.full((1, MOBA_Q), jnp.finfo(_F32).min, _F32) for _ in range(2)]
    for hh in range(2):
        acc_ref[hh] = jnp.zeros((V_ROWS, MOBA_Q), _F32)

    mx_a = scores(last, sa_ref, nearest=True)
    mx_b = scores(jnp.maximum(last - 1, 0), sb_ref)
    state = accumulate(last, sa_ref, mx_a, state)

    def body(k, carry):
        st, mx_b = carry[:2], carry[2:]
        t = last - 1 - 2 * k
        mx_a = scores(t - 1, sa_ref)
        st = accumulate(t, sb_ref, mx_b, st)
        mx_b = scores(jnp.maximum(t - 2, 0), sb_ref)
        st = accumulate(t - 1, sa_ref, mx_a, st)
        return tuple(st) + tuple(mx_b)

    carry = lax.fori_loop(0, (ntiles - 1) // 2, body, tuple(state) + tuple(mx_b))

    @pl.when(ntiles % 2 == 0)
    def _():
        accumulate(first, sb_ref, carry[2:], carry[:2])

    o_ref[0] = jnp.concatenate(
        [acc_ref[hh, 0:HEAD_DIM, :] / acc_ref[hh, HEAD_DIM:HEAD_DIM + 1, :] for hh in range(2)],
        axis=0).astype(_BF16)


def _moba(qa_t, qaug, ka, va_t, kmean, kamax, slopes, causal):
    b, _, s = qa_t.shape
    nb = s // KV_BLOCK
    npair = N_HEADS // 2
    assert nb % MOBA_G == 0
    return pl.pallas_call(
        _moba_kernel,
        grid=(b, npair, s // MOBA_Q),
        in_specs=[
            pl.BlockSpec((1, PAIR, MOBA_Q), lambda bi, hp, c: (bi, hp, c)),
            pl.BlockSpec((2, HEAD_DIM, MOBA_Q), lambda bi, hp, c: (hp, 0, 0)),
            pl.BlockSpec((1, nb, KV_BLOCK, 2 * PAIR), lambda bi, hp, c: (bi, 0, 0, hp)),
            pl.BlockSpec((1, nb, 2 * V_ROWS, KV_BLOCK), lambda bi, hp, c: (bi, 0, hp, 0)),
            pl.BlockSpec((1, nb, 2 * PAIR), lambda bi, hp, c: (bi, 0, hp)),
            pl.BlockSpec((1, nb, 2 * PAIR), lambda bi, hp, c: (bi, 0, hp)),
            pl.BlockSpec((2, SUBLANES, MOBA_Q), lambda bi, hp, c: (hp, 0, 0)),
            _const_spec((2, KV_BLOCK, MOBA_Q)),
        ],
        out_specs=pl.BlockSpec((1, PAIR, MOBA_Q), lambda bi, hp, c: (bi, hp, c)),
        out_shape=jax.ShapeDtypeStruct((b, WIDTH, s), _BF16),
        scratch_shapes=[pltpu.VMEM((2, nb, MOBA_Q), _F32),
                        pltpu.VMEM((2, MOBA_G * KV_BLOCK, MOBA_Q), _F32),
                        pltpu.VMEM((2, MOBA_G * KV_BLOCK, MOBA_Q), _F32),
                        pltpu.VMEM((2, V_ROWS, MOBA_Q), _F32)],
        compiler_params=_params("arbitrary", "arbitrary", "arbitrary"),
        name="moba",
    )(qa_t, qaug, ka, va_t, kmean, kamax, slopes, causal)


def _suffix_product_over_sublanes(x):
    sub = lax.broadcasted_iota(jnp.int32, x.shape, 0)
    y = x
    for d in (1, 2, 4):
        up = pltpu.roll(y, SUBLANES - d, 0)
        y = y * jnp.where(sub < SUBLANES - d, up, 1.0)
    return y


def _sb_scores(k_j, qm, mask):
    z = jnp.dot(k_j, qm, preferred_element_type=_F32)
    keep = 1.0 / (1.0 + jnp.exp2(z))
    if mask is not None:
        keep = jnp.where(mask, keep, 1.0)
    run = jnp.ones((SUBLANES, z.shape[1]), _F32)
    diff = [None] * RUN
    for r in reversed(range(RUN)):
        nxt = run * keep[r * SUBLANES:(r + 1) * SUBLANES, :]
        diff[r] = run - nxt
        run = nxt
    sub = lax.broadcasted_iota(jnp.int32, run.shape, 0)
    shifted = jnp.where(sub < SUBLANES - 1, pltpu.roll(run, SUBLANES - 1, 0), 1.0)
    return diff, run, _suffix_product_over_sublanes(shifted)


def _sb_weights(scores, v_jh, carry, acc):
    diff, run, later = scores
    base = later * carry
    w = jnp.concatenate([d * base for d in diff], axis=0)
    acc = acc + jnp.dot(v_jh, w.astype(_BF16), preferred_element_type=_F32)
    return carry * (later[0:1, :] * run[0:1, :]), acc


def _sb_kernel(q_ref, k_ref, v_ref, o_ref):
    rho = lax.broadcasted_iota(jnp.int32, (KV_BLOCK, SB_Q), 0)
    kpos = (rho & (SUBLANES - 1)) * RUN + (rho >> 3)
    qpos = lax.broadcasted_iota(jnp.int32, (KV_BLOCK, SB_Q), 1)
    strict = kpos < qpos
    rows = [slice(hh * HEAD_DIM, (hh + 1) * HEAD_DIM) for hh in range(2)]

    jds = [pl.program_id(2) * SB_TILES + u for u in range(SB_TILES)]
    jps = [jnp.maximum(jd - 1, 0) for jd in jds]
    heads = [[_head_rows(q_ref[0, :, u * SB_Q:(u + 1) * SB_Q], hh) for hh in range(2)]
             for u in range(SB_TILES)]
    diags = [[_sb_scores(k_ref[0, jds[u]], heads[u][hh], strict) for hh in range(2)]
             for u in range(SB_TILES)]
    prevs = [[_sb_scores(k_ref[0, jps[u]], heads[u][hh], None) for hh in range(2)]
             for u in range(SB_TILES)]
    states = []
    for u in range(SB_TILES):
        state = []
        for hh in range(2):
            carry, acc = _sb_weights(diags[u][hh], v_ref[0, jds[u], rows[hh], :],
                                     jnp.ones((1, SB_Q), _F32),
                                     jnp.zeros((HEAD_DIM, SB_Q), _F32))
            carry = jnp.where(jds[u] > 0, carry, 0.0)
            state += list(_sb_weights(prevs[u][hh], v_ref[0, jps[u], rows[hh], :], carry, acc))
        states.append(state)

    def more(trip, sts):
        flags = [jnp.logical_and(jds[u] - 2 - trip >= 0,
                                 jnp.maximum(jnp.max(sts[u][0]), jnp.max(sts[u][2])) > SB_STOP)
                 for u in range(SB_TILES)]
        return functools.reduce(jnp.logical_or, flags).astype(jnp.int32)

    def body(loop):
        trip, flat = loop[0], loop[2:]
        new = []
        for u in range(SB_TILES):
            j = jds[u] - 2 - trip
            jc = jnp.maximum(j, 0)
            out = []
            for hh in range(2):
                carry, acc = flat[4 * u + 2 * hh:4 * u + 2 * hh + 2]
                carry = jnp.where(j >= 0, carry, 0.0)
                out += list(_sb_weights(_sb_scores(k_ref[0, jc], heads[u][hh], None),
                                        v_ref[0, jc, rows[hh], :], carry, acc))
            new.append(out)
        return (trip + 1, more(trip + 1, new), *[a for st in new for a in st])

    zero = jnp.int32(0)
    final = lax.while_loop(lambda loop: loop[1] > 0, body,
                           (zero, more(zero, states), *[a for st in states for a in st]))[2:]
    for u in range(SB_TILES):
        o_ref[0, :, u * SB_Q:(u + 1) * SB_Q] = jnp.concatenate(
            [final[4 * u + 1], final[4 * u + 3]], axis=0).astype(_BF16)


def _stick_breaking(qs_t, ks, vs_t):
    b, _, s = qs_t.shape
    nb = s // KV_BLOCK
    npair = N_HEADS // 2
    return pl.pallas_call(
        _sb_kernel,
        grid=(b, npair, s // (SB_Q * SB_TILES)),
        in_specs=[
            pl.BlockSpec((1, PAIR, SB_Q * SB_TILES), lambda bi, hp, i: (bi, hp, i)),
            pl.BlockSpec((1, nb, KV_BLOCK, PAIR), lambda bi, hp, i: (bi, 0, 0, hp)),
            pl.BlockSpec((1, nb, PAIR, KV_BLOCK), lambda bi, hp, i: (bi, 0, hp, 0)),
        ],
        out_specs=pl.BlockSpec((1, PAIR, SB_Q * SB_TILES), lambda bi, hp, i: (bi, hp, i)),
        out_shape=jax.ShapeDtypeStruct((b, WIDTH, s), _BF16),
        compiler_params=_params("arbitrary", "arbitrary", "arbitrary"),
        name="stick_breaking",
    )(qs_t, ks, vs_t)


MIX_T = 512
HALO = 16


def _mix_kernel(oa_ref, ob_ref, rest_ref, halo_ref, x_ref, convw_ref, bg_ref,
                wpa_ref, wpb_ref, wpc_ref, wout_ref, o_ref):
    i = pl.program_id(1)
    r = rest_ref[0]
    xc = r[:, 0:WIDTH].astype(_F32)
    bc = r[:, WIDTH:2 * WIDTH].astype(_F32)
    cc = r[:, 2 * WIDTH:3 * WIDTH].astype(_F32)
    u = cc * xc
    hl = halo_ref[0]
    uh = hl[:, 2 * WIDTH:3 * WIDTH].astype(_F32) * hl[:, 0:WIDTH].astype(_F32)
    uh = jnp.where(i > 0, uh, 0.0)
    prev1 = uh[HALO - 1:HALO, :]
    prev2 = uh[HALO - 2:HALO - 1, :]
    row = lax.broadcasted_iota(jnp.int32, u.shape, 0)
    u1 = jnp.where(row == 0, prev1, pltpu.roll(u, 1, 0))
    u2 = jnp.where(row == 0, prev2, jnp.where(row == 1, prev1, pltpu.roll(u, 2, 0)))
    w = convw_ref[0]
    y = bc * (u2 * w[0:1, :] + u1 * w[1:2, :] + u * w[2:3, :])
    g = jax.nn.sigmoid(r[:, 3 * WIDTH:].astype(_F32) + bg_ref[0])
    d = D_MODEL
    merged = (g[:, 0:d] * lax.dot_general(oa_ref[0], wpa_ref[0], _TN,
                                          preferred_element_type=_F32)
              + g[:, d:2 * d] * lax.dot_general(ob_ref[0], wpb_ref[0], _TN,
                                                preferred_element_type=_F32)
              + g[:, 2 * d:] * jnp.dot(y.astype(_BF16), wpc_ref[0],
                                       preferred_element_type=_F32))
    o_ref[0] = x_ref[0] + jnp.dot(merged.astype(_BF16), wout_ref[0],
                                  preferred_element_type=_F32)


def _mix(layer, oa, ob, rest, x, convw, bg, wpa, wpb, wpc, wout):
    b, s, d = x.shape
    t = MIX_T
    tile = lambda width: pl.BlockSpec((1, t, width), lambda bi, i: (bi, i, 0))
    tile_t = pl.BlockSpec((1, WIDTH, t), lambda bi, i: (bi, 0, i))
    halo = pl.BlockSpec((1, HALO, REST_WIDTH),
                        lambda bi, i: (bi, jnp.maximum(i * (t // HALO) - 1, 0), 0))
    return pl.pallas_call(
        _mix_kernel,
        grid=(b, s // t),
        in_specs=[tile_t, tile_t, tile(REST_WIDTH), halo, tile(d),
                  _layer_spec(layer, (CONV_K, WIDTH)), _layer_spec(layer, (1, 3 * d)),
                  _layer_spec(layer, (WIDTH, d)), _layer_spec(layer, (WIDTH, d)),
                  _layer_spec(layer, (WIDTH, d)), _layer_spec(layer, (d, d))],
        out_specs=tile(d),
        out_shape=jax.ShapeDtypeStruct((b, s, d), _F32),
        compiler_params=_params("arbitrary", "arbitrary"),
        name="mix",
    )(oa, ob, rest, rest, x, convw, bg, wpa, wpb, wpc, wout)


FFN_T = 512
FF_CHUNK = 256


def _ffn_up_kernel(x_ref, g_ref, wg_ref, wu_ref, o_ref):
    h = _rms(x_ref[...], g_ref[0]).astype(_BF16)
    for n in range(D_FF // FF_CHUNK):
        cols = slice(n * FF_CHUNK, (n + 1) * FF_CHUNK)
        a = jnp.dot(h, wg_ref[0, :, cols], preferred_element_type=_F32)
        bgate = jnp.dot(h, wu_ref[0, :, cols], preferred_element_type=_F32)
        o_ref[:, cols] = (a * jax.nn.sigmoid(a) * bgate).astype(_BF16)


def _ffn_down_kernel(a_ref, x_ref, wd_ref, g_ref, o_ref, *, final_norm):
    y = x_ref[...] + jnp.dot(a_ref[...], wd_ref[0], preferred_element_type=_F32)
    if final_norm:
        y = _rms(y, g_ref[...])
    o_ref[...] = y


def _ffn(layer, x2d, g, wg, wu, wd, gfinal, final_norm):
    n, d = x2d.shape
    t = FFN_T
    act = pl.pallas_call(
        _ffn_up_kernel,
        grid=(n // t,),
        in_specs=[pl.BlockSpec((t, d), lambda i: (i, 0)), _layer_spec(layer, (1, d)),
                  _layer_spec(layer, (d, D_FF)), _layer_spec(layer, (d, D_FF))],
        out_specs=pl.BlockSpec((t, D_FF), lambda i: (i, 0)),
        out_shape=jax.ShapeDtypeStruct((n, D_FF), _BF16),
        compiler_params=_params("arbitrary"),
        name="ffn_up",
    )(x2d, g, wg, wu)
    return pl.pallas_call(
        functools.partial(_ffn_down_kernel, final_norm=final_norm),
        grid=(n // t,),
        in_specs=[pl.BlockSpec((t, D_FF), lambda i: (i, 0)),
                  pl.BlockSpec((t, d), lambda i: (i, 0)),
                  _layer_spec(layer, (D_FF, d)), _const_spec((1, d))],
        out_specs=pl.BlockSpec((t, d), lambda i: (i, 0)),
        out_shape=jax.ShapeDtypeStruct((n, d), _F32),
        compiler_params=_params("arbitrary"),
        name="ffn_down",
    )(act, x2d, wd, gfinal)


def _sb_permutation():
    rho = np.arange(KV_BLOCK)
    kappa = (rho % SUBLANES) * RUN + rho // SUBLANES
    p = np.zeros((KV_BLOCK, KV_BLOCK), np.float32)
    p[rho, kappa] = 1.0
    return jnp.asarray(p, _BF16)


def _moba_tables():
    slopes = np.exp2(-8.0 * (np.arange(N_HEADS, dtype=np.float64) + 1.0) / N_HEADS) * LOG2E
    kpos = np.zeros((KV_BLOCK, N_HEADS * PAIR), np.float32)
    qaug = np.zeros((N_HEADS, HEAD_DIM, MOBA_Q), np.float32)
    for hd in range(N_HEADS):
        rest = np.float64(slopes[hd])
        for a in range(N_AUG):
            piece = np.float64(np.float32(rest).astype(_BF16))
            kpos[:, hd * PAIR + HEAD_DIM + a] = np.arange(KV_BLOCK)
            qaug[hd, a, :] = piece
            rest = rest - piece
    future = np.arange(KV_BLOCK)[:, None] > np.arange(MOBA_Q)[None, :]
    causal = np.stack([np.zeros((KV_BLOCK, MOBA_Q)), np.where(future, -np.inf, 0.0)])
    slope_tab = slopes[:, None, None] * np.ones((1, SUBLANES, MOBA_Q))
    return (jnp.asarray(kpos), jnp.asarray(qaug, _BF16),
            jnp.asarray(slope_tab.astype(np.float32)), jnp.asarray(causal.astype(np.float32)))


def kernel(x, norm_mix_g, w_in, b_gate, conv_w, w_proj_moba, w_proj_sb, w_proj_conv, w_out,
           norm_ffn_g, w_ffn_gate, w_ffn_up, w_ffn_down, norm_final_g):
    depth = w_in.shape[0]
    b, s, d = x.shape
    scale = HEAD_DIM ** -0.5
    perm = _sb_permutation()
    kpos, qaug, slopes, causal = _moba_tables()
    col_scale = np.ones((w_in.shape[2],), np.float32)
    col_scale[0:WIDTH] = scale * LOG2E
    col_scale[3 * WIDTH:4 * WIDTH] = scale * LOG2E
    w_all = (w_in * col_scale).astype(_BF16)
    bf16 = lambda p: p.astype(_BF16)
    wpa, wpb, wpc, wout = bf16(w_proj_moba), bf16(w_proj_sb), bf16(w_proj_conv), bf16(w_out)
    wg, wu, wd = bf16(w_ffn_gate), bf16(w_ffn_up), bf16(w_ffn_down)
    g_mix, g_ffn, bg = norm_mix_g[:, None, :], norm_ffn_g[:, None, :], b_gate[:, None, :]
    for l in range(depth):
        qa_t, ka, va_t, kmean, kamax, qs_t, ks, vs_t, rest = _in_proj(
            l, x, g_mix, perm, kpos, w_all)
        oa = _moba(qa_t, qaug, ka, va_t, kmean, kamax, slopes, causal)
        ob = _stick_breaking(qs_t, ks, vs_t)
        x = _mix(l, oa, ob, rest, x, conv_w, bg, wpa, wpb, wpc, wout)
        x = _ffn(l, x.reshape(b * s, d), g_ffn, wg, wu, wd, norm_final_g[None, :],
                 final_norm=(l == depth - 1)).reshape(b, s, d)
    return x
```

```python
import functools

import jax
import jax.numpy as jnp
import numpy as np
from jax import lax
from jax.experimental import pallas as pl
from jax.experimental.pallas import tpu as pltpu

D_MODEL = 1024
HEAD_DIM = 64
N_HEADS = 8
WIDTH = N_HEADS * HEAD_DIM
CONV_K = 3
KV_BLOCK = 256
MOBA_TOPK = 3
MOBA_G = 2
MOBA_Q = MOBA_G * KV_BLOCK
SB_Q = KV_BLOCK
SB_TILES = 2
MOBA_SKIP = -150.0
D_FF = 2816
RMS_EPS = 1e-6
REST_WIDTH = 3 * WIDTH + 3 * D_MODEL
PAIR = 2 * HEAD_DIM
V_ROWS = HEAD_DIM + 16
N_AUG = 4
SUBLANES = 8
LOG2E = 1.4426950408889634
RUN = KV_BLOCK // SUBLANES
SB_STOP = 1e-30
VMEM_LIMIT = 56 * 1024 * 1024

_TN = (((0,), (0,)), ((), ()))
_F32 = jnp.float32
_BF16 = jnp.bfloat16


def _params(*sem):
    return pltpu.CompilerParams(dimension_semantics=sem, vmem_limit_bytes=VMEM_LIMIT)


def _const_spec(shape):
    zeros = (0,) * len(shape)
    return pl.BlockSpec(shape, lambda *_: zeros)


def _layer_spec(layer, shape):
    index = (layer,) + (0,) * len(shape)
    return pl.BlockSpec((1,) + tuple(shape), lambda *_: index)


def _rms(x, g):
    y = x * lax.rsqrt(jnp.mean(x * x, axis=-1, keepdims=True) + RMS_EPS)
    return y * g


def _in_proj_kernel(x_ref, g_ref, perm_ref, kpos_ref, w_ref,
                    qa_ref, ka_ref, va_ref, kmean_ref, kamax_ref, qs_ref, ks_ref, vs_ref,
                    rest_ref):
    i = pl.program_id(1)
    h = _rms(x_ref[0], g_ref[0]).astype(_BF16)

    def proj(lhs, k):
        return jnp.dot(lhs, w_ref[0, :, k * WIDTH:(k + 1) * WIDTH], preferred_element_type=_F32)

    qa_ref[0] = proj(h, 0).T.astype(_BF16)
    ka = proj(h, 1)
    lane = lax.broadcasted_iota(jnp.int32, (KV_BLOCK, PAIR), 1)
    tiles = []
    for pair in range(N_HEADS // 2):
        both = ka[:, pair * PAIR:(pair + 1) * PAIR]
        tiles += [jnp.where(lane < HEAD_DIM, both, 0.0),
                  jnp.where(lane < HEAD_DIM, pltpu.roll(both, HEAD_DIM, 1), 0.0)]
    ka_wide = jnp.concatenate(tiles, axis=1)
    ka_ref[0, 0] = (ka_wide + kpos_ref[...]).astype(_BF16)
    kmean_ref[0, pl.ds(i, 1), :] = jnp.mean(ka_wide, axis=0, keepdims=True)
    kamax_ref[0, pl.ds(i, 1), :] = jnp.max(jnp.abs(ka_wide.astype(_BF16).astype(_F32)),
                                           axis=0, keepdims=True)
    va_t = proj(h, 2).T.astype(_BF16)
    extra = (lax.broadcasted_iota(jnp.int32, (V_ROWS - HEAD_DIM, KV_BLOCK), 0) == 0)
    extra = extra.astype(_F32).astype(_BF16)
    va_ref[0, 0] = jnp.concatenate(
        [piece for hd in range(N_HEADS)
         for piece in (va_t[hd * HEAD_DIM:(hd + 1) * HEAD_DIM, :], extra)], axis=0)
    qs_ref[0] = proj(h, 3).T.astype(_BF16)
    hp = jnp.dot(perm_ref[...], h, preferred_element_type=_F32).astype(_BF16)
    ks_ref[0, 0] = proj(hp, 4).astype(_BF16)
    vs_ref[0, 0] = proj(hp, 5).T.astype(_BF16)
    rest_ref[0] = jnp.dot(h, w_ref[0, :, 6 * WIDTH:], preferred_element_type=_F32).astype(_BF16)


def _in_proj(layer, x, g, perm, kpos, w_all):
    b, s, d = x.shape
    nb = s // KV_BLOCK
    t = KV_BLOCK
    kw = N_HEADS * PAIR
    out_shape = (
        jax.ShapeDtypeStruct((b, WIDTH, s), _BF16),
        jax.ShapeDtypeStruct((b, nb, t, kw), _BF16),
        jax.ShapeDtypeStruct((b, nb, N_HEADS * V_ROWS, t), _BF16),
        jax.ShapeDtypeStruct((b, nb, kw), _F32),
        jax.ShapeDtypeStruct((b, nb, kw), _F32),
        jax.ShapeDtypeStruct((b, WIDTH, s), _BF16),
        jax.ShapeDtypeStruct((b, nb, t, WIDTH), _BF16),
        jax.ShapeDtypeStruct((b, nb, WIDTH, t), _BF16),
        jax.ShapeDtypeStruct((b, s, REST_WIDTH), _BF16),
    )
    qt_spec = pl.BlockSpec((1, WIDTH, t), lambda bi, i: (bi, 0, i))
    k_spec = pl.BlockSpec((1, 1, t, WIDTH), lambda bi, i: (bi, i, 0, 0))
    vt_spec = pl.BlockSpec((1, 1, WIDTH, t), lambda bi, i: (bi, i, 0, 0))
    return pl.pallas_call(
        _in_proj_kernel,
        grid=(b, nb),
        in_specs=[
            pl.BlockSpec((1, t, d), lambda bi, i: (bi, i, 0)),
            _layer_spec(layer, (1, d)),
            _const_spec((t, t)),
            _const_spec((t, kw)),
            _layer_spec(layer, (d, 6 * WIDTH + REST_WIDTH)),
        ],
        out_specs=(
            qt_spec,
            pl.BlockSpec((1, 1, t, kw), lambda bi, i: (bi, i, 0, 0)),
            pl.BlockSpec((1, 1, N_HEADS * V_ROWS, t), lambda bi, i: (bi, i, 0, 0)),
            pl.BlockSpec((1, nb, kw), lambda bi, i: (bi, 0, 0)),
            pl.BlockSpec((1, nb, kw), lambda bi, i: (bi, 0, 0)),
            qt_spec, k_spec, vt_spec,
            pl.BlockSpec((1, t, REST_WIDTH), lambda bi, i: (bi, i, 0)),
        ),
        out_shape=out_shape,
        compiler_params=_params("arbitrary", "arbitrary"),
        name="in_proj",
    )(x, g, perm, kpos, w_all)


def _head_rows(q_pair, hh):
    row = lax.broadcasted_iota(jnp.int32, q_pair.shape, 0)
    keep = (row >= hh * HEAD_DIM) & (row < (hh + 1) * HEAD_DIM)
    return jnp.where(keep, q_pair, jnp.zeros_like(q_pair))


def _moba_kernel(q_ref, qaug_ref, k_ref, v_ref, kmean_ref, kamax_ref, slope_ref, causal_ref,
                 o_ref, sel_ref, sa_ref, sb_ref, acc_ref):
    last = pl.program_id(2)
    nb = k_ref.shape[1]
    nidx = lax.broadcasted_iota(jnp.int32, (nb, MOBA_Q), 0)
    blk = last * MOBA_G + (lax.broadcasted_iota(jnp.int32, (nb, MOBA_Q), 1)
                           >> (KV_BLOCK.bit_length() - 1))
    blk_row = blk[0:1, :]
    neg_inf = jnp.float32(-jnp.inf)
    slope = [slope_ref[hh, 0:1, :] for hh in range(2)]
    lanes = [slice(hh * PAIR, (hh + 1) * PAIR) for hh in range(2)]
    vrows = [slice(hh * V_ROWS, (hh + 1) * V_ROWS) for hh in range(2)]

    heads = []
    first_needed = jnp.int32(nb)
    for hh in range(2):
        qm = jnp.concatenate([q_ref[0, hh * HEAD_DIM:(hh + 1) * HEAD_DIM, :], qaug_ref[hh]], axis=0)
        km = kmean_ref[0, :, lanes[hh]].astype(_BF16)
        kamax = kamax_ref[0, :, lanes[hh]].astype(_BF16)
        bound = jnp.dot(kamax, jnp.abs(qm), preferred_element_type=_F32)
        own = jnp.max(jnp.where(nidx == blk, bound, 0.0), axis=0, keepdims=True)
        far = slope[hh] * ((nidx + 1 - blk) * KV_BLOCK).astype(_F32)
        needed = (nidx < blk) & (bound + own + far > MOBA_SKIP)
        first_needed = jnp.minimum(first_needed, jnp.min(jnp.where(needed, nidx, nb)))
        gate = jnp.dot(km, qm, preferred_element_type=_F32)
        g = jnp.where(nidx < blk, gate, neg_inf)
        sel = nidx == blk
        for r in range(MOBA_TOPK):
            mx = jnp.max(g, axis=0, keepdims=True)
            first = jnp.min(jnp.where(g == mx, nidx, nb), axis=0, keepdims=True)
            hit = nidx == first
            sel = sel | (hit & (jnp.full((nb, MOBA_Q), r, jnp.int32) < blk))
            g = jnp.where(hit, neg_inf, g)
        sel_ref[hh] = jnp.where(sel, 0.0, neg_inf)
        heads.append(qm)

    def shift(j):
        return ((j - blk_row) * KV_BLOCK).astype(_F32)

    def row_bias(hh, j):
        return sel_ref[hh, pl.ds(j, 1), :] + slope[hh] * shift(j)

    def scores(t, dst_ref, nearest=False):
        mx = []
        for hh in range(2):
            col = jnp.full((1, MOBA_Q), neg_inf, _F32)
            for g in range(MOBA_G):
                j = t * MOBA_G + g
                s = jnp.dot(k_ref[0, j, :, lanes[hh]], heads[hh], preferred_element_type=_F32)
                if nearest:
                    s = s + causal_ref[g]
                dst_ref[hh, g * KV_BLOCK:(g + 1) * KV_BLOCK, :] = s
                col = jnp.maximum(col, jnp.max(s, axis=0, keepdims=True) + row_bias(hh, j))
            mx.append(col)
        return mx

    def accumulate(t, src_ref, mx, st):
        out = []
        for hh in range(2):
            m_new = jnp.maximum(st[hh], mx[hh])
            acc = jnp.exp2(st[hh] - m_new) * acc_ref[hh]
            for g in range(MOBA_G):
                j = t * MOBA_G + g
                s = src_ref[hh, g * KV_BLOCK:(g + 1) * KV_BLOCK, :]
                p = jnp.exp2(s - (m_new - row_bias(hh, j)))
                acc = acc + jnp.dot(v_ref[0, j, vrows[hh], :], p.astype(_BF16),
                                    preferred_element_type=_F32)
            acc_ref[hh] = acc
            out.append(m_new)
        return out

    first = jnp.minimum(first_needed // MOBA_G, last)
    ntiles = last - first + 1
    state = [jnp.full((1, MOBA_Q), jnp.finfo(_F32).min, _F32) for _ in range(2)]
    for hh in range(2):
        acc_ref[hh] = jnp.zeros((V_ROWS, MOBA_Q), _F32)

    mx_a = scores(last, sa_ref, nearest=True)
    mx_b = scores(jnp.maximum(last - 1, 0), sb_ref)
    state = accumulate(last, sa_ref, mx_a, state)

    def body(k, carry):
        st, mx_b = carry[:2], carry[2:]
        t = last - 1 - 2 * k
        mx_a = scores(t - 1, sa_ref)
        st = accumulate(t, sb_ref, mx_b, st)
        mx_b = scores(jnp.maximum(t - 2, 0), sb_ref)
        st = accumulate(t - 1, sa_ref, mx_a, st)
        return tuple(st) + tuple(mx_b)

    carry = lax.fori_loop(0, (ntiles - 1) // 2, body, tuple(state) + tuple(mx_b))

    @pl.when(ntiles % 2 == 0)
    def _():
        accumulate(first, sb_ref, carry[2:], carry[:2])

    o_ref[0] = jnp.concatenate(
        [acc_ref[hh, 0:HEAD_DIM, :] / acc_ref[hh, HEAD_DIM:HEAD_DIM + 1, :] for hh in range(2)],
        axis=0).astype(_BF16)


def _moba(qa_t, qaug, ka, va_t, kmean, kamax, slopes, causal):
    b, _, s = qa_t.shape
    nb = s // KV_BLOCK
    npair = N_HEADS // 2
    assert nb % MOBA_G == 0
    return pl.pallas_call(
        _moba_kernel,
        grid=(b, npair, s // MOBA_Q),
        in_specs=[
            pl.BlockSpec((1, PAIR, MOBA_Q), lambda bi, hp, c: (bi, hp, c)),
            pl.BlockSpec((2, HEAD_DIM, MOBA_Q), lambda bi, hp, c: (hp, 0, 0)),
            pl.BlockSpec((1, nb, KV_BLOCK, 2 * PAIR), lambda bi, hp, c: (bi, 0, 0, hp)),
            pl.BlockSpec((1, nb, 2 * V_ROWS, KV_BLOCK), lambda bi, hp, c: (bi, 0, hp, 0)),
            pl.BlockSpec((1, nb, 2 * PAIR), lambda bi, hp, c: (bi, 0, hp)),
            pl.BlockSpec((1, nb, 2 * PAIR), lambda bi, hp, c: (bi, 0, hp)),
            pl.BlockSpec((2, SUBLANES, MOBA_Q), lambda bi, hp, c: (hp, 0, 0)),
            _const_spec((MOBA_G, KV_BLOCK, MOBA_Q)),
        ],
        out_specs=pl.BlockSpec((1, PAIR, MOBA_Q), lambda bi, hp, c: (bi, hp, c)),
        out_shape=jax.ShapeDtypeStruct((b, WIDTH, s), _BF16),
        scratch_shapes=[pltpu.VMEM((2, nb, MOBA_Q), _F32),
                        pltpu.VMEM((2, MOBA_G * KV_BLOCK, MOBA_Q), _F32),
                        pltpu.VMEM((2, MOBA_G * KV_BLOCK, MOBA_Q), _F32),
                        pltpu.VMEM((2, V_ROWS, MOBA_Q), _F32)],
        compiler_params=_params("arbitrary", "arbitrary", "arbitrary"),
        name="moba",
    )(qa_t, qaug, ka, va_t, kmean, kamax, slopes, causal)


def _suffix_product_over_sublanes(x):
    sub = lax.broadcasted_iota(jnp.int32, x.shape, 0)
    y = x
    for d in (1, 2, 4):
        up = pltpu.roll(y, SUBLANES - d, 0)
        y = y * jnp.where(sub < SUBLANES - d, up, 1.0)
    return y


def _sb_scores(k_j, qm, mask):
    z = jnp.dot(k_j, qm, preferred_element_type=_F32)
    keep = 1.0 / (1.0 + jnp.exp2(z))
    if mask is not None:
        keep = jnp.where(mask, keep, 1.0)
    run = jnp.ones((SUBLANES, z.shape[1]), _F32)
    diff = [None] * RUN
    for r in reversed(range(RUN)):
        nxt = run * keep[r * SUBLANES:(r + 1) * SUBLANES, :]
        diff[r] = run - nxt
        run = nxt
    sub = lax.broadcasted_iota(jnp.int32, run.shape, 0)
    shifted = jnp.where(sub < SUBLANES - 1, pltpu.roll(run, SUBLANES - 1, 0), 1.0)
    return diff, run, _suffix_product_over_sublanes(shifted)


def _sb_weights(scores, v_jh, carry, acc):
    diff, run, later = scores
    base = later * carry
    w = jnp.concatenate([d * base for d in diff], axis=0)
    acc = acc + jnp.dot(v_jh, w.astype(_BF16), preferred_element_type=_F32)
    return carry * (later[0:1, :] * run[0:1, :]), acc


def _sb_kernel(q_ref, k_ref, v_ref, o_ref):
    rho = lax.broadcasted_iota(jnp.int32, (KV_BLOCK, SB_Q), 0)
    kpos = (rho & (SUBLANES - 1)) * RUN + (rho >> 3)
    qpos = lax.broadcasted_iota(jnp.int32, (KV_BLOCK, SB_Q), 1)
    strict = kpos < qpos
    rows = [slice(hh * HEAD_DIM, (hh + 1) * HEAD_DIM) for hh in range(2)]

    jds = [pl.program_id(2) * SB_TILES + u for u in range(SB_TILES)]
    jps = [jnp.maximum(jd - 1, 0) for jd in jds]
    heads = [[_head_rows(q_ref[0, :, u * SB_Q:(u + 1) * SB_Q], hh) for hh in range(2)]
             for u in range(SB_TILES)]
    diags = [[_sb_scores(k_ref[0, jds[u]], heads[u][hh], strict) for hh in range(2)]
             for u in range(SB_TILES)]
    prevs = [[_sb_scores(k_ref[0, jps[u]], heads[u][hh], None) for hh in range(2)]
             for u in range(SB_TILES)]
    states = []
    for u in range(SB_TILES):
        state = []
        for hh in range(2):
            carry, acc = _sb_weights(diags[u][hh], v_ref[0, jds[u], rows[hh], :],
                                     jnp.ones((1, SB_Q), _F32),
                                     jnp.zeros((HEAD_DIM, SB_Q), _F32))
            carry = jnp.where(jds[u] > 0, carry, 0.0)
            state += list(_sb_weights(prevs[u][hh], v_ref[0, jps[u], rows[hh], :], carry, acc))
        states.append(state)

    def more(trip, sts):
        flags = [jnp.logical_and(jds[u] - 2 - trip >= 0,
                                 jnp.maximum(jnp.max(sts[u][0]), jnp.max(sts[u][2])) > SB_STOP)
                 for u in range(SB_TILES)]
        return functools.reduce(jnp.logical_or, flags).astype(jnp.int32)

    def body(loop):
        trip, flat = loop[0], loop[2:]
        new = []
        for u in range(SB_TILES):
            j = jds[u] - 2 - trip
            jc = jnp.maximum(j, 0)
            out = []
            for hh in range(2):
                carry, acc = flat[4 * u + 2 * hh:4 * u + 2 * hh + 2]
                carry = jnp.where(j >= 0, carry, 0.0)
                out += list(_sb_weights(_sb_scores(k_ref[0, jc], heads[u][hh], None),
                                        v_ref[0, jc, rows[hh], :], carry, acc))
            new.append(out)
        return (trip + 1, more(trip + 1, new), *[a for st in new for a in st])

    zero = jnp.int32(0)
    final = lax.while_loop(lambda loop: loop[1] > 0, body,
                           (zero, more(zero, states), *[a for st in states for a in st]))[2:]
    for u in range(SB_TILES):
        o_ref[0, :, u * SB_Q:(u + 1) * SB_Q] = jnp.concatenate(
            [final[4 * u + 1], final[4 * u + 3]], axis=0).astype(_BF16)


def _stick_breaking(qs_t, ks, vs_t):
    b, _, s = qs_t.shape
    nb = s // KV_BLOCK
    npair = N_HEADS // 2
    return pl.pallas_call(
        _sb_kernel,
        grid=(b, npair, s // (SB_Q * SB_TILES)),
        in_specs=[
            pl.BlockSpec((1, PAIR, SB_Q * SB_TILES), lambda bi, hp, i: (bi, hp, i)),
            pl.BlockSpec((1, nb, KV_BLOCK, PAIR), lambda bi, hp, i: (bi, 0, 0, hp)),
            pl.BlockSpec((1, nb, PAIR, KV_BLOCK), lambda bi, hp, i: (bi, 0, hp, 0)),
        ],
        out_specs=pl.BlockSpec((1, PAIR, SB_Q * SB_TILES), lambda bi, hp, i: (bi, hp, i)),
        out_shape=jax.ShapeDtypeStruct((b, WIDTH, s), _BF16),
        compiler_params=_params("arbitrary", "arbitrary", "arbitrary"),
        name="stick_breaking",
    )(qs_t, ks, vs_t)


MIX_T = 512
HALO = 16


def _mix_kernel(oa_ref, ob_ref, rest_ref, halo_ref, x_ref, convw_ref, bg_ref,
                wpa_ref, wpb_ref, wpc_ref, wout_ref, o_ref):
    i = pl.program_id(1)
    r = rest_ref[0]
    xc = r[:, 0:WIDTH].astype(_F32)
    bc = r[:, WIDTH:2 * WIDTH].astype(_F32)
    cc = r[:, 2 * WIDTH:3 * WIDTH].astype(_F32)
    u = cc * xc
    hl = halo_ref[0]
    uh = hl[:, 2 * WIDTH:3 * WIDTH].astype(_F32) * hl[:, 0:WIDTH].astype(_F32)
    uh = jnp.where(i > 0, uh, 0.0)
    prev1 = uh[HALO - 1:HALO, :]
    prev2 = uh[HALO - 2:HALO - 1, :]
    row = lax.broadcasted_iota(jnp.int32, u.shape, 0)
    u1 = jnp.where(row == 0, prev1, pltpu.roll(u, 1, 0))
    u2 = jnp.where(row == 0, prev2, jnp.where(row == 1, prev1, pltpu.roll(u, 2, 0)))
    w = convw_ref[0]
    y = bc * (u2 * w[0:1, :] + u1 * w[1:2, :] + u * w[2:3, :])
    g = jax.nn.sigmoid(r[:, 3 * WIDTH:].astype(_F32) + bg_ref[0])
    d = D_MODEL
    merged = (g[:, 0:d] * lax.dot_general(oa_ref[0], wpa_ref[0], _TN,
                                          preferred_element_type=_F32)
              + g[:, d:2 * d] * lax.dot_general(ob_ref[0], wpb_ref[0], _TN,
                                                preferred_element_type=_F32)
              + g[:, 2 * d:] * jnp.dot(y.astype(_BF16), wpc_ref[0],
                                       preferred_element_type=_F32))
    o_ref[0] = x_ref[0] + jnp.dot(merged.astype(_BF16), wout_ref[0],
                                  preferred_element_type=_F32)


def _mix(layer, oa, ob, rest, x, convw, bg, wpa, wpb, wpc, wout):
    b, s, d = x.shape
    t = MIX_T
    tile = lambda width: pl.BlockSpec((1, t, width), lambda bi, i: (bi, i, 0))
    tile_t = pl.BlockSpec((1, WIDTH, t), lambda bi, i: (bi, 0, i))
    halo = pl.BlockSpec((1, HALO, REST_WIDTH),
                        lambda bi, i: (bi, jnp.maximum(i * (t // HALO) - 1, 0), 0))
    return pl.pallas_call(
        _mix_kernel,
        grid=(b, s // t),
        in_specs=[tile_t, tile_t, tile(REST_WIDTH), halo, tile(d),
                  _layer_spec(layer, (CONV_K, WIDTH)), _layer_spec(layer, (1, 3 * d)),
                  _layer_spec(layer, (WIDTH, d)), _layer_spec(layer, (WIDTH, d)),
                  _layer_spec(layer, (WIDTH, d)), _layer_spec(layer, (d, d))],
        out_specs=tile(d),
        out_shape=jax.ShapeDtypeStruct((b, s, d), _F32),
        compiler_params=_params("arbitrary", "arbitrary"),
        name="mix",
    )(oa, ob, rest, rest, x, convw, bg, wpa, wpb, wpc, wout)


FFN_T = 512
FF_CHUNK = 256


def _ffn_up_kernel(x_ref, g_ref, wg_ref, wu_ref, o_ref):
    h = _rms(x_ref[...], g_ref[0]).astype(_BF16)
    for n in range(D_FF // FF_CHUNK):
        cols = slice(n * FF_CHUNK, (n + 1) * FF_CHUNK)
        a = jnp.dot(h, wg_ref[0, :, cols], preferred_element_type=_F32)
        bgate = jnp.dot(h, wu_ref[0, :, cols], preferred_element_type=_F32)
        o_ref[:, cols] = (a * jax.nn.sigmoid(a) * bgate).astype(_BF16)


def _ffn_down_kernel(a_ref, x_ref, wd_ref, g_ref, o_ref, *, final_norm):
    y = x_ref[...] + jnp.dot(a_ref[...], wd_ref[0], preferred_element_type=_F32)
    if final_norm:
        y = _rms(y, g_ref[...])
    o_ref[...] = y


def _ffn(layer, x2d, g, wg, wu, wd, gfinal, final_norm):
    n, d = x2d.shape
    t = FFN_T
    act = pl.pallas_call(
        _ffn_up_kernel,
        grid=(n // t,),
        in_specs=[pl.BlockSpec((t, d), lambda i: (i, 0)), _layer_spec(layer, (1, d)),
                  _layer_spec(layer, (d, D_FF)), _layer_spec(layer, (d, D_FF))],
        out_specs=pl.BlockSpec((t, D_FF), lambda i: (i, 0)),
        out_shape=jax.ShapeDtypeStruct((n, D_FF), _BF16),
        compiler_params=_params("arbitrary"),
        name="ffn_up",
    )(x2d, g, wg, wu)
    return pl.pallas_call(
        functools.partial(_ffn_down_kernel, final_norm=final_norm),
        grid=(n // t,),
        in_specs=[pl.BlockSpec((t, D_FF), lambda i: (i, 0)),
                  pl.BlockSpec((t, d), lambda i: (i, 0)),
                  _layer_spec(layer, (D_FF, d)), _const_spec((1, d))],
        out_specs=pl.BlockSpec((t, d), lambda i: (i, 0)),
        out_shape=jax.ShapeDtypeStruct((n, d), _F32),
        compiler_params=_params("arbitrary"),
        name="ffn_down",
    )(act, x2d, wd, gfinal)


def _sb_permutation():
    rho = np.arange(KV_BLOCK)
    kappa = (rho % SUBLANES) * RUN + rho // SUBLANES
    p = np.zeros((KV_BLOCK, KV_BLOCK), np.float32)
    p[rho, kappa] = 1.0
    return jnp.asarray(p, _BF16)


def _moba_tables():
    slopes = np.exp2(-8.0 * (np.arange(N_HEADS, dtype=np.float64) + 1.0) / N_HEADS) * LOG2E
    kpos = np.zeros((KV_BLOCK, N_HEADS * PAIR), np.float32)
    qaug = np.zeros((N_HEADS, HEAD_DIM, MOBA_Q), np.float32)
    for hd in range(N_HEADS):
        rest = np.float64(slopes[hd])
        for a in range(N_AUG):
            piece = np.float64(np.float32(rest).astype(_BF16))
            kpos[:, hd * PAIR + HEAD_DIM + a] = np.arange(KV_BLOCK)
            qaug[hd, a, :] = piece
            rest = rest - piece
    key = np.arange(KV_BLOCK)[:, None]
    q_blk, q_off = np.divmod(np.arange(MOBA_Q)[None, :], KV_BLOCK)
    causal = np.stack([np.where((q_blk > g) | ((q_blk == g) & (key <= q_off)), 0.0, -np.inf)
                       for g in range(MOBA_G)])
    slope_tab = slopes[:, None, None] * np.ones((1, SUBLANES, MOBA_Q))
    return (jnp.asarray(kpos), jnp.asarray(qaug, _BF16),
            jnp.asarray(slope_tab.astype(np.float32)), jnp.asarray(causal.astype(np.float32)))


def kernel(x, norm_mix_g, w_in, b_gate, conv_w, w_proj_moba, w_proj_sb, w_proj_conv, w_out,
           norm_ffn_g, w_ffn_gate, w_ffn_up, w_ffn_down, norm_final_g):
    depth = w_in.shape[0]
    b, s, d = x.shape
    scale = HEAD_DIM ** -0.5
    perm = _sb_permutation()
    kpos, qaug, slopes, causal = _moba_tables()
    col_scale = np.ones((w_in.shape[2],), np.float32)
    col_scale[0:WIDTH] = scale * LOG2E
    col_scale[3 * WIDTH:4 * WIDTH] = scale * LOG2E
    w_all = (w_in * col_scale).astype(_BF16)
    bf16 = lambda p: p.astype(_BF16)
    wpa, wpb, wpc, wout = bf16(w_proj_moba), bf16(w_proj_sb), bf16(w_proj_conv), bf16(w_out)
    wg, wu, wd = bf16(w_ffn_gate), bf16(w_ffn_up), bf16(w_ffn_down)
    g_mix, g_ffn, bg = norm_mix_g[:, None, :], norm_ffn_g[:, None, :], b_gate[:, None, :]
    for l in range(depth):
        qa_t, ka, va_t, kmean, kamax, qs_t, ks, vs_t, rest = _in_proj(
            l, x, g_mix, perm, kpos, w_all)
        oa = _moba(qa_t, qaug, ka, va_t, kmean, kamax, slopes, causal)
        ob = _stick_breaking(qs_t, ks, vs_t)
        x = _mix(l, oa, ob, rest, x, conv_w, bg, wpa, wpb, wpc, wout)
        x = _ffn(l, x.reshape(b * s, d), g_ffn, wg, wu, wd, norm_final_g[None, :],
                 final_norm=(l == depth - 1)).reshape(b, s, d)
    return x
```

```python
import functools

import jax
import jax.numpy as jnp
import numpy as np
from jax import lax
from jax.experimental import pallas as pl
from jax.experimental.pallas import tpu as pltpu

D_MODEL = 1024
HEAD_DIM = 64
N_HEADS = 8
WIDTH = N_HEADS * HEAD_DIM
CONV_K = 3
KV_BLOCK = 256
MOBA_TOPK = 3
MOBA_G = 2
MOBA_Q = MOBA_G * KV_BLOCK
SB_Q = KV_BLOCK
SB_TILES = 4
MOBA_SKIP = -150.0
D_FF = 2816
RMS_EPS = 1e-6
REST_WIDTH = 3 * WIDTH + 3 * D_MODEL
PAIR = 2 * HEAD_DIM
V_ROWS = HEAD_DIM + 16
N_AUG = 4
SUBLANES = 8
LOG2E = 1.4426950408889634
RUN = KV_BLOCK // SUBLANES
SB_STOP = 1e-30
VMEM_LIMIT = 56 * 1024 * 1024

_TN = (((0,), (0,)), ((), ()))
_F32 = jnp.float32
_BF16 = jnp.bfloat16


def _params(*sem):
    return pltpu.CompilerParams(dimension_semantics=sem, vmem_limit_bytes=VMEM_LIMIT)


def _const_spec(shape):
    zeros = (0,) * len(shape)
    return pl.BlockSpec(shape, lambda *_: zeros)


def _layer_spec(layer, shape):
    index = (layer,) + (0,) * len(shape)
    return pl.BlockSpec((1,) + tuple(shape), lambda *_: index)


def _rms(x, g):
    y = x * lax.rsqrt(jnp.mean(x * x, axis=-1, keepdims=True) + RMS_EPS)
    return y * g


def _in_proj_kernel(x_ref, g_ref, perm_ref, kpos_ref, w_ref,
                    qa_ref, ka_ref, va_ref, kmean_ref, kamax_ref, qs_ref, ks_ref, vs_ref,
                    rest_ref):
    i = pl.program_id(1)
    h = _rms(x_ref[0], g_ref[0]).astype(_BF16)

    def proj(lhs, k):
        return jnp.dot(lhs, w_ref[0, :, k * WIDTH:(k + 1) * WIDTH], preferred_element_type=_F32)

    qa_ref[0] = proj(h, 0).T.astype(_BF16)
    ka = proj(h, 1)
    lane = lax.broadcasted_iota(jnp.int32, (KV_BLOCK, PAIR), 1)
    tiles = []
    for pair in range(N_HEADS // 2):
        both = ka[:, pair * PAIR:(pair + 1) * PAIR]
        tiles += [jnp.where(lane < HEAD_DIM, both, 0.0),
                  jnp.where(lane < HEAD_DIM, pltpu.roll(both, HEAD_DIM, 1), 0.0)]
    ka_wide = jnp.concatenate(tiles, axis=1)
    ka_ref[0, 0] = (ka_wide + kpos_ref[...]).astype(_BF16)
    kmean_ref[0, pl.ds(i, 1), :] = jnp.mean(ka_wide, axis=0, keepdims=True)
    kamax_ref[0, pl.ds(i, 1), :] = jnp.max(jnp.abs(ka_wide.astype(_BF16).astype(_F32)),
                                           axis=0, keepdims=True)
    va_t = proj(h, 2).T.astype(_BF16)
    extra = (lax.broadcasted_iota(jnp.int32, (V_ROWS - HEAD_DIM, KV_BLOCK), 0) == 0)
    extra = extra.astype(_F32).astype(_BF16)
    va_ref[0, 0] = jnp.concatenate(
        [piece for hd in range(N_HEADS)
         for piece in (va_t[hd * HEAD_DIM:(hd + 1) * HEAD_DIM, :], extra)], axis=0)
    qs_ref[0] = proj(h, 3).T.astype(_BF16)
    hp = jnp.dot(perm_ref[...], h, preferred_element_type=_F32).astype(_BF16)
    ks_ref[0, 0] = proj(hp, 4).astype(_BF16)
    vs_ref[0, 0] = proj(hp, 5).T.astype(_BF16)
    rest_ref[0] = jnp.dot(h, w_ref[0, :, 6 * WIDTH:], preferred_element_type=_F32).astype(_BF16)


def _in_proj(layer, x, g, perm, kpos, w_all):
    b, s, d = x.shape
    nb = s // KV_BLOCK
    t = KV_BLOCK
    kw = N_HEADS * PAIR
    out_shape = (
        jax.ShapeDtypeStruct((b, WIDTH, s), _BF16),
        jax.ShapeDtypeStruct((b, nb, t, kw), _BF16),
        jax.ShapeDtypeStruct((b, nb, N_HEADS * V_ROWS, t), _BF16),
        jax.ShapeDtypeStruct((b, nb, kw), _F32),
        jax.ShapeDtypeStruct((b, nb, kw), _F32),
        jax.ShapeDtypeStruct((b, WIDTH, s), _BF16),
        jax.ShapeDtypeStruct((b, nb, t, WIDTH), _BF16),
        jax.ShapeDtypeStruct((b, nb, WIDTH, t), _BF16),
        jax.ShapeDtypeStruct((b, s, REST_WIDTH), _BF16),
    )
    qt_spec = pl.BlockSpec((1, WIDTH, t), lambda bi, i: (bi, 0, i))
    k_spec = pl.BlockSpec((1, 1, t, WIDTH), lambda bi, i: (bi, i, 0, 0))
    vt_spec = pl.BlockSpec((1, 1, WIDTH, t), lambda bi, i: (bi, i, 0, 0))
    return pl.pallas_call(
        _in_proj_kernel,
        grid=(b, nb),
        in_specs=[
            pl.BlockSpec((1, t, d), lambda bi, i: (bi, i, 0)),
            _layer_spec(layer, (1, d)),
            _const_spec((t, t)),
            _const_spec((t, kw)),
            _layer_spec(layer, (d, 6 * WIDTH + REST_WIDTH)),
        ],
        out_specs=(
            qt_spec,
            pl.BlockSpec((1, 1, t, kw), lambda bi, i: (bi, i, 0, 0)),
            pl.BlockSpec((1, 1, N_HEADS * V_ROWS, t), lambda bi, i: (bi, i, 0, 0)),
            pl.BlockSpec((1, nb, kw), lambda bi, i: (bi, 0, 0)),
            pl.BlockSpec((1, nb, kw), lambda bi, i: (bi, 0, 0)),
            qt_spec, k_spec, vt_spec,
            pl.BlockSpec((1, t, REST_WIDTH), lambda bi, i: (bi, i, 0)),
        ),
        out_shape=out_shape,
        compiler_params=_params("arbitrary", "arbitrary"),
        name="in_proj",
    )(x, g, perm, kpos, w_all)


def _head_rows(q_pair, hh):
    row = lax.broadcasted_iota(jnp.int32, q_pair.shape, 0)
    keep = (row >= hh * HEAD_DIM) & (row < (hh + 1) * HEAD_DIM)
    return jnp.where(keep, q_pair, jnp.zeros_like(q_pair))


def _moba_kernel(q_ref, qaug_ref, k_ref, v_ref, kmean_ref, kamax_ref, slope_ref, causal_ref,
                 o_ref, sel_ref, sa_ref, sb_ref, acc_ref):
    last = pl.program_id(2)
    nb = k_ref.shape[1]
    nidx = lax.broadcasted_iota(jnp.int32, (nb, MOBA_Q), 0)
    blk = last * MOBA_G + (lax.broadcasted_iota(jnp.int32, (nb, MOBA_Q), 1)
                           >> (KV_BLOCK.bit_length() - 1))
    blk_row = blk[0:1, :]
    neg_inf = jnp.float32(-jnp.inf)
    slope = [slope_ref[hh, 0:1, :] for hh in range(2)]
    lanes = [slice(hh * PAIR, (hh + 1) * PAIR) for hh in range(2)]
    vrows = [slice(hh * V_ROWS, (hh + 1) * V_ROWS) for hh in range(2)]

    heads, reach = [], []
    for hh in range(2):
        qm = jnp.concatenate([q_ref[0, hh * HEAD_DIM:(hh + 1) * HEAD_DIM, :], qaug_ref[hh]], axis=0)
        km = kmean_ref[0, :, lanes[hh]].astype(_BF16)
        kamax = kamax_ref[0, :, lanes[hh]].astype(_BF16)
        bound = jnp.dot(kamax, jnp.abs(qm), preferred_element_type=_F32)
        reach.append(bound + slope[hh] * ((nidx + 1 - blk) * KV_BLOCK).astype(_F32))
        gate = jnp.dot(km, qm, preferred_element_type=_F32)
        g = jnp.where(nidx < blk, gate, neg_inf)
        sel = nidx == blk
        for r in range(MOBA_TOPK):
            mx = jnp.max(g, axis=0, keepdims=True)
            first = jnp.min(jnp.where(g == mx, nidx, nb), axis=0, keepdims=True)
            hit = nidx == first
            sel = sel | (hit & (jnp.full((nb, MOBA_Q), r, jnp.int32) < blk))
            g = jnp.where(hit, neg_inf, g)
        sel_ref[hh] = jnp.where(sel, 0.0, neg_inf)
        heads.append(qm)

    def shift(j):
        return ((j - blk_row) * KV_BLOCK).astype(_F32)

    def row_bias(hh, j):
        return sel_ref[hh, pl.ds(j, 1), :] + slope[hh] * shift(j)

    def scores(t, dst_ref, nearest=False):
        mx = []
        for hh in range(2):
            col = jnp.full((1, MOBA_Q), neg_inf, _F32)
            for g in range(MOBA_G):
                j = t * MOBA_G + g
                s = jnp.dot(k_ref[0, j, :, lanes[hh]], heads[hh], preferred_element_type=_F32)
                if nearest:
                    s = s + causal_ref[g]
                dst_ref[hh, g * KV_BLOCK:(g + 1) * KV_BLOCK, :] = s
                col = jnp.maximum(col, jnp.max(s, axis=0, keepdims=True) + row_bias(hh, j))
            mx.append(col)
        return mx

    def accumulate(t, src_ref, mx, st):
        out = []
        for hh in range(2):
            m_new = jnp.maximum(st[hh], mx[hh])
            acc = jnp.exp2(st[hh] - m_new) * acc_ref[hh]
            for g in range(MOBA_G):
                j = t * MOBA_G + g
                s = src_ref[hh, g * KV_BLOCK:(g + 1) * KV_BLOCK, :]
                p = jnp.exp2(s - (m_new - row_bias(hh, j)))
                acc = acc + jnp.dot(v_ref[0, j, vrows[hh], :], p.astype(_BF16),
                                    preferred_element_type=_F32)
            acc_ref[hh] = acc
            out.append(m_new)
        return out

    state = [jnp.full((1, MOBA_Q), jnp.finfo(_F32).min, _F32) for _ in range(2)]
    for hh in range(2):
        acc_ref[hh] = jnp.zeros((V_ROWS, MOBA_Q), _F32)

    mx_a = scores(last, sa_ref, nearest=True)
    mx_b = scores(jnp.maximum(last - 1, 0), sb_ref)
    first_needed = jnp.int32(nb)
    for hh in range(2):
        needed = (nidx < blk) & (reach[hh] - mx_a[hh] > MOBA_SKIP)
        first_needed = jnp.minimum(first_needed, jnp.min(jnp.where(needed, nidx, nb)))
    first = jnp.minimum(first_needed // MOBA_G, last)
    ntiles = last - first + 1
    state = accumulate(last, sa_ref, mx_a, state)

    def body(k, carry):
        st, mx_b = carry[:2], carry[2:]
        t = last - 1 - 2 * k
        mx_a = scores(t - 1, sa_ref)
        st = accumulate(t, sb_ref, mx_b, st)
        mx_b = scores(jnp.maximum(t - 2, 0), sb_ref)
        st = accumulate(t - 1, sa_ref, mx_a, st)
        return tuple(st) + tuple(mx_b)

    carry = lax.fori_loop(0, (ntiles - 1) // 2, body, tuple(state) + tuple(mx_b))

    @pl.when(ntiles % 2 == 0)
    def _():
        accumulate(first, sb_ref, carry[2:], carry[:2])

    o_ref[0] = jnp.concatenate(
        [acc_ref[hh, 0:HEAD_DIM, :] / acc_ref[hh, HEAD_DIM:HEAD_DIM + 1, :] for hh in range(2)],
        axis=0).astype(_BF16)


def _moba(qa_t, qaug, ka, va_t, kmean, kamax, slopes, causal):
    b, _, s = qa_t.shape
    nb = s // KV_BLOCK
    npair = N_HEADS // 2
    assert nb % MOBA_G == 0
    return pl.pallas_call(
        _moba_kernel,
        grid=(b, npair, s // MOBA_Q),
        in_specs=[
            pl.BlockSpec((1, PAIR, MOBA_Q), lambda bi, hp, c: (bi, hp, c)),
            pl.BlockSpec((2, HEAD_DIM, MOBA_Q), lambda bi, hp, c: (hp, 0, 0)),
            pl.BlockSpec((1, nb, KV_BLOCK, 2 * PAIR), lambda bi, hp, c: (bi, 0, 0, hp)),
            pl.BlockSpec((1, nb, 2 * V_ROWS, KV_BLOCK), lambda bi, hp, c: (bi, 0, hp, 0)),
            pl.BlockSpec((1, nb, 2 * PAIR), lambda bi, hp, c: (bi, 0, hp)),
            pl.BlockSpec((1, nb, 2 * PAIR), lambda bi, hp, c: (bi, 0, hp)),
            pl.BlockSpec((2, SUBLANES, MOBA_Q), lambda bi, hp, c: (hp, 0, 0)),
            _const_spec((MOBA_G, KV_BLOCK, MOBA_Q)),
        ],
        out_specs=pl.BlockSpec((1, PAIR, MOBA_Q), lambda bi, hp, c: (bi, hp, c)),
        out_shape=jax.ShapeDtypeStruct((b, WIDTH, s), _BF16),
        scratch_shapes=[pltpu.VMEM((2, nb, MOBA_Q), _F32),
                        pltpu.VMEM((2, MOBA_G * KV_BLOCK, MOBA_Q), _F32),
                        pltpu.VMEM((2, MOBA_G * KV_BLOCK, MOBA_Q), _F32),
                        pltpu.VMEM((2, V_ROWS, MOBA_Q), _F32)],
        compiler_params=_params("arbitrary", "arbitrary", "arbitrary"),
        name="moba",
    )(qa_t, qaug, ka, va_t, kmean, kamax, slopes, causal)


def _suffix_product_over_sublanes(x):
    sub = lax.broadcasted_iota(jnp.int32, x.shape, 0)
    y = x
    for d in (1, 2, 4):
        up = pltpu.roll(y, SUBLANES - d, 0)
        y = y * jnp.where(sub < SUBLANES - d, up, 1.0)
    return y


def _sb_scores(k_j, qm, mask):
    z = jnp.dot(k_j, qm, preferred_element_type=_F32)
    keep = 1.0 / (1.0 + jnp.exp2(z))
    if mask is not None:
        keep = jnp.where(mask, keep, 1.0)
    run = jnp.ones((SUBLANES, z.shape[1]), _F32)
    diff = [None] * RUN
    for r in reversed(range(RUN)):
        nxt = run * keep[r * SUBLANES:(r + 1) * SUBLANES, :]
        diff[r] = run - nxt
        run = nxt
    sub = lax.broadcasted_iota(jnp.int32, run.shape, 0)
    shifted = jnp.where(sub < SUBLANES - 1, pltpu.roll(run, SUBLANES - 1, 0), 1.0)
    return diff, run, _suffix_product_over_sublanes(shifted)


def _sb_weights(scores, v_jh, carry, acc):
    diff, run, later = scores
    base = later * carry
    w = jnp.concatenate([d * base for d in diff], axis=0)
    acc = acc + jnp.dot(v_jh, w.astype(_BF16), preferred_element_type=_F32)
    return carry * (later[0:1, :] * run[0:1, :]), acc


def _sb_kernel(q_ref, k_ref, v_ref, o_ref):
    rho = lax.broadcasted_iota(jnp.int32, (KV_BLOCK, SB_Q), 0)
    kpos = (rho & (SUBLANES - 1)) * RUN + (rho >> 3)
    qpos = lax.broadcasted_iota(jnp.int32, (KV_BLOCK, SB_Q), 1)
    strict = kpos < qpos
    rows = [slice(hh * HEAD_DIM, (hh + 1) * HEAD_DIM) for hh in range(2)]

    jds = [pl.program_id(2) * SB_TILES + u for u in range(SB_TILES)]
    jps = [jnp.maximum(jd - 1, 0) for jd in jds]
    heads = [[_head_rows(q_ref[0, :, u * SB_Q:(u + 1) * SB_Q], hh) for hh in range(2)]
             for u in range(SB_TILES)]
    diags = [[_sb_scores(k_ref[0, jds[u]], heads[u][hh], strict) for hh in range(2)]
             for u in range(SB_TILES)]
    prevs = [[_sb_scores(k_ref[0, jps[u]], heads[u][hh], None) for hh in range(2)]
             for u in range(SB_TILES)]
    states = []
    for u in range(SB_TILES):
        state = []
        for hh in range(2):
            carry, acc = _sb_weights(diags[u][hh], v_ref[0, jds[u], rows[hh], :],
                                     jnp.ones((1, SB_Q), _F32),
                                     jnp.zeros((HEAD_DIM, SB_Q), _F32))
            carry = jnp.where(jds[u] > 0, carry, 0.0)
            state += list(_sb_weights(prevs[u][hh], v_ref[0, jps[u], rows[hh], :], carry, acc))
        states.append(state)

    def more(trip, sts):
        flags = [jnp.logical_and(jds[u] - 2 - trip >= 0,
                                 jnp.maximum(jnp.max(sts[u][0]), jnp.max(sts[u][2])) > SB_STOP)
                 for u in range(SB_TILES)]
        return functools.reduce(jnp.logical_or, flags).astype(jnp.int32)

    def body(loop):
        trip, flat = loop[0], loop[2:]
        new = []
        for u in range(SB_TILES):
            j = jds[u] - 2 - trip
            jc = jnp.maximum(j, 0)
            out = []
            for hh in range(2):
                carry, acc = flat[4 * u + 2 * hh:4 * u + 2 * hh + 2]
                carry = jnp.where(j >= 0, carry, 0.0)
                out += list(_sb_weights(_sb_scores(k_ref[0, jc], heads[u][hh], None),
                                        v_ref[0, jc, rows[hh], :], carry, acc))
            new.append(out)
        return (trip + 1, more(trip + 1, new), *[a for st in new for a in st])

    zero = jnp.int32(0)
    final = lax.while_loop(lambda loop: loop[1] > 0, body,
                           (zero, more(zero, states), *[a for st in states for a in st]))[2:]
    for u in range(SB_TILES):
        o_ref[0, :, u * SB_Q:(u + 1) * SB_Q] = jnp.concatenate(
            [final[4 * u + 1], final[4 * u + 3]], axis=0).astype(_BF16)


def _stick_breaking(qs_t, ks, vs_t):
    b, _, s = qs_t.shape
    nb = s // KV_BLOCK
    npair = N_HEADS // 2
    return pl.pallas_call(
        _sb_kernel,
        grid=(b, npair, s // (SB_Q * SB_TILES)),
        in_specs=[
            pl.BlockSpec((1, PAIR, SB_Q * SB_TILES), lambda bi, hp, i: (bi, hp, i)),
            pl.BlockSpec((1, nb, KV_BLOCK, PAIR), lambda bi, hp, i: (bi, 0, 0, hp)),
            pl.BlockSpec((1, nb, PAIR, KV_BLOCK), lambda bi, hp, i: (bi, 0, hp, 0)),
        ],
        out_specs=pl.BlockSpec((1, PAIR, SB_Q * SB_TILES), lambda bi, hp, i: (bi, hp, i)),
        out_shape=jax.ShapeDtypeStruct((b, WIDTH, s), _BF16),
        compiler_params=_params("arbitrary", "arbitrary", "arbitrary"),
        name="stick_breaking",
    )(qs_t, ks, vs_t)


MIX_T = 512
MIX_CHUNK = 256
HALO = 16


def _mix_kernel(oa_ref, ob_ref, rest_ref, halo_ref, x_ref, convw_ref, bg_ref,
                wpa_ref, wpb_ref, wpc_ref, wout_ref, o_ref, merged_ref):
    i = pl.program_id(1)
    r = rest_ref[0]
    xc = r[:, 0:WIDTH].astype(_F32)
    bc = r[:, WIDTH:2 * WIDTH].astype(_F32)
    cc = r[:, 2 * WIDTH:3 * WIDTH].astype(_F32)
    u = cc * xc
    hl = halo_ref[0]
    uh = hl[:, 2 * WIDTH:3 * WIDTH].astype(_F32) * hl[:, 0:WIDTH].astype(_F32)
    uh = jnp.where(i > 0, uh, 0.0)
    prev1 = uh[HALO - 1:HALO, :]
    prev2 = uh[HALO - 2:HALO - 1, :]
    row = lax.broadcasted_iota(jnp.int32, u.shape, 0)
    u1 = jnp.where(row == 0, prev1, pltpu.roll(u, 1, 0))
    u2 = jnp.where(row == 0, prev2, jnp.where(row == 1, prev1, pltpu.roll(u, 2, 0)))
    w = convw_ref[0]
    y = (bc * (u2 * w[0:1, :] + u1 * w[1:2, :] + u * w[2:3, :])).astype(_BF16)
    oa, ob = oa_ref[0], ob_ref[0]
    d = D_MODEL
    for c in range(d // MIX_CHUNK):
        cols = slice(c * MIX_CHUNK, (c + 1) * MIX_CHUNK)

        def gate(k):
            lo = k * d + c * MIX_CHUNK
            return 1.0 / (1.0 + jnp.exp2(
                r[:, 3 * WIDTH + lo:3 * WIDTH + lo + MIX_CHUNK].astype(_F32)
                + bg_ref[0, :, lo:lo + MIX_CHUNK]))

        merged_ref[:, cols] = (
            gate(0) * lax.dot_general(oa, wpa_ref[0, :, cols], _TN, preferred_element_type=_F32)
            + gate(1) * lax.dot_general(ob, wpb_ref[0, :, cols], _TN,
                                        preferred_element_type=_F32)
            + gate(2) * jnp.dot(y, wpc_ref[0, :, cols], preferred_element_type=_F32)
        ).astype(_BF16)
    o_ref[0] = x_ref[0] + jnp.dot(merged_ref[...], wout_ref[0], preferred_element_type=_F32)


def _mix(layer, oa, ob, rest, x, convw, bg, wpa, wpb, wpc, wout):
    b, s, d = x.shape
    t = MIX_T
    tile = lambda width: pl.BlockSpec((1, t, width), lambda bi, i: (bi, i, 0))
    tile_t = pl.BlockSpec((1, WIDTH, t), lambda bi, i: (bi, 0, i))
    halo = pl.BlockSpec((1, HALO, REST_WIDTH),
                        lambda bi, i: (bi, jnp.maximum(i * (t // HALO) - 1, 0), 0))
    return pl.pallas_call(
        _mix_kernel,
        grid=(b, s // t),
        in_specs=[tile_t, tile_t, tile(REST_WIDTH), halo, tile(d),
                  _layer_spec(layer, (CONV_K, WIDTH)), _layer_spec(layer, (1, 3 * d)),
                  _layer_spec(layer, (WIDTH, d)), _layer_spec(layer, (WIDTH, d)),
                  _layer_spec(layer, (WIDTH, d)), _layer_spec(layer, (d, d))],
        out_specs=tile(d),
        out_shape=jax.ShapeDtypeStruct((b, s, d), _F32),
        scratch_shapes=[pltpu.VMEM((t, d), _BF16)],
        compiler_params=_params("arbitrary", "arbitrary"),
        name="mix",
    )(oa, ob, rest, rest, x, convw, bg, wpa, wpb, wpc, wout)


FFN_T = 1024
FF_CHUNK = 256


def _ffn_up_kernel(x_ref, g_ref, wg_ref, wu_ref, o_ref):
    h = _rms(x_ref[...], g_ref[0]).astype(_BF16)
    for n in range(D_FF // FF_CHUNK):
        cols = slice(n * FF_CHUNK, (n + 1) * FF_CHUNK)
        a = jnp.dot(h, wg_ref[0, :, cols], preferred_element_type=_F32)
        bgate = jnp.dot(h, wu_ref[0, :, cols], preferred_element_type=_F32)
        o_ref[:, cols] = (a * jax.nn.sigmoid(a) * bgate).astype(_BF16)


def _ffn_down_kernel(a_ref, x_ref, wd_ref, g_ref, o_ref, *, final_norm):
    y = x_ref[...] + jnp.dot(a_ref[...], wd_ref[0], preferred_element_type=_F32)
    if final_norm:
        y = _rms(y, g_ref[...])
    o_ref[...] = y


def _ffn(layer, x2d, g, wg, wu, wd, gfinal, final_norm):
    n, d = x2d.shape
    t = FFN_T
    act = pl.pallas_call(
        _ffn_up_kernel,
        grid=(n // t,),
        in_specs=[pl.BlockSpec((t, d), lambda i: (i, 0)), _layer_spec(layer, (1, d)),
                  _layer_spec(layer, (d, D_FF)), _layer_spec(layer, (d, D_FF))],
        out_specs=pl.BlockSpec((t, D_FF), lambda i: (i, 0)),
        out_shape=jax.ShapeDtypeStruct((n, D_FF), _BF16),
        compiler_params=_params("arbitrary"),
        name="ffn_up",
    )(x2d, g, wg, wu)
    return pl.pallas_call(
        functools.partial(_ffn_down_kernel, final_norm=final_norm),
        grid=(n // t,),
        in_specs=[pl.BlockSpec((t, D_FF), lambda i: (i, 0)),
                  pl.BlockSpec((t, d), lambda i: (i, 0)),
                  _layer_spec(layer, (D_FF, d)), _const_spec((1, d))],
        out_specs=pl.BlockSpec((t, d), lambda i: (i, 0)),
        out_shape=jax.ShapeDtypeStruct((n, d), _F32),
        compiler_params=_params("arbitrary"),
        name="ffn_down",
    )(act, x2d, wd, gfinal)


def _sb_permutation():
    rho = np.arange(KV_BLOCK)
    kappa = (rho % SUBLANES) * RUN + rho // SUBLANES
    p = np.zeros((KV_BLOCK, KV_BLOCK), np.float32)
    p[rho, kappa] = 1.0
    return jnp.asarray(p, _BF16)


def _moba_tables():
    slopes = np.exp2(-8.0 * (np.arange(N_HEADS, dtype=np.float64) + 1.0) / N_HEADS) * LOG2E
    kpos = np.zeros((KV_BLOCK, N_HEADS * PAIR), np.float32)
    qaug = np.zeros((N_HEADS, HEAD_DIM, MOBA_Q), np.float32)
    for hd in range(N_HEADS):
        rest = np.float64(slopes[hd])
        for a in range(N_AUG):
            piece = np.float64(np.float32(rest).astype(_BF16))
            kpos[:, hd * PAIR + HEAD_DIM + a] = np.arange(KV_BLOCK)
            qaug[hd, a, :] = piece
            rest = rest - piece
    key = np.arange(KV_BLOCK)[:, None]
    q_blk, q_off = np.divmod(np.arange(MOBA_Q)[None, :], KV_BLOCK)
    causal = np.stack([np.where((q_blk > g) | ((q_blk == g) & (key <= q_off)), 0.0, -np.inf)
                       for g in range(MOBA_G)])
    slope_tab = slopes[:, None, None] * np.ones((1, SUBLANES, MOBA_Q))
    return (jnp.asarray(kpos), jnp.asarray(qaug, _BF16),
            jnp.asarray(slope_tab.astype(np.float32)), jnp.asarray(causal.astype(np.float32)))


def kernel(x, norm_mix_g, w_in, b_gate, conv_w, w_proj_moba, w_proj_sb, w_proj_conv, w_out,
           norm_ffn_g, w_ffn_gate, w_ffn_up, w_ffn_down, norm_final_g):
    depth = w_in.shape[0]
    b, s, d = x.shape
    scale = HEAD_DIM ** -0.5
    perm = _sb_permutation()
    kpos, qaug, slopes, causal = _moba_tables()
    col_scale = np.ones((w_in.shape[2],), np.float32)
    col_scale[0:WIDTH] = scale * LOG2E
    col_scale[3 * WIDTH:4 * WIDTH] = scale * LOG2E
    col_scale[9 * WIDTH:] = -LOG2E
    w_all = (w_in * col_scale).astype(_BF16)
    bf16 = lambda p: p.astype(_BF16)
    wpa, wpb, wpc, wout = bf16(w_proj_moba), bf16(w_proj_sb), bf16(w_proj_conv), bf16(w_out)
    wg, wu, wd = bf16(w_ffn_gate), bf16(w_ffn_up), bf16(w_ffn_down)
    g_mix, g_ffn, bg = norm_mix_g[:, None, :], norm_ffn_g[:, None, :], b_gate[:, None, :] * -LOG2E
    for l in range(depth):
        qa_t, ka, va_t, kmean, kamax, qs_t, ks, vs_t, rest = _in_proj(
            l, x, g_mix, perm, kpos, w_all)
        oa = _moba(qa_t, qaug, ka, va_t, kmean, kamax, slopes, causal)
        ob = _stick_breaking(qs_t, ks, vs_t)
        x = _mix(l, oa, ob, rest, x, conv_w, bg, wpa, wpb, wpc, wout)
        x = _ffn(l, x.reshape(b * s, d), g_ffn, wg, wu, wd, norm_final_g[None, :],
                 final_norm=(l == depth - 1)).reshape(b, s, d)
    return x
```

```python
import functools

import jax
import jax.numpy as jnp
import numpy as np
from jax import lax
from jax.experimental import pallas as pl
from jax.experimental.pallas import tpu as pltpu

D_MODEL = 1024
HEAD_DIM = 64
N_HEADS = 8
WIDTH = N_HEADS * HEAD_DIM
CONV_K = 3
KV_BLOCK = 256
MOBA_TOPK = 3
MOBA_G = 2
MOBA_Q = MOBA_G * KV_BLOCK
SB_Q = KV_BLOCK
SB_TILES = 4
MOBA_SKIP = -150.0
D_FF = 2816
RMS_EPS = 1e-6
REST_WIDTH = 3 * WIDTH + 3 * D_MODEL
PAIR = 2 * HEAD_DIM
V_ROWS = HEAD_DIM + 16
N_AUG = 4
SUBLANES = 8
LOG2E = 1.4426950408889634
RUN = KV_BLOCK // SUBLANES
SB_STOP = 0.0
VMEM_LIMIT = 56 * 1024 * 1024

_TN = (((0,), (0,)), ((), ()))
_F32 = jnp.float32
_BF16 = jnp.bfloat16


def _params(*sem):
    return pltpu.CompilerParams(dimension_semantics=sem, vmem_limit_bytes=VMEM_LIMIT)


def _const_spec(shape):
    zeros = (0,) * len(shape)
    return pl.BlockSpec(shape, lambda *_: zeros)


def _layer_spec(layer, shape):
    index = (layer,) + (0,) * len(shape)
    return pl.BlockSpec((1,) + tuple(shape), lambda *_: index)


def _rms(x, g):
    y = x * lax.rsqrt(jnp.mean(x * x, axis=-1, keepdims=True) + RMS_EPS)
    return y * g


def _in_proj_kernel(x_ref, g_ref, perm_ref, kpos_ref, w_ref,
                    qa_ref, ka_ref, va_ref, kmean_ref, kamax_ref, qs_ref, ks_ref, vs_ref,
                    rest_ref):
    i = pl.program_id(1)
    h = _rms(x_ref[0], g_ref[0]).astype(_BF16)

    def proj(lhs, k):
        return jnp.dot(lhs, w_ref[0, :, k * WIDTH:(k + 1) * WIDTH], preferred_element_type=_F32)

    qa_ref[0] = proj(h, 0).T.astype(_BF16)
    ka = proj(h, 1)
    lane = lax.broadcasted_iota(jnp.int32, (KV_BLOCK, PAIR), 1)
    tiles = []
    for pair in range(N_HEADS // 2):
        both = ka[:, pair * PAIR:(pair + 1) * PAIR]
        tiles += [jnp.where(lane < HEAD_DIM, both, 0.0),
                  jnp.where(lane < HEAD_DIM, pltpu.roll(both, HEAD_DIM, 1), 0.0)]
    ka_wide = jnp.concatenate(tiles, axis=1)
    ka_ref[0, 0] = (ka_wide + kpos_ref[...]).astype(_BF16)
    kmean_ref[0, pl.ds(i, 1), :] = jnp.mean(ka_wide, axis=0, keepdims=True)
    kamax_ref[0, pl.ds(i, 1), :] = jnp.max(jnp.abs(ka_wide.astype(_BF16).astype(_F32)),
                                           axis=0, keepdims=True)
    va_t = proj(h, 2).T.astype(_BF16)
    extra = (lax.broadcasted_iota(jnp.int32, (V_ROWS - HEAD_DIM, KV_BLOCK), 0) == 0)
    extra = extra.astype(_F32).astype(_BF16)
    va_ref[0, 0] = jnp.concatenate(
        [piece for hd in range(N_HEADS)
         for piece in (va_t[hd * HEAD_DIM:(hd + 1) * HEAD_DIM, :], extra)], axis=0)
    qs_ref[0] = proj(h, 3).T.astype(_BF16)
    hp = jnp.dot(perm_ref[...], h, preferred_element_type=_F32).astype(_BF16)
    ks_ref[0, 0] = proj(hp, 4).astype(_BF16)
    vs_ref[0, 0] = proj(hp, 5).T.astype(_BF16)
    rest_ref[0] = jnp.dot(h, w_ref[0, :, 6 * WIDTH:], preferred_element_type=_F32).astype(_BF16)


def _in_proj(layer, x, g, perm, kpos, w_all):
    b, s, d = x.shape
    nb = s // KV_BLOCK
    t = KV_BLOCK
    kw = N_HEADS * PAIR
    out_shape = (
        jax.ShapeDtypeStruct((b, WIDTH, s), _BF16),
        jax.ShapeDtypeStruct((b, nb, t, kw), _BF16),
        jax.ShapeDtypeStruct((b, nb, N_HEADS * V_ROWS, t), _BF16),
        jax.ShapeDtypeStruct((b, nb, kw), _F32),
        jax.ShapeDtypeStruct((b, nb, kw), _F32),
        jax.ShapeDtypeStruct((b, WIDTH, s), _BF16),
        jax.ShapeDtypeStruct((b, nb, t, WIDTH), _BF16),
        jax.ShapeDtypeStruct((b, nb, WIDTH, t), _BF16),
        jax.ShapeDtypeStruct((b, s, REST_WIDTH), _BF16),
    )
    qt_spec = pl.BlockSpec((1, WIDTH, t), lambda bi, i: (bi, 0, i))
    k_spec = pl.BlockSpec((1, 1, t, WIDTH), lambda bi, i: (bi, i, 0, 0))
    vt_spec = pl.BlockSpec((1, 1, WIDTH, t), lambda bi, i: (bi, i, 0, 0))
    return pl.pallas_call(
        _in_proj_kernel,
        grid=(b, nb),
        in_specs=[
            pl.BlockSpec((1, t, d), lambda bi, i: (bi, i, 0)),
            _layer_spec(layer, (1, d)),
            _const_spec((t, t)),
            _const_spec((t, kw)),
            _layer_spec(layer, (d, 6 * WIDTH + REST_WIDTH)),
        ],
        out_specs=(
            qt_spec,
            pl.BlockSpec((1, 1, t, kw), lambda bi, i: (bi, i, 0, 0)),
            pl.BlockSpec((1, 1, N_HEADS * V_ROWS, t), lambda bi, i: (bi, i, 0, 0)),
            pl.BlockSpec((1, nb, kw), lambda bi, i: (bi, 0, 0)),
            pl.BlockSpec((1, nb, kw), lambda bi, i: (bi, 0, 0)),
            qt_spec, k_spec, vt_spec,
            pl.BlockSpec((1, t, REST_WIDTH), lambda bi, i: (bi, i, 0)),
        ),
        out_shape=out_shape,
        compiler_params=_params("arbitrary", "arbitrary"),
        name="in_proj",
    )(x, g, perm, kpos, w_all)


def _head_rows(q_pair, hh):
    row = lax.broadcasted_iota(jnp.int32, q_pair.shape, 0)
    keep = (row >= hh * HEAD_DIM) & (row < (hh + 1) * HEAD_DIM)
    return jnp.where(keep, q_pair, jnp.zeros_like(q_pair))


def _moba_kernel(q_ref, qaug_ref, k_ref, v_ref, kmean_ref, kamax_ref, slope_ref, causal_ref,
                 o_ref, sel_ref, sa_ref, sb_ref, acc_ref):
    last = pl.program_id(2)
    nb = k_ref.shape[1]
    nidx = lax.broadcasted_iota(jnp.int32, (nb, MOBA_Q), 0)
    blk = last * MOBA_G + (lax.broadcasted_iota(jnp.int32, (nb, MOBA_Q), 1)
                           >> (KV_BLOCK.bit_length() - 1))
    blk_row = blk[0:1, :]
    neg_inf = jnp.float32(-jnp.inf)
    slope = [slope_ref[hh, 0:1, :] for hh in range(2)]
    lanes = [slice(hh * PAIR, (hh + 1) * PAIR) for hh in range(2)]
    vrows = [slice(hh * V_ROWS, (hh + 1) * V_ROWS) for hh in range(2)]

    heads, reach = [], []
    for hh in range(2):
        qm = jnp.concatenate([q_ref[0, hh * HEAD_DIM:(hh + 1) * HEAD_DIM, :], qaug_ref[hh]], axis=0)
        km = kmean_ref[0, :, lanes[hh]].astype(_BF16)
        kamax = kamax_ref[0, :, lanes[hh]].astype(_BF16)
        bound = jnp.dot(kamax, jnp.abs(qm), preferred_element_type=_F32)
        reach.append(bound + slope[hh] * ((nidx + 1 - blk) * KV_BLOCK).astype(_F32))
        gate = jnp.dot(km, qm, preferred_element_type=_F32)
        g = jnp.where(nidx < blk, gate, neg_inf)
        sel = nidx == blk
        for r in range(MOBA_TOPK):
            mx = jnp.max(g, axis=0, keepdims=True)
            first = jnp.min(jnp.where(g == mx, nidx, nb), axis=0, keepdims=True)
            hit = nidx == first
            sel = sel | (hit & (jnp.full((nb, MOBA_Q), r, jnp.int32) < blk))
            g = jnp.where(hit, neg_inf, g)
        sel_ref[hh] = jnp.where(sel, 0.0, neg_inf)
        heads.append(qm)

    def shift(j):
        return ((j - blk_row) * KV_BLOCK).astype(_F32)

    def row_bias(hh, j):
        return sel_ref[hh, pl.ds(j, 1), :] + slope[hh] * shift(j)

    def scores(t, dst_ref, nearest=False):
        mx = []
        for hh in range(2):
            col = jnp.full((1, MOBA_Q), neg_inf, _F32)
            for g in range(MOBA_G):
                j = t * MOBA_G + g
                s = jnp.dot(k_ref[0, j, :, lanes[hh]], heads[hh], preferred_element_type=_F32)
                if nearest:
                    s = s + causal_ref[g]
                dst_ref[hh, g * KV_BLOCK:(g + 1) * KV_BLOCK, :] = s
                col = jnp.maximum(col, jnp.max(s, axis=0, keepdims=True) + row_bias(hh, j))
            mx.append(col)
        return mx

    def accumulate(t, src_ref, mx, st):
        out = []
        for hh in range(2):
            m_new = jnp.maximum(st[hh], mx[hh])
            acc = jnp.exp2(st[hh] - m_new) * acc_ref[hh]
            for g in range(MOBA_G):
                j = t * MOBA_G + g
                s = src_ref[hh, g * KV_BLOCK:(g + 1) * KV_BLOCK, :]
                p = jnp.exp2(s - (m_new - row_bias(hh, j)))
                acc = acc + jnp.dot(v_ref[0, j, vrows[hh], :], p.astype(_BF16),
                                    preferred_element_type=_F32)
            acc_ref[hh] = acc
            out.append(m_new)
        return out

    state = [jnp.full((1, MOBA_Q), jnp.finfo(_F32).min, _F32) for _ in range(2)]
    for hh in range(2):
        acc_ref[hh] = jnp.zeros((V_ROWS, MOBA_Q), _F32)

    mx_a = scores(last, sa_ref, nearest=True)
    mx_b = scores(jnp.maximum(last - 1, 0), sb_ref)
    first_needed = jnp.int32(nb)
    for hh in range(2):
        needed = (nidx < blk) & (reach[hh] - mx_a[hh] > MOBA_SKIP)
        first_needed = jnp.minimum(first_needed, jnp.min(jnp.where(needed, nidx, nb)))
    first = jnp.minimum(first_needed // MOBA_G, last)
    ntiles = last - first + 1
    state = accumulate(last, sa_ref, mx_a, state)

    def body(k, carry):
        st, mx_b = carry[:2], carry[2:]
        t = last - 1 - 2 * k
        mx_a = scores(t - 1, sa_ref)
        st = accumulate(t, sb_ref, mx_b, st)
        mx_b = scores(jnp.maximum(t - 2, 0), sb_ref)
        st = accumulate(t - 1, sa_ref, mx_a, st)
        return tuple(st) + tuple(mx_b)

    carry = lax.fori_loop(0, (ntiles - 1) // 2, body, tuple(state) + tuple(mx_b))

    @pl.when(ntiles % 2 == 0)
    def _():
        accumulate(first, sb_ref, carry[2:], carry[:2])

    o_ref[0] = jnp.concatenate(
        [acc_ref[hh, 0:HEAD_DIM, :] / acc_ref[hh, HEAD_DIM:HEAD_DIM + 1, :] for hh in range(2)],
        axis=0).astype(_BF16)


def _moba(qa_t, qaug, ka, va_t, kmean, kamax, slopes, causal):
    b, _, s = qa_t.shape
    nb = s // KV_BLOCK
    npair = N_HEADS // 2
    assert nb % MOBA_G == 0
    return pl.pallas_call(
        _moba_kernel,
        grid=(b, npair, s // MOBA_Q),
        in_specs=[
            pl.BlockSpec((1, PAIR, MOBA_Q), lambda bi, hp, c: (bi, hp, c)),
            pl.BlockSpec((2, HEAD_DIM, MOBA_Q), lambda bi, hp, c: (hp, 0, 0)),
            pl.BlockSpec((1, nb, KV_BLOCK, 2 * PAIR), lambda bi, hp, c: (bi, 0, 0, hp)),
            pl.BlockSpec((1, nb, 2 * V_ROWS, KV_BLOCK), lambda bi, hp, c: (bi, 0, hp, 0)),
            pl.BlockSpec((1, nb, 2 * PAIR), lambda bi, hp, c: (bi, 0, hp)),
            pl.BlockSpec((1, nb, 2 * PAIR), lambda bi, hp, c: (bi, 0, hp)),
            pl.BlockSpec((2, SUBLANES, MOBA_Q), lambda bi, hp, c: (hp, 0, 0)),
            _const_spec((MOBA_G, KV_BLOCK, MOBA_Q)),
        ],
        out_specs=pl.BlockSpec((1, PAIR, MOBA_Q), lambda bi, hp, c: (bi, hp, c)),
        out_shape=jax.ShapeDtypeStruct((b, WIDTH, s), _BF16),
        scratch_shapes=[pltpu.VMEM((2, nb, MOBA_Q), _F32),
                        pltpu.VMEM((2, MOBA_G * KV_BLOCK, MOBA_Q), _F32),
                        pltpu.VMEM((2, MOBA_G * KV_BLOCK, MOBA_Q), _F32),
                        pltpu.VMEM((2, V_ROWS, MOBA_Q), _F32)],
        compiler_params=_params("arbitrary", "arbitrary", "arbitrary"),
        name="moba",
    )(qa_t, qaug, ka, va_t, kmean, kamax, slopes, causal)


def _suffix_product_over_sublanes(x):
    sub = lax.broadcasted_iota(jnp.int32, x.shape, 0)
    y = x
    for d in (1, 2, 4):
        up = pltpu.roll(y, SUBLANES - d, 0)
        y = y * jnp.where(sub < SUBLANES - d, up, 1.0)
    return y


def _sb_scores(k_j, qm, mask):
    z = jnp.dot(k_j, qm, preferred_element_type=_F32)
    keep = 1.0 / (1.0 + jnp.exp2(z))
    if mask is not None:
        keep = jnp.where(mask, keep, 1.0)
    run = jnp.ones((SUBLANES, z.shape[1]), _F32)
    diff = [None] * RUN
    for r in reversed(range(RUN)):
        nxt = run * keep[r * SUBLANES:(r + 1) * SUBLANES, :]
        diff[r] = run - nxt
        run = nxt
    sub = lax.broadcasted_iota(jnp.int32, run.shape, 0)
    shifted = jnp.where(sub < SUBLANES - 1, pltpu.roll(run, SUBLANES - 1, 0), 1.0)
    return diff, run, _suffix_product_over_sublanes(shifted)


def _sb_weights(scores, v_jh, carry, acc):
    diff, run, later = scores
    base = later * carry
    w = jnp.concatenate([d * base for d in diff], axis=0)
    acc = acc + jnp.dot(v_jh, w.astype(_BF16), preferred_element_type=_F32)
    return carry * (later[0:1, :] * run[0:1, :]), acc


def _sb_kernel(q_ref, k_ref, v_ref, o_ref):
    rho = lax.broadcasted_iota(jnp.int32, (KV_BLOCK, SB_Q), 0)
    kpos = (rho & (SUBLANES - 1)) * RUN + (rho >> (SUBLANES.bit_length() - 1))
    qpos = lax.broadcasted_iota(jnp.int32, (KV_BLOCK, SB_Q), 1)
    strict = kpos < qpos
    rows = [slice(hh * HEAD_DIM, (hh + 1) * HEAD_DIM) for hh in range(2)]

    jds = [pl.program_id(2) * SB_TILES + u for u in range(SB_TILES)]
    jps = [jnp.maximum(jd - 1, 0) for jd in jds]
    heads = [[_head_rows(q_ref[0, :, u * SB_Q:(u + 1) * SB_Q], hh) for hh in range(2)]
             for u in range(SB_TILES)]
    diags = [[_sb_scores(k_ref[0, jds[u]], heads[u][hh], strict) for hh in range(2)]
             for u in range(SB_TILES)]
    prevs = [[_sb_scores(k_ref[0, jps[u]], heads[u][hh], None) for hh in range(2)]
             for u in range(SB_TILES)]
    states = []
    for u in range(SB_TILES):
        state = []
        for hh in range(2):
            carry, acc = _sb_weights(diags[u][hh], v_ref[0, jds[u], rows[hh], :],
                                     jnp.ones((1, SB_Q), _F32),
                                     jnp.zeros((HEAD_DIM, SB_Q), _F32))
            carry = jnp.where(jds[u] > 0, carry, 0.0)
            state += list(_sb_weights(prevs[u][hh], v_ref[0, jps[u], rows[hh], :], carry, acc))
        states.append(state)

    def more(trip, sts):
        flags = [jnp.logical_and(jds[u] - 2 - trip >= 0,
                                 jnp.maximum(jnp.max(sts[u][0]), jnp.max(sts[u][2])) > SB_STOP)
                 for u in range(SB_TILES)]
        return functools.reduce(jnp.logical_or, flags).astype(jnp.int32)

    def body(loop):
        trip, flat = loop[0], loop[2:]
        new = []
        for u in range(SB_TILES):
            j = jds[u] - 2 - trip
            jc = jnp.maximum(j, 0)
            out = []
            for hh in range(2):
                carry, acc = flat[4 * u + 2 * hh:4 * u + 2 * hh + 2]
                carry = jnp.where(j >= 0, carry, 0.0)
                out += list(_sb_weights(_sb_scores(k_ref[0, jc], heads[u][hh], None),
                                        v_ref[0, jc, rows[hh], :], carry, acc))
            new.append(out)
        return (trip + 1, more(trip + 1, new), *[a for st in new for a in st])

    zero = jnp.int32(0)
    final = lax.while_loop(lambda loop: loop[1] > 0, body,
                           (zero, more(zero, states), *[a for st in states for a in st]))[2:]
    for u in range(SB_TILES):
        o_ref[0, :, u * SB_Q:(u + 1) * SB_Q] = jnp.concatenate(
            [final[4 * u + 1], final[4 * u + 3]], axis=0).astype(_BF16)


def _stick_breaking(qs_t, ks, vs_t):
    b, _, s = qs_t.shape
    nb = s // KV_BLOCK
    npair = N_HEADS // 2
    return pl.pallas_call(
        _sb_kernel,
        grid=(b, npair, s // (SB_Q * SB_TILES)),
        in_specs=[
            pl.BlockSpec((1, PAIR, SB_Q * SB_TILES), lambda bi, hp, i: (bi, hp, i)),
            pl.BlockSpec((1, nb, KV_BLOCK, PAIR), lambda bi, hp, i: (bi, 0, 0, hp)),
            pl.BlockSpec((1, nb, PAIR, KV_BLOCK), lambda bi, hp, i: (bi, 0, hp, 0)),
        ],
        out_specs=pl.BlockSpec((1, PAIR, SB_Q * SB_TILES), lambda bi, hp, i: (bi, hp, i)),
        out_shape=jax.ShapeDtypeStruct((b, WIDTH, s), _BF16),
        compiler_params=_params("arbitrary", "arbitrary", "arbitrary"),
        name="stick_breaking",
    )(qs_t, ks, vs_t)


MIX_T = 512
MIX_CHUNK = 256
HALO = 16


def _mix_kernel(oa_ref, ob_ref, rest_ref, halo_ref, x_ref, convw_ref, bg_ref,
                wpa_ref, wpb_ref, wpc_ref, wout_ref, o_ref, merged_ref):
    i = pl.program_id(1)
    r = rest_ref[0]
    xc = r[:, 0:WIDTH].astype(_F32)
    bc = r[:, WIDTH:2 * WIDTH].astype(_F32)
    cc = r[:, 2 * WIDTH:3 * WIDTH].astype(_F32)
    u = cc * xc
    hl = halo_ref[0]
    uh = hl[:, 2 * WIDTH:3 * WIDTH].astype(_F32) * hl[:, 0:WIDTH].astype(_F32)
    uh = jnp.where(i > 0, uh, 0.0)
    prev1 = uh[HALO - 1:HALO, :]
    prev2 = uh[HALO - 2:HALO - 1, :]
    row = lax.broadcasted_iota(jnp.int32, u.shape, 0)
    u1 = jnp.where(row == 0, prev1, pltpu.roll(u, 1, 0))
    u2 = jnp.where(row == 0, prev2, jnp.where(row == 1, prev1, pltpu.roll(u, 2, 0)))
    w = convw_ref[0]
    y = (bc * (u2 * w[0:1, :] + u1 * w[1:2, :] + u * w[2:3, :])).astype(_BF16)
    oa, ob = oa_ref[0], ob_ref[0]
    d = D_MODEL
    for c in range(d // MIX_CHUNK):
        cols = slice(c * MIX_CHUNK, (c + 1) * MIX_CHUNK)

        def gate(k):
            lo = k * d + c * MIX_CHUNK
            return 1.0 / (1.0 + jnp.exp2(
                r[:, 3 * WIDTH + lo:3 * WIDTH + lo + MIX_CHUNK].astype(_F32)
                + bg_ref[0, :, lo:lo + MIX_CHUNK]))

        merged_ref[:, cols] = (
            gate(0) * lax.dot_general(oa, wpa_ref[0, :, cols], _TN, preferred_element_type=_F32)
            + gate(1) * lax.dot_general(ob, wpb_ref[0, :, cols], _TN,
                                        preferred_element_type=_F32)
            + gate(2) * jnp.dot(y, wpc_ref[0, :, cols], preferred_element_type=_F32)
        ).astype(_BF16)
    o_ref[0] = x_ref[0] + jnp.dot(merged_ref[...], wout_ref[0], preferred_element_type=_F32)


def _mix(layer, oa, ob, rest, x, convw, bg, wpa, wpb, wpc, wout):
    b, s, d = x.shape
    t = MIX_T
    tile = lambda width: pl.BlockSpec((1, t, width), lambda bi, i: (bi, i, 0))
    tile_t = pl.BlockSpec((1, WIDTH, t), lambda bi, i: (bi, 0, i))
    halo = pl.BlockSpec((1, HALO, REST_WIDTH),
                        lambda bi, i: (bi, jnp.maximum(i * (t // HALO) - 1, 0), 0))
    return pl.pallas_call(
        _mix_kernel,
        grid=(b, s // t),
        in_specs=[tile_t, tile_t, tile(REST_WIDTH), halo, tile(d),
                  _layer_spec(layer, (CONV_K, WIDTH)), _layer_spec(layer, (1, 3 * d)),
                  _layer_spec(layer, (WIDTH, d)), _layer_spec(layer, (WIDTH, d)),
                  _layer_spec(layer, (WIDTH, d)), _layer_spec(layer, (d, d))],
        out_specs=tile(d),
        out_shape=jax.ShapeDtypeStruct((b, s, d), _F32),
        scratch_shapes=[pltpu.VMEM((t, d), _BF16)],
        compiler_params=_params("arbitrary", "arbitrary"),
        name="mix",
    )(oa, ob, rest, rest, x, convw, bg, wpa, wpb, wpc, wout)


FFN_T = 1024
FF_CHUNK = 256


def _ffn_up_kernel(x_ref, g_ref, wg_ref, wu_ref, o_ref):
    h = _rms(x_ref[...], g_ref[0]).astype(_BF16)
    for n in range(D_FF // FF_CHUNK):
        cols = slice(n * FF_CHUNK, (n + 1) * FF_CHUNK)
        a = jnp.dot(h, wg_ref[0, :, cols], preferred_element_type=_F32)
        bgate = jnp.dot(h, wu_ref[0, :, cols], preferred_element_type=_F32)
        o_ref[:, cols] = (a * jax.nn.sigmoid(a) * bgate).astype(_BF16)


def _ffn_down_kernel(a_ref, x_ref, wd_ref, g_ref, o_ref, *, final_norm):
    y = x_ref[...] + jnp.dot(a_ref[...], wd_ref[0], preferred_element_type=_F32)
    if final_norm:
        y = _rms(y, g_ref[...])
    o_ref[...] = y


def _ffn(layer, x2d, g, wg, wu, wd, gfinal, final_norm):
    n, d = x2d.shape
    t = FFN_T
    act = pl.pallas_call(
        _ffn_up_kernel,
        grid=(n // t,),
        in_specs=[pl.BlockSpec((t, d), lambda i: (i, 0)), _layer_spec(layer, (1, d)),
                  _layer_spec(layer, (d, D_FF)), _layer_spec(layer, (d, D_FF))],
        out_specs=pl.BlockSpec((t, D_FF), lambda i: (i, 0)),
        out_shape=jax.ShapeDtypeStruct((n, D_FF), _BF16),
        compiler_params=_params("arbitrary"),
        name="ffn_up",
    )(x2d, g, wg, wu)
    return pl.pallas_call(
        functools.partial(_ffn_down_kernel, final_norm=final_norm),
        grid=(n // t,),
        in_specs=[pl.BlockSpec((t, D_FF), lambda i: (i, 0)),
                  pl.BlockSpec((t, d), lambda i: (i, 0)),
                  _layer_spec(layer, (D_FF, d)), _const_spec((1, d))],
        out_specs=pl.BlockSpec((t, d), lambda i: (i, 0)),
        out_shape=jax.ShapeDtypeStruct((n, d), _F32),
        compiler_params=_params("arbitrary"),
        name="ffn_down",
    )(act, x2d, wd, gfinal)


def _sb_permutation():
    rho = np.arange(KV_BLOCK)
    kappa = (rho % SUBLANES) * RUN + rho // SUBLANES
    p = np.zeros((KV_BLOCK, KV_BLOCK), np.float32)
    p[rho, kappa] = 1.0
    return jnp.asarray(p, _BF16)


def _moba_tables():
    slopes = np.exp2(-8.0 * (np.arange(N_HEADS, dtype=np.float64) + 1.0) / N_HEADS) * LOG2E
    kpos = np.zeros((KV_BLOCK, N_HEADS * PAIR), np.float32)
    qaug = np.zeros((N_HEADS, HEAD_DIM, MOBA_Q), np.float32)
    for hd in range(N_HEADS):
        rest = np.float64(slopes[hd])
        for a in range(N_AUG):
            piece = np.float64(np.float32(rest).astype(_BF16))
            kpos[:, hd * PAIR + HEAD_DIM + a] = np.arange(KV_BLOCK)
            qaug[hd, a, :] = piece
            rest = rest - piece
    key = np.arange(KV_BLOCK)[:, None]
    q_blk, q_off = np.divmod(np.arange(MOBA_Q)[None, :], KV_BLOCK)
    causal = np.stack([np.where((q_blk > g) | ((q_blk == g) & (key <= q_off)), 0.0, -np.inf)
                       for g in range(MOBA_G)])
    slope_tab = slopes[:, None, None] * np.ones((1, SUBLANES, MOBA_Q))
    return (jnp.asarray(kpos), jnp.asarray(qaug, _BF16),
            jnp.asarray(slope_tab.astype(np.float32)), jnp.asarray(causal.astype(np.float32)))


def kernel(x, norm_mix_g, w_in, b_gate, conv_w, w_proj_moba, w_proj_sb, w_proj_conv, w_out,
           norm_ffn_g, w_ffn_gate, w_ffn_up, w_ffn_down, norm_final_g):
    depth = w_in.shape[0]
    b, s, d = x.shape
    scale = HEAD_DIM ** -0.5
    perm = _sb_permutation()
    kpos, qaug, slopes, causal = _moba_tables()
    col_scale = np.ones((w_in.shape[2],), np.float32)
    col_scale[0:WIDTH] = scale * LOG2E
    col_scale[3 * WIDTH:4 * WIDTH] = scale * LOG2E
    col_scale[9 * WIDTH:] = -LOG2E
    w_all = (w_in * col_scale).astype(_BF16)
    bf16 = lambda p: p.astype(_BF16)
    wpa, wpb, wpc, wout = bf16(w_proj_moba), bf16(w_proj_sb), bf16(w_proj_conv), bf16(w_out)
    wg, wu, wd = bf16(w_ffn_gate), bf16(w_ffn_up), bf16(w_ffn_down)
    g_mix, g_ffn, bg = norm_mix_g[:, None, :], norm_ffn_g[:, None, :], b_gate[:, None, :] * -LOG2E
    for l in range(depth):
        qa_t, ka, va_t, kmean, kamax, qs_t, ks, vs_t, rest = _in_proj(
            l, x, g_mix, perm, kpos, w_all)
        oa = _moba(qa_t, qaug, ka, va_t, kmean, kamax, slopes, causal)
        ob = _stick_breaking(qs_t, ks, vs_t)
        x = _mix(l, oa, ob, rest, x, conv_w, bg, wpa, wpb, wpc, wout)
        x = _ffn(l, x.reshape(b * s, d), g_ffn, wg, wu, wd, norm_final_g[None, :],
                 final_norm=(l == depth - 1)).reshape(b, s, d)
    return x
```

```python
import functools

import jax
import jax.numpy as jnp
import numpy as np
from jax import lax
from jax.experimental import pallas as pl
from jax.experimental.pallas import tpu as pltpu

D_MODEL = 1024
HEAD_DIM = 64
N_HEADS = 8
WIDTH = N_HEADS * HEAD_DIM
CONV_K = 3
KV_BLOCK = 256
MOBA_TOPK = 3
MOBA_G = 2
MOBA_Q = MOBA_G * KV_BLOCK
SB_Q = KV_BLOCK
SB_TILES = 4
MOBA_SKIP = -150.0
D_FF = 2816
RMS_EPS = 1e-6
PAIR = 2 * HEAD_DIM
V_ROWS = HEAD_DIM + 16
N_AUG = 4
SUBLANES = 8
LOG2E = 1.4426950408889634
RUN = KV_BLOCK // SUBLANES
SB_STOP = 0.0
VMEM_LIMIT = 56 * 1024 * 1024

_TN = (((0,), (0,)), ((), ()))
_F32 = jnp.float32
_BF16 = jnp.bfloat16


def _params(*sem):
    return pltpu.CompilerParams(dimension_semantics=sem, vmem_limit_bytes=VMEM_LIMIT)


def _const_spec(shape):
    zeros = (0,) * len(shape)
    return pl.BlockSpec(shape, lambda *_: zeros)


def _layer_spec(layer, shape):
    index = (layer,) + (0,) * len(shape)
    return pl.BlockSpec((1,) + tuple(shape), lambda *_: index)


def _rms(x, g):
    y = x * lax.rsqrt(jnp.mean(x * x, axis=-1, keepdims=True) + RMS_EPS)
    return y * g


def _in_proj_kernel(x_ref, g_ref, perm_ref, kpos_ref, w_ref,
                    qa_ref, ka_ref, va_ref, kmean_ref, kamax_ref, qs_ref, ks_ref, vs_ref):
    i = pl.program_id(1)
    h = _rms(x_ref[0], g_ref[0]).astype(_BF16)

    def proj(lhs, k):
        return jnp.dot(lhs, w_ref[0, :, k * WIDTH:(k + 1) * WIDTH], preferred_element_type=_F32)

    qa_ref[0] = proj(h, 0).T.astype(_BF16)
    ka = proj(h, 1)
    lane = lax.broadcasted_iota(jnp.int32, (KV_BLOCK, PAIR), 1)
    tiles = []
    for pair in range(N_HEADS // 2):
        both = ka[:, pair * PAIR:(pair + 1) * PAIR]
        tiles += [jnp.where(lane < HEAD_DIM, both, 0.0),
                  jnp.where(lane < HEAD_DIM, pltpu.roll(both, HEAD_DIM, 1), 0.0)]
    ka_wide = jnp.concatenate(tiles, axis=1)
    ka_ref[0, 0] = (ka_wide + kpos_ref[...]).astype(_BF16)
    kmean_ref[0, pl.ds(i, 1), :] = jnp.mean(ka_wide, axis=0, keepdims=True)
    kamax_ref[0, pl.ds(i, 1), :] = jnp.max(jnp.abs(ka_wide.astype(_BF16).astype(_F32)),
                                           axis=0, keepdims=True)
    va_t = proj(h, 2).T.astype(_BF16)
    extra = (lax.broadcasted_iota(jnp.int32, (V_ROWS - HEAD_DIM, KV_BLOCK), 0) == 0)
    extra = extra.astype(_F32).astype(_BF16)
    va_ref[0, 0] = jnp.concatenate(
        [piece for hd in range(N_HEADS)
         for piece in (va_t[hd * HEAD_DIM:(hd + 1) * HEAD_DIM, :], extra)], axis=0)
    qs_ref[0] = proj(h, 3).T.astype(_BF16)
    hp = jnp.dot(perm_ref[...], h, preferred_element_type=_F32).astype(_BF16)
    ks_ref[0, 0] = proj(hp, 4).astype(_BF16)
    vs_ref[0, 0] = proj(hp, 5).T.astype(_BF16)


def _in_proj(layer, x, g, perm, kpos, w_all):
    b, s, d = x.shape
    nb = s // KV_BLOCK
    t = KV_BLOCK
    kw = N_HEADS * PAIR
    out_shape = (
        jax.ShapeDtypeStruct((b, WIDTH, s), _BF16),
        jax.ShapeDtypeStruct((b, nb, t, kw), _BF16),
        jax.ShapeDtypeStruct((b, nb, N_HEADS * V_ROWS, t), _BF16),
        jax.ShapeDtypeStruct((b, nb, kw), _F32),
        jax.ShapeDtypeStruct((b, nb, kw), _F32),
        jax.ShapeDtypeStruct((b, WIDTH, s), _BF16),
        jax.ShapeDtypeStruct((b, nb, t, WIDTH), _BF16),
        jax.ShapeDtypeStruct((b, nb, WIDTH, t), _BF16),
    )
    qt_spec = pl.BlockSpec((1, WIDTH, t), lambda bi, i: (bi, 0, i))
    k_spec = pl.BlockSpec((1, 1, t, WIDTH), lambda bi, i: (bi, i, 0, 0))
    vt_spec = pl.BlockSpec((1, 1, WIDTH, t), lambda bi, i: (bi, i, 0, 0))
    return pl.pallas_call(
        _in_proj_kernel,
        grid=(b, nb),
        in_specs=[
            pl.BlockSpec((1, t, d), lambda bi, i: (bi, i, 0)),
            _layer_spec(layer, (1, d)),
            _const_spec((t, t)),
            _const_spec((t, kw)),
            _layer_spec(layer, (d, 6 * WIDTH)),
        ],
        out_specs=(
            qt_spec,
            pl.BlockSpec((1, 1, t, kw), lambda bi, i: (bi, i, 0, 0)),
            pl.BlockSpec((1, 1, N_HEADS * V_ROWS, t), lambda bi, i: (bi, i, 0, 0)),
            pl.BlockSpec((1, nb, kw), lambda bi, i: (bi, 0, 0)),
            pl.BlockSpec((1, nb, kw), lambda bi, i: (bi, 0, 0)),
            qt_spec, k_spec, vt_spec,
        ),
        out_shape=out_shape,
        compiler_params=_params("arbitrary", "arbitrary"),
        name="in_proj",
    )(x, g, perm, kpos, w_all)


def _head_rows(q_pair, hh):
    row = lax.broadcasted_iota(jnp.int32, q_pair.shape, 0)
    keep = (row >= hh * HEAD_DIM) & (row < (hh + 1) * HEAD_DIM)
    return jnp.where(keep, q_pair, jnp.zeros_like(q_pair))


def _moba_kernel(q_ref, qaug_ref, k_ref, v_ref, kmean_ref, kamax_ref, slope_ref, causal_ref,
                 o_ref, sel_ref, sa_ref, sb_ref, acc_ref):
    last = pl.program_id(2)
    nb = k_ref.shape[1]
    nidx = lax.broadcasted_iota(jnp.int32, (nb, MOBA_Q), 0)
    blk = last * MOBA_G + (lax.broadcasted_iota(jnp.int32, (nb, MOBA_Q), 1)
                           >> (KV_BLOCK.bit_length() - 1))
    blk_row = blk[0:1, :]
    neg_inf = jnp.float32(-jnp.inf)
    slope = [slope_ref[hh, 0:1, :] for hh in range(2)]
    lanes = [slice(hh * PAIR, (hh + 1) * PAIR) for hh in range(2)]
    vrows = [slice(hh * V_ROWS, (hh + 1) * V_ROWS) for hh in range(2)]

    heads, reach = [], []
    for hh in range(2):
        qm = jnp.concatenate([q_ref[0, hh * HEAD_DIM:(hh + 1) * HEAD_DIM, :], qaug_ref[hh]], axis=0)
        km = kmean_ref[0, :, lanes[hh]].astype(_BF16)
        kamax = kamax_ref[0, :, lanes[hh]].astype(_BF16)
        bound = jnp.dot(kamax, jnp.abs(qm), preferred_element_type=_F32)
        reach.append(bound + slope[hh] * ((nidx + 1 - blk) * KV_BLOCK).astype(_F32))
        gate = jnp.dot(km, qm, preferred_element_type=_F32)
        g = jnp.where(nidx < blk, gate, neg_inf)
        sel = nidx == blk
        for r in range(MOBA_TOPK):
            mx = jnp.max(g, axis=0, keepdims=True)
            first = jnp.min(jnp.where(g == mx, nidx, nb), axis=0, keepdims=True)
            hit = nidx == first
            sel = sel | (hit & (jnp.full((nb, MOBA_Q), r, jnp.int32) < blk))
            g = jnp.where(hit, neg_inf, g)
        sel_ref[hh] = jnp.where(sel, 0.0, neg_inf)
        heads.append(qm)

    def shift(j):
        return ((j - blk_row) * KV_BLOCK).astype(_F32)

    def row_bias(hh, j):
        return sel_ref[hh, pl.ds(j, 1), :] + slope[hh] * shift(j)

    def scores(t, dst_ref, nearest=False):
        mx = []
        for hh in range(2):
            col = jnp.full((1, MOBA_Q), neg_inf, _F32)
            for g in range(MOBA_G):
                j = t * MOBA_G + g
                s = jnp.dot(k_ref[0, j, :, lanes[hh]], heads[hh], preferred_element_type=_F32)
                if nearest:
                    s = s + causal_ref[g]
                dst_ref[hh, g * KV_BLOCK:(g + 1) * KV_BLOCK, :] = s
                col = jnp.maximum(col, jnp.max(s, axis=0, keepdims=True) + row_bias(hh, j))
            mx.append(col)
        return mx

    def accumulate(t, src_ref, mx, st):
        out = []
        for hh in range(2):
            m_new = jnp.maximum(st[hh], mx[hh])
            acc = jnp.exp2(st[hh] - m_new) * acc_ref[hh]
            for g in range(MOBA_G):
                j = t * MOBA_G + g
                s = src_ref[hh, g * KV_BLOCK:(g + 1) * KV_BLOCK, :]
                p = jnp.exp2(s - (m_new - row_bias(hh, j)))
                acc = acc + jnp.dot(v_ref[0, j, vrows[hh], :], p.astype(_BF16),
                                    preferred_element_type=_F32)
            acc_ref[hh] = acc
            out.append(m_new)
        return out

    state = [jnp.full((1, MOBA_Q), jnp.finfo(_F32).min, _F32) for _ in range(2)]
    for hh in range(2):
        acc_ref[hh] = jnp.zeros((V_ROWS, MOBA_Q), _F32)

    mx_a = scores(last, sa_ref, nearest=True)
    mx_b = scores(jnp.maximum(last - 1, 0), sb_ref)
    first_needed = jnp.int32(nb)
    for hh in range(2):
        needed = (nidx < blk) & (reach[hh] - mx_a[hh] > MOBA_SKIP)
        first_needed = jnp.minimum(first_needed, jnp.min(jnp.where(needed, nidx, nb)))
    first = jnp.minimum(first_needed // MOBA_G, last)
    ntiles = last - first + 1
    state = accumulate(last, sa_ref, mx_a, state)

    def body(k, carry):
        st, mx_b = carry[:2], carry[2:]
        t = last - 1 - 2 * k
        mx_a = scores(t - 1, sa_ref)
        st = accumulate(t, sb_ref, mx_b, st)
        mx_b = scores(jnp.maximum(t - 2, 0), sb_ref)
        st = accumulate(t - 1, sa_ref, mx_a, st)
        return tuple(st) + tuple(mx_b)

    carry = lax.fori_loop(0, (ntiles - 1) // 2, body, tuple(state) + tuple(mx_b))

    @pl.when(ntiles % 2 == 0)
    def _():
        accumulate(first, sb_ref, carry[2:], carry[:2])

    o_ref[0] = jnp.concatenate(
        [acc_ref[hh, 0:HEAD_DIM, :] / acc_ref[hh, HEAD_DIM:HEAD_DIM + 1, :] for hh in range(2)],
        axis=0).astype(_BF16)


def _moba(qa_t, qaug, ka, va_t, kmean, kamax, slopes, causal):
    b, _, s = qa_t.shape
    nb = s // KV_BLOCK
    npair = N_HEADS // 2
    assert nb % MOBA_G == 0
    return pl.pallas_call(
        _moba_kernel,
        grid=(b, npair, s // MOBA_Q),
        in_specs=[
            pl.BlockSpec((1, PAIR, MOBA_Q), lambda bi, hp, c: (bi, hp, c)),
            pl.BlockSpec((2, HEAD_DIM, MOBA_Q), lambda bi, hp, c: (hp, 0, 0)),
            pl.BlockSpec((1, nb, KV_BLOCK, 2 * PAIR), lambda bi, hp, c: (bi, 0, 0, hp)),
            pl.BlockSpec((1, nb, 2 * V_ROWS, KV_BLOCK), lambda bi, hp, c: (bi, 0, hp, 0)),
            pl.BlockSpec((1, nb, 2 * PAIR), lambda bi, hp, c: (bi, 0, hp)),
            pl.BlockSpec((1, nb, 2 * PAIR), lambda bi, hp, c: (bi, 0, hp)),
            pl.BlockSpec((2, SUBLANES, MOBA_Q), lambda bi, hp, c: (hp, 0, 0)),
            _const_spec((MOBA_G, KV_BLOCK, MOBA_Q)),
        ],
        out_specs=pl.BlockSpec((1, PAIR, MOBA_Q), lambda bi, hp, c: (bi, hp, c)),
        out_shape=jax.ShapeDtypeStruct((b, WIDTH, s), _BF16),
        scratch_shapes=[pltpu.VMEM((2, nb, MOBA_Q), _F32),
                        pltpu.VMEM((2, MOBA_G * KV_BLOCK, MOBA_Q), _F32),
                        pltpu.VMEM((2, MOBA_G * KV_BLOCK, MOBA_Q), _F32),
                        pltpu.VMEM((2, V_ROWS, MOBA_Q), _F32)],
        compiler_params=_params("arbitrary", "arbitrary", "arbitrary"),
        name="moba",
    )(qa_t, qaug, ka, va_t, kmean, kamax, slopes, causal)


def _suffix_product_over_sublanes(x):
    sub = lax.broadcasted_iota(jnp.int32, x.shape, 0)
    y = x
    for d in (1, 2, 4):
        up = pltpu.roll(y, SUBLANES - d, 0)
        y = y * jnp.where(sub < SUBLANES - d, up, 1.0)
    return y


def _sb_scores(k_j, qm, mask):
    z = jnp.dot(k_j, qm, preferred_element_type=_F32)
    keep = 1.0 / (1.0 + jnp.exp2(z))
    if mask is not None:
        keep = jnp.where(mask, keep, 1.0)
    run = jnp.ones((SUBLANES, z.shape[1]), _F32)
    diff = [None] * RUN
    for r in reversed(range(RUN)):
        nxt = run * keep[r * SUBLANES:(r + 1) * SUBLANES, :]
        diff[r] = run - nxt
        run = nxt
    sub = lax.broadcasted_iota(jnp.int32, run.shape, 0)
    shifted = jnp.where(sub < SUBLANES - 1, pltpu.roll(run, SUBLANES - 1, 0), 1.0)
    return diff, run, _suffix_product_over_sublanes(shifted)


def _sb_weights(scores, v_jh, carry, acc):
    diff, run, later = scores
    base = later * carry
    w = jnp.concatenate([d * base for d in diff], axis=0)
    acc = acc + jnp.dot(v_jh, w.astype(_BF16), preferred_element_type=_F32)
    return carry * (later[0:1, :] * run[0:1, :]), acc


def _sb_kernel(q_ref, k_ref, v_ref, o_ref):
    rho = lax.broadcasted_iota(jnp.int32, (KV_BLOCK, SB_Q), 0)
    kpos = (rho & (SUBLANES - 1)) * RUN + (rho >> (SUBLANES.bit_length() - 1))
    qpos = lax.broadcasted_iota(jnp.int32, (KV_BLOCK, SB_Q), 1)
    strict = kpos < qpos
    rows = [slice(hh * HEAD_DIM, (hh + 1) * HEAD_DIM) for hh in range(2)]

    jds = [pl.program_id(2) * SB_TILES + u for u in range(SB_TILES)]
    jps = [jnp.maximum(jd - 1, 0) for jd in jds]
    heads = [[_head_rows(q_ref[0, :, u * SB_Q:(u + 1) * SB_Q], hh) for hh in range(2)]
             for u in range(SB_TILES)]
    diags = [[_sb_scores(k_ref[0, jds[u]], heads[u][hh], strict) for hh in range(2)]
             for u in range(SB_TILES)]
    prevs = [[_sb_scores(k_ref[0, jps[u]], heads[u][hh], None) for hh in range(2)]
             for u in range(SB_TILES)]
    states = []
    for u in range(SB_TILES):
        state = []
        for hh in range(2):
            carry, acc = _sb_weights(diags[u][hh], v_ref[0, jds[u], rows[hh], :],
                                     jnp.ones((1, SB_Q), _F32),
                                     jnp.zeros((HEAD_DIM, SB_Q), _F32))
            carry = jnp.where(jds[u] > 0, carry, 0.0)
            state += list(_sb_weights(prevs[u][hh], v_ref[0, jps[u], rows[hh], :], carry, acc))
        states.append(state)

    def more(trip, sts):
        flags = [jnp.logical_and(jds[u] - 2 - trip >= 0,
                                 jnp.maximum(jnp.max(sts[u][0]), jnp.max(sts[u][2])) > SB_STOP)
                 for u in range(SB_TILES)]
        return functools.reduce(jnp.logical_or, flags).astype(jnp.int32)

    def body(loop):
        trip, flat = loop[0], loop[2:]
        new = []
        for u in range(SB_TILES):
            j = jds[u] - 2 - trip
            jc = jnp.maximum(j, 0)
            out = []
            for hh in range(2):
                carry, acc = flat[4 * u + 2 * hh:4 * u + 2 * hh + 2]
                carry = jnp.where(j >= 0, carry, 0.0)
                out += list(_sb_weights(_sb_scores(k_ref[0, jc], heads[u][hh], None),
                                        v_ref[0, jc, rows[hh], :], carry, acc))
            new.append(out)
        return (trip + 1, more(trip + 1, new), *[a for st in new for a in st])

    zero = jnp.int32(0)
    final = lax.while_loop(lambda loop: loop[1] > 0, body,
                           (zero, more(zero, states), *[a for st in states for a in st]))[2:]
    for u in range(SB_TILES):
        o_ref[0, :, u * SB_Q:(u + 1) * SB_Q] = jnp.concatenate(
            [final[4 * u + 1], final[4 * u + 3]], axis=0).astype(_BF16)


def _stick_breaking(qs_t, ks, vs_t):
    b, _, s = qs_t.shape
    nb = s // KV_BLOCK
    npair = N_HEADS // 2
    return pl.pallas_call(
        _sb_kernel,
        grid=(b, npair, s // (SB_Q * SB_TILES)),
        in_specs=[
            pl.BlockSpec((1, PAIR, SB_Q * SB_TILES), lambda bi, hp, i: (bi, hp, i)),
            pl.BlockSpec((1, nb, KV_BLOCK, PAIR), lambda bi, hp, i: (bi, 0, 0, hp)),
            pl.BlockSpec((1, nb, PAIR, KV_BLOCK), lambda bi, hp, i: (bi, 0, hp, 0)),
        ],
        out_specs=pl.BlockSpec((1, PAIR, SB_Q * SB_TILES), lambda bi, hp, i: (bi, hp, i)),
        out_shape=jax.ShapeDtypeStruct((b, WIDTH, s), _BF16),
        compiler_params=_params("arbitrary", "arbitrary", "arbitrary"),
        name="stick_breaking",
    )(qs_t, ks, vs_t)


MIX_T = 512
MIX_CHUNK = 256
HALO = SUBLANES
GATE_SPLIT = 3 * WIDTH


def _mix_kernel(oa_ref, ob_ref, halo_ref, x_ref, g_ref, wconv_ref, wg0_ref, wg1_ref, convw_ref,
                bg_ref, wpa_ref, wpb_ref, wpc_ref, wout_ref, o_ref, merged_ref):
    i = pl.program_id(1)
    h = _rms(x_ref[0], g_ref[0]).astype(_BF16)
    r = jnp.dot(h, wconv_ref[0], preferred_element_type=_F32)
    xc, bc, cc = r[:, 0:WIDTH], r[:, WIDTH:2 * WIDTH], r[:, 2 * WIDTH:3 * WIDTH]
    u = cc * xc
    hl = jnp.dot(_rms(halo_ref[0], g_ref[0]).astype(_BF16), wconv_ref[0],
                 preferred_element_type=_F32)
    uh = hl[:, 2 * WIDTH:3 * WIDTH] * hl[:, 0:WIDTH]
    uh = jnp.where(i > 0, uh, 0.0)
    prev1 = uh[HALO - 1:HALO, :]
    prev2 = uh[HALO - 2:HALO - 1, :]
    row = lax.broadcasted_iota(jnp.int32, u.shape, 0)
    u1 = jnp.where(row == 0, prev1, pltpu.roll(u, 1, 0))
    u2 = jnp.where(row == 0, prev2, jnp.where(row == 1, prev1, pltpu.roll(u, 2, 0)))
    w = convw_ref[0]
    y = (bc * (u2 * w[0:1, :] + u1 * w[1:2, :] + u * w[2:3, :])).astype(_BF16)
    oa, ob = oa_ref[0], ob_ref[0]
    d = D_MODEL
    for c in range(d // MIX_CHUNK):
        cols = slice(c * MIX_CHUNK, (c + 1) * MIX_CHUNK)

        def gate(k):
            lo = k * d + c * MIX_CHUNK
            w_ref, at = (wg0_ref, lo) if lo < GATE_SPLIT else (wg1_ref, lo - GATE_SPLIT)
            pre = jnp.dot(h, w_ref[0, :, at:at + MIX_CHUNK], preferred_element_type=_F32)
            return 1.0 / (1.0 + jnp.exp2(pre + bg_ref[0, :, lo:lo + MIX_CHUNK]))

        merged_ref[:, cols] = (
            gate(0) * lax.dot_general(oa, wpa_ref[0, :, cols], _TN, preferred_element_type=_F32)
            + gate(1) * lax.dot_general(ob, wpb_ref[0, :, cols], _TN,
                                        preferred_element_type=_F32)
            + gate(2) * jnp.dot(y, wpc_ref[0, :, cols], preferred_element_type=_F32)
        ).astype(_BF16)
    o_ref[0] = x_ref[0] + jnp.dot(merged_ref[...], wout_ref[0], preferred_element_type=_F32)


def _mix(layer, oa, ob, x, g, w_all, convw, bg, wpa, wpb, wpc, wout):
    b, s, d = x.shape
    t = MIX_T
    tile = lambda width: pl.BlockSpec((1, t, width), lambda bi, i: (bi, i, 0))
    tile_t = pl.BlockSpec((1, WIDTH, t), lambda bi, i: (bi, 0, i))
    halo = pl.BlockSpec((1, HALO, d), lambda bi, i: (bi, jnp.maximum(i * (t // HALO) - 1, 0), 0))
    w_block = lambda n: pl.BlockSpec((1, d, GATE_SPLIT), lambda *_: (layer, 0, n))
    assert 6 * WIDTH == 2 * GATE_SPLIT and 3 * d == 2 * GATE_SPLIT
    return pl.pallas_call(
        _mix_kernel,
        grid=(b, s // t),
        in_specs=[tile_t, tile_t, halo, tile(d), _layer_spec(layer, (1, d)),
                  w_block(2), w_block(3), w_block(4),
                  _layer_spec(layer, (CONV_K, WIDTH)), _layer_spec(layer, (1, 3 * d)),
                  _layer_spec(layer, (WIDTH, d)), _layer_spec(layer, (WIDTH, d)),
                  _layer_spec(layer, (WIDTH, d)), _layer_spec(layer, (d, d))],
        out_specs=tile(d),
        out_shape=jax.ShapeDtypeStruct((b, s, d), _F32),
        scratch_shapes=[pltpu.VMEM((t, d), _BF16)],
        compiler_params=_params("arbitrary", "arbitrary"),
        name="mix",
    )(oa, ob, x, x, g, w_all, w_all, w_all, convw, bg, wpa, wpb, wpc, wout)


FFN_T = 1024
FF_CHUNK = 256


def _ffn_up_kernel(x_ref, g_ref, wg_ref, wu_ref, o_ref):
    h = _rms(x_ref[...], g_ref[0]).astype(_BF16)
    for n in range(D_FF // FF_CHUNK):
        cols = slice(n * FF_CHUNK, (n + 1) * FF_CHUNK)
        a = jnp.dot(h, wg_ref[0, :, cols], preferred_element_type=_F32)
        bgate = jnp.dot(h, wu_ref[0, :, cols], preferred_element_type=_F32)
        o_ref[:, cols] = (a * jax.nn.sigmoid(a) * bgate).astype(_BF16)


def _ffn_down_kernel(a_ref, x_ref, wd_ref, g_ref, o_ref, *, final_norm):
    y = x_ref[...] + jnp.dot(a_ref[...], wd_ref[0], preferred_element_type=_F32)
    if final_norm:
        y = _rms(y, g_ref[...])
    o_ref[...] = y


def _ffn(layer, x2d, g, wg, wu, wd, gfinal, final_norm):
    n, d = x2d.shape
    t = FFN_T
    act = pl.pallas_call(
        _ffn_up_kernel,
        grid=(n // t,),
        in_specs=[pl.BlockSpec((t, d), lambda i: (i, 0)), _layer_spec(layer, (1, d)),
                  _layer_spec(layer, (d, D_FF)), _layer_spec(layer, (d, D_FF))],
        out_specs=pl.BlockSpec((t, D_FF), lambda i: (i, 0)),
        out_shape=jax.ShapeDtypeStruct((n, D_FF), _BF16),
        compiler_params=_params("arbitrary"),
        name="ffn_up",
    )(x2d, g, wg, wu)
    return pl.pallas_call(
        functools.partial(_ffn_down_kernel, final_norm=final_norm),
        grid=(n // t,),
        in_specs=[pl.BlockSpec((t, D_FF), lambda i: (i, 0)),
                  pl.BlockSpec((t, d), lambda i: (i, 0)),
                  _layer_spec(layer, (D_FF, d)), _const_spec((1, d))],
        out_specs=pl.BlockSpec((t, d), lambda i: (i, 0)),
        out_shape=jax.ShapeDtypeStruct((n, d), _F32),
        compiler_params=_params("arbitrary"),
        name="ffn_down",
    )(act, x2d, wd, gfinal)


def _sb_permutation():
    rho = np.arange(KV_BLOCK)
    kappa = (rho % SUBLANES) * RUN + rho // SUBLANES
    p = np.zeros((KV_BLOCK, KV_BLOCK), np.float32)
    p[rho, kappa] = 1.0
    return jnp.asarray(p, _BF16)


def _moba_tables():
    slopes = np.exp2(-8.0 * (np.arange(N_HEADS, dtype=np.float64) + 1.0) / N_HEADS) * LOG2E
    kpos = np.zeros((KV_BLOCK, N_HEADS * PAIR), np.float32)
    qaug = np.zeros((N_HEADS, HEAD_DIM, MOBA_Q), np.float32)
    for hd in range(N_HEADS):
        rest = np.float64(slopes[hd])
        for a in range(N_AUG):
            piece = np.float64(np.float32(rest).astype(_BF16))
            kpos[:, hd * PAIR + HEAD_DIM + a] = np.arange(KV_BLOCK)
            qaug[hd, a, :] = piece
            rest = rest - piece
    key = np.arange(KV_BLOCK)[:, None]
    q_blk, q_off = np.divmod(np.arange(MOBA_Q)[None, :], KV_BLOCK)
    causal = np.stack([np.where((q_blk > g) | ((q_blk == g) & (key <= q_off)), 0.0, -np.inf)
                       for g in range(MOBA_G)])
    slope_tab = slopes[:, None, None] * np.ones((1, SUBLANES, MOBA_Q))
    return (jnp.asarray(kpos), jnp.asarray(qaug, _BF16),
            jnp.asarray(slope_tab.astype(np.float32)), jnp.asarray(causal.astype(np.float32)))


def kernel(x, norm_mix_g, w_in, b_gate, conv_w, w_proj_moba, w_proj_sb, w_proj_conv, w_out,
           norm_ffn_g, w_ffn_gate, w_ffn_up, w_ffn_down, norm_final_g):
    depth = w_in.shape[0]
    b, s, d = x.shape
    scale = HEAD_DIM ** -0.5
    perm = _sb_permutation()
    kpos, qaug, slopes, causal = _moba_tables()
    col_scale = np.ones((w_in.shape[2],), np.float32)
    col_scale[0:WIDTH] = scale * LOG2E
    col_scale[3 * WIDTH:4 * WIDTH] = scale * LOG2E
    col_scale[9 * WIDTH:] = -LOG2E
    w_all = (w_in * col_scale).astype(_BF16)
    bf16 = lambda p: p.astype(_BF16)
    wpa, wpb, wpc, wout = bf16(w_proj_moba), bf16(w_proj_sb), bf16(w_proj_conv), bf16(w_out)
    wg, wu, wd = bf16(w_ffn_gate), bf16(w_ffn_up), bf16(w_ffn_down)
    g_mix, g_ffn, bg = norm_mix_g[:, None, :], norm_ffn_g[:, None, :], b_gate[:, None, :] * -LOG2E
    for l in range(depth):
        qa_t, ka, va_t, kmean, kamax, qs_t, ks, vs_t = _in_proj(l, x, g_mix, perm, kpos, w_all)
        oa = _moba(qa_t, qaug, ka, va_t, kmean, kamax, slopes, causal)
        ob = _stick_breaking(qs_t, ks, vs_t)
        x = _mix(l, oa, ob, x, g_mix, w_all, conv_w, bg, wpa, wpb, wpc, wout)
        x = _ffn(l, x.reshape(b * s, d), g_ffn, wg, wu, wd, norm_final_g[None, :],
                 final_norm=(l == depth - 1)).reshape(b, s, d)
    return x
```

```python
import functools

import jax
import jax.numpy as jnp
import numpy as np
from jax import lax
from jax.experimental import pallas as pl
from jax.experimental.pallas import tpu as pltpu

D_MODEL = 1024
HEAD_DIM = 64
N_HEADS = 8
WIDTH = N_HEADS * HEAD_DIM
CONV_K = 3
KV_BLOCK = 256
IN_BLOCKS = 2
MOBA_TOPK = 3
MOBA_G = 2
MOBA_Q = MOBA_G * KV_BLOCK
SB_Q = KV_BLOCK
SB_TILES = 4
MOBA_SKIP = -150.0
D_FF = 2816
RMS_EPS = 1e-6
PAIR = 2 * HEAD_DIM
V_ROWS = HEAD_DIM + 16
N_AUG = 4
SUBLANES = 8
LOG2E = 1.4426950408889634
RUN = KV_BLOCK // SUBLANES
SB_STOP = 0.0
VMEM_LIMIT = 56 * 1024 * 1024

_TN = (((0,), (0,)), ((), ()))
_F32 = jnp.float32
_BF16 = jnp.bfloat16


def _params(*sem):
    return pltpu.CompilerParams(dimension_semantics=sem, vmem_limit_bytes=VMEM_LIMIT)


def _const_spec(shape):
    zeros = (0,) * len(shape)
    return pl.BlockSpec(shape, lambda *_: zeros)


def _layer_spec(layer, shape):
    index = (layer,) + (0,) * len(shape)
    return pl.BlockSpec((1,) + tuple(shape), lambda *_: index)


def _rms(x, g):
    y = x * lax.rsqrt(jnp.mean(x * x, axis=-1, keepdims=True) + RMS_EPS)
    return y * g


def _in_proj_kernel(x_ref, g_ref, perm_ref, kpos_ref, w_ref,
                    qa_ref, ka_ref, va_ref, kmean_ref, kamax_ref, qs_ref, ks_ref, vs_ref):
    i = pl.program_id(1)
    h = _rms(x_ref[0], g_ref[0]).astype(_BF16)
    blocks = [slice(n * KV_BLOCK, (n + 1) * KV_BLOCK) for n in range(IN_BLOCKS)]

    def proj(lhs, k):
        return jnp.dot(lhs, w_ref[0, :, k * WIDTH:(k + 1) * WIDTH], preferred_element_type=_F32)

    qa_ref[0] = proj(h, 0).T.astype(_BF16)
    ka = proj(h, 1)
    lane = lax.broadcasted_iota(jnp.int32, (ka.shape[0], PAIR), 1)
    tiles = []
    for pair in range(N_HEADS // 2):
        both = ka[:, pair * PAIR:(pair + 1) * PAIR]
        tiles += [jnp.where(lane < HEAD_DIM, both, 0.0),
                  jnp.where(lane < HEAD_DIM, pltpu.roll(both, HEAD_DIM, 1), 0.0)]
    ka_wide = jnp.concatenate(tiles, axis=1)
    va_t = proj(h, 2).T.astype(_BF16)
    extra = (lax.broadcasted_iota(jnp.int32, (V_ROWS - HEAD_DIM, KV_BLOCK), 0) == 0)
    extra = extra.astype(_F32).astype(_BF16)
    qs_ref[0] = proj(h, 3).T.astype(_BF16)
    hp = jnp.concatenate(
        [jnp.dot(perm_ref[...], h[rows, :], preferred_element_type=_F32).astype(_BF16)
         for rows in blocks], axis=0)
    ks = proj(hp, 4).astype(_BF16)
    vs_t = proj(hp, 5).T.astype(_BF16)
    for n, rows in enumerate(blocks):
        blk_k = ka_wide[rows, :]
        ka_ref[0, n] = (blk_k + kpos_ref[...]).astype(_BF16)
        kmean_ref[0, pl.ds(i * IN_BLOCKS + n, 1), :] = jnp.mean(blk_k, axis=0, keepdims=True)
        kamax_ref[0, pl.ds(i * IN_BLOCKS + n, 1), :] = jnp.max(
            jnp.abs(blk_k.astype(_BF16).astype(_F32)), axis=0, keepdims=True)
        va_ref[0, n] = jnp.concatenate(
            [piece for hd in range(N_HEADS)
             for piece in (va_t[hd * HEAD_DIM:(hd + 1) * HEAD_DIM, rows], extra)], axis=0)
        ks_ref[0, n] = ks[rows, :]
        vs_ref[0, n] = vs_t[:, rows]


def _in_proj(layer, x, g, perm, kpos, w_all):
    b, s, d = x.shape
    nb = s // KV_BLOCK
    kb = KV_BLOCK
    t = IN_BLOCKS * kb
    kw = N_HEADS * PAIR
    out_shape = (
        jax.ShapeDtypeStruct((b, WIDTH, s), _BF16),
        jax.ShapeDtypeStruct((b, nb, kb, kw), _BF16),
        jax.ShapeDtypeStruct((b, nb, N_HEADS * V_ROWS, kb), _BF16),
        jax.ShapeDtypeStruct((b, nb, kw), _F32),
        jax.ShapeDtypeStruct((b, nb, kw), _F32),
        jax.ShapeDtypeStruct((b, WIDTH, s), _BF16),
        jax.ShapeDtypeStruct((b, nb, kb, WIDTH), _BF16),
        jax.ShapeDtypeStruct((b, nb, WIDTH, kb), _BF16),
    )
    qt_spec = pl.BlockSpec((1, WIDTH, t), lambda bi, i: (bi, 0, i))
    per_block = lambda *shape: pl.BlockSpec((1, IN_BLOCKS) + shape, lambda bi, i: (bi, i, 0, 0))
    return pl.pallas_call(
        _in_proj_kernel,
        grid=(b, nb // IN_BLOCKS),
        in_specs=[
            pl.BlockSpec((1, t, d), lambda bi, i: (bi, i, 0)),
            _layer_spec(layer, (1, d)),
            _const_spec((kb, kb)),
            _const_spec((kb, kw)),
            _layer_spec(layer, (d, 6 * WIDTH)),
        ],
        out_specs=(
            qt_spec,
            per_block(kb, kw),
            per_block(N_HEADS * V_ROWS, kb),
            pl.BlockSpec((1, nb, kw), lambda bi, i: (bi, 0, 0)),
            pl.BlockSpec((1, nb, kw), lambda bi, i: (bi, 0, 0)),
            qt_spec, per_block(kb, WIDTH), per_block(WIDTH, kb),
        ),
        out_shape=out_shape,
        compiler_params=_params("arbitrary", "arbitrary"),
        name="in_proj",
    )(x, g, perm, kpos, w_all)


def _head_rows(q_pair, hh):
    row = lax.broadcasted_iota(jnp.int32, q_pair.shape, 0)
    keep = (row >= hh * HEAD_DIM) & (row < (hh + 1) * HEAD_DIM)
    return jnp.where(keep, q_pair, jnp.zeros_like(q_pair))


def _moba_kernel(q_ref, qaug_ref, k_ref, v_ref, kmean_ref, kamax_ref, slope_ref, causal_ref,
                 o_ref, sel_ref, sa_ref, sb_ref, acc_ref):
    last = pl.program_id(2)
    nb = k_ref.shape[1]
    nidx = lax.broadcasted_iota(jnp.int32, (nb, MOBA_Q), 0)
    blk = last * MOBA_G + (lax.broadcasted_iota(jnp.int32, (nb, MOBA_Q), 1)
                           >> (KV_BLOCK.bit_length() - 1))
    blk_row = blk[0:1, :]
    neg_inf = jnp.float32(-jnp.inf)
    slope = [slope_ref[hh, 0:1, :] for hh in range(2)]
    lanes = [slice(hh * PAIR, (hh + 1) * PAIR) for hh in range(2)]
    vrows = [slice(hh * V_ROWS, (hh + 1) * V_ROWS) for hh in range(2)]

    heads, reach = [], []
    for hh in range(2):
        qm = jnp.concatenate([q_ref[0, hh * HEAD_DIM:(hh + 1) * HEAD_DIM, :], qaug_ref[hh]], axis=0)
        km = kmean_ref[0, :, lanes[hh]].astype(_BF16)
        kamax = kamax_ref[0, :, lanes[hh]].astype(_BF16)
        bound = jnp.dot(kamax, jnp.abs(qm), preferred_element_type=_F32)
        reach.append(bound + slope[hh] * ((nidx + 1 - blk) * KV_BLOCK).astype(_F32))
        gate = jnp.dot(km, qm, preferred_element_type=_F32)
        g = jnp.where(nidx < blk, gate, neg_inf)
        sel = nidx == blk
        for r in range(MOBA_TOPK):
            mx = jnp.max(g, axis=0, keepdims=True)
            first = jnp.min(jnp.where(g == mx, nidx, nb), axis=0, keepdims=True)
            hit = nidx == first
            sel = sel | (hit & (jnp.full((nb, MOBA_Q), r, jnp.int32) < blk))
            g = jnp.where(hit, neg_inf, g)
        sel_ref[hh] = jnp.where(sel, 0.0, neg_inf)
        heads.append(qm)

    def shift(j):
        return ((j - blk_row) * KV_BLOCK).astype(_F32)

    def row_bias(hh, j):
        return sel_ref[hh, pl.ds(j, 1), :] + slope[hh] * shift(j)

    def scores(t, dst_ref, nearest=False):
        mx = []
        for hh in range(2):
            col = jnp.full((1, MOBA_Q), neg_inf, _F32)
            for g in range(MOBA_G):
                j = t * MOBA_G + g
                s = jnp.dot(k_ref[0, j, :, lanes[hh]], heads[hh], preferred_element_type=_F32)
                if nearest:
                    s = s + causal_ref[g]
                dst_ref[hh, g * KV_BLOCK:(g + 1) * KV_BLOCK, :] = s
                col = jnp.maximum(col, jnp.max(s, axis=0, keepdims=True) + row_bias(hh, j))
            mx.append(col)
        return mx

    def accumulate(t, src_ref, mx, st):
        out = []
        for hh in range(2):
            m_new = jnp.maximum(st[hh], mx[hh])
            acc = jnp.exp2(st[hh] - m_new) * acc_ref[hh]
            for g in range(MOBA_G):
                j = t * MOBA_G + g
                s = src_ref[hh, g * KV_BLOCK:(g + 1) * KV_BLOCK, :]
                p = jnp.exp2(s - (m_new - row_bias(hh, j)))
                acc = acc + jnp.dot(v_ref[0, j, vrows[hh], :], p.astype(_BF16),
                                    preferred_element_type=_F32)
            acc_ref[hh] = acc
            out.append(m_new)
        return out

    state = [jnp.full((1, MOBA_Q), jnp.finfo(_F32).min, _F32) for _ in range(2)]
    for hh in range(2):
        acc_ref[hh] = jnp.zeros((V_ROWS, MOBA_Q), _F32)

    mx_a = scores(last, sa_ref, nearest=True)
    mx_b = scores(jnp.maximum(last - 1, 0), sb_ref)
    first_needed = jnp.int32(nb)
    for hh in range(2):
        needed = (nidx < blk) & (reach[hh] - mx_a[hh] > MOBA_SKIP)
        first_needed = jnp.minimum(first_needed, jnp.min(jnp.where(needed, nidx, nb)))
    first = jnp.minimum(first_needed // MOBA_G, last)
    ntiles = last - first + 1
    state = accumulate(last, sa_ref, mx_a, state)

    def body(k, carry):
        st, mx_b = carry[:2], carry[2:]
        t = last - 1 - 2 * k
        mx_a = scores(t - 1, sa_ref)
        st = accumulate(t, sb_ref, mx_b, st)
        mx_b = scores(jnp.maximum(t - 2, 0), sb_ref)
        st = accumulate(t - 1, sa_ref, mx_a, st)
        return tuple(st) + tuple(mx_b)

    carry = lax.fori_loop(0, (ntiles - 1) // 2, body, tuple(state) + tuple(mx_b))

    @pl.when(ntiles % 2 == 0)
    def _():
        accumulate(first, sb_ref, carry[2:], carry[:2])

    o_ref[0] = jnp.concatenate(
        [acc_ref[hh, 0:HEAD_DIM, :] / acc_ref[hh, HEAD_DIM:HEAD_DIM + 1, :] for hh in range(2)],
        axis=0).astype(_BF16)


def _moba(qa_t, qaug, ka, va_t, kmean, kamax, slopes, causal):
    b, _, s = qa_t.shape
    nb = s // KV_BLOCK
    npair = N_HEADS // 2
    assert nb % MOBA_G == 0
    return pl.pallas_call(
        _moba_kernel,
        grid=(b, npair, s // MOBA_Q),
        in_specs=[
            pl.BlockSpec((1, PAIR, MOBA_Q), lambda bi, hp, c: (bi, hp, c)),
            pl.BlockSpec((2, HEAD_DIM, MOBA_Q), lambda bi, hp, c: (hp, 0, 0)),
            pl.BlockSpec((1, nb, KV_BLOCK, 2 * PAIR), lambda bi, hp, c: (bi, 0, 0, hp)),
            pl.BlockSpec((1, nb, 2 * V_ROWS, KV_BLOCK), lambda bi, hp, c: (bi, 0, hp, 0)),
            pl.BlockSpec((1, nb, 2 * PAIR), lambda bi, hp, c: (bi, 0, hp)),
            pl.BlockSpec((1, nb, 2 * PAIR), lambda bi, hp, c: (bi, 0, hp)),
            pl.BlockSpec((2, SUBLANES, MOBA_Q), lambda bi, hp, c: (hp, 0, 0)),
            _const_spec((MOBA_G, KV_BLOCK, MOBA_Q)),
        ],
        out_specs=pl.BlockSpec((1, PAIR, MOBA_Q), lambda bi, hp, c: (bi, hp, c)),
        out_shape=jax.ShapeDtypeStruct((b, WIDTH, s), _BF16),
        scratch_shapes=[pltpu.VMEM((2, nb, MOBA_Q), _F32),
                        pltpu.VMEM((2, MOBA_G * KV_BLOCK, MOBA_Q), _F32),
                        pltpu.VMEM((2, MOBA_G * KV_BLOCK, MOBA_Q), _F32),
                        pltpu.VMEM((2, V_ROWS, MOBA_Q), _F32)],
        compiler_params=_params("arbitrary", "arbitrary", "arbitrary"),
        name="moba",
    )(qa_t, qaug, ka, va_t, kmean, kamax, slopes, causal)


def _suffix_product_over_sublanes(x):
    sub = lax.broadcasted_iota(jnp.int32, x.shape, 0)
    y = x
    for d in (1, 2, 4):
        up = pltpu.roll(y, SUBLANES - d, 0)
        y = y * jnp.where(sub < SUBLANES - d, up, 1.0)
    return y


def _sb_scores(k_j, qm, mask):
    z = jnp.dot(k_j, qm, preferred_element_type=_F32)
    keep = 1.0 / (1.0 + jnp.exp2(z))
    if mask is not None:
        keep = jnp.where(mask, keep, 1.0)
    run = jnp.ones((SUBLANES, z.shape[1]), _F32)
    diff = [None] * RUN
    for r in reversed(range(RUN)):
        nxt = run * keep[r * SUBLANES:(r + 1) * SUBLANES, :]
        diff[r] = run - nxt
        run = nxt
    sub = lax.broadcasted_iota(jnp.int32, run.shape, 0)
    shifted = jnp.where(sub < SUBLANES - 1, pltpu.roll(run, SUBLANES - 1, 0), 1.0)
    return diff, run, _suffix_product_over_sublanes(shifted)


def _sb_weights(scores, v_jh, carry, acc):
    diff, run, later = scores
    base = later * carry
    w = jnp.concatenate([d * base for d in diff], axis=0)
    acc = acc + jnp.dot(v_jh, w.astype(_BF16), preferred_element_type=_F32)
    return carry * (later[0:1, :] * run[0:1, :]), acc


def _sb_kernel(q_ref, k_ref, v_ref, o_ref):
    rho = lax.broadcasted_iota(jnp.int32, (KV_BLOCK, SB_Q), 0)
    kpos = (rho & (SUBLANES - 1)) * RUN + (rho >> (SUBLANES.bit_length() - 1))
    qpos = lax.broadcasted_iota(jnp.int32, (KV_BLOCK, SB_Q), 1)
    strict = kpos < qpos
    rows = [slice(hh * HEAD_DIM, (hh + 1) * HEAD_DIM) for hh in range(2)]

    jds = [pl.program_id(2) * SB_TILES + u for u in range(SB_TILES)]
    jps = [jnp.maximum(jd - 1, 0) for jd in jds]
    heads = [[_head_rows(q_ref[0, :, u * SB_Q:(u + 1) * SB_Q], hh) for hh in range(2)]
             for u in range(SB_TILES)]
    diags = [[_sb_scores(k_ref[0, jds[u]], heads[u][hh], strict) for hh in range(2)]
             for u in range(SB_TILES)]
    prevs = [[_sb_scores(k_ref[0, jps[u]], heads[u][hh], None) for hh in range(2)]
             for u in range(SB_TILES)]
    states = []
    for u in range(SB_TILES):
        state = []
        for hh in range(2):
            carry, acc = _sb_weights(diags[u][hh], v_ref[0, jds[u], rows[hh], :],
                                     jnp.ones((1, SB_Q), _F32),
                                     jnp.zeros((HEAD_DIM, SB_Q), _F32))
            carry = jnp.where(jds[u] > 0, carry, 0.0)
            state += list(_sb_weights(prevs[u][hh], v_ref[0, jps[u], rows[hh], :], carry, acc))
        states.append(state)

    def more(trip, sts):
        flags = [jnp.logical_and(jds[u] - 2 - trip >= 0,
                                 jnp.maximum(jnp.max(sts[u][0]), jnp.max(sts[u][2])) > SB_STOP)
                 for u in range(SB_TILES)]
        return functools.reduce(jnp.logical_or, flags).astype(jnp.int32)

    def body(loop):
        trip, flat = loop[0], loop[2:]
        new = []
        for u in range(SB_TILES):
            j = jds[u] - 2 - trip
            jc = jnp.maximum(j, 0)
            out = []
            for hh in range(2):
                carry, acc = flat[4 * u + 2 * hh:4 * u + 2 * hh + 2]
                carry = jnp.where(j >= 0, carry, 0.0)
                out += list(_sb_weights(_sb_scores(k_ref[0, jc], heads[u][hh], None),
                                        v_ref[0, jc, rows[hh], :], carry, acc))
            new.append(out)
        return (trip + 1, more(trip + 1, new), *[a for st in new for a in st])

    zero = jnp.int32(0)
    final = lax.while_loop(lambda loop: loop[1] > 0, body,
                           (zero, more(zero, states), *[a for st in states for a in st]))[2:]
    for u in range(SB_TILES):
        o_ref[0, :, u * SB_Q:(u + 1) * SB_Q] = jnp.concatenate(
            [final[4 * u + 1], final[4 * u + 3]], axis=0).astype(_BF16)


def _stick_breaking(qs_t, ks, vs_t):
    b, _, s = qs_t.shape
    nb = s // KV_BLOCK
    npair = N_HEADS // 2
    return pl.pallas_call(
        _sb_kernel,
        grid=(b, npair, s // (SB_Q * SB_TILES)),
        in_specs=[
            pl.BlockSpec((1, PAIR, SB_Q * SB_TILES), lambda bi, hp, i: (bi, hp, i)),
            pl.BlockSpec((1, nb, KV_BLOCK, PAIR), lambda bi, hp, i: (bi, 0, 0, hp)),
            pl.BlockSpec((1, nb, PAIR, KV_BLOCK), lambda bi, hp, i: (bi, 0, hp, 0)),
        ],
        out_specs=pl.BlockSpec((1, PAIR, SB_Q * SB_TILES), lambda bi, hp, i: (bi, hp, i)),
        out_shape=jax.ShapeDtypeStruct((b, WIDTH, s), _BF16),
        compiler_params=_params("arbitrary", "arbitrary", "arbitrary"),
        name="stick_breaking",
    )(qs_t, ks, vs_t)


MIX_T = 512
MIX_CHUNK = 256
HALO = SUBLANES
GATE_SPLIT = 3 * WIDTH


def _mix_kernel(oa_ref, ob_ref, halo_ref, x_ref, g_ref, wconv_ref, wg0_ref, wg1_ref, convw_ref,
                bg_ref, wpa_ref, wpb_ref, wpc_ref, wout_ref, o_ref, merged_ref):
    i = pl.program_id(1)
    h = _rms(x_ref[0], g_ref[0]).astype(_BF16)
    r = jnp.dot(h, wconv_ref[0], preferred_element_type=_F32)
    xc, bc, cc = r[:, 0:WIDTH], r[:, WIDTH:2 * WIDTH], r[:, 2 * WIDTH:3 * WIDTH]
    u = cc * xc
    hl = jnp.dot(_rms(halo_ref[0], g_ref[0]).astype(_BF16), wconv_ref[0],
                 preferred_element_type=_F32)
    uh = hl[:, 2 * WIDTH:3 * WIDTH] * hl[:, 0:WIDTH]
    uh = jnp.where(i > 0, uh, 0.0)
    prev1 = uh[HALO - 1:HALO, :]
    prev2 = uh[HALO - 2:HALO - 1, :]
    row = lax.broadcasted_iota(jnp.int32, u.shape, 0)
    u1 = jnp.where(row == 0, prev1, pltpu.roll(u, 1, 0))
    u2 = jnp.where(row == 0, prev2, jnp.where(row == 1, prev1, pltpu.roll(u, 2, 0)))
    w = convw_ref[0]
    y = (bc * (u2 * w[0:1, :] + u1 * w[1:2, :] + u * w[2:3, :])).astype(_BF16)
    oa, ob = oa_ref[0], ob_ref[0]
    d = D_MODEL
    for c in range(d // MIX_CHUNK):
        cols = slice(c * MIX_CHUNK, (c + 1) * MIX_CHUNK)

        def gate(k):
            lo = k * d + c * MIX_CHUNK
            w_ref, at = (wg0_ref, lo) if lo < GATE_SPLIT else (wg1_ref, lo - GATE_SPLIT)
            pre = jnp.dot(h, w_ref[0, :, at:at + MIX_CHUNK], preferred_element_type=_F32)
            return 1.0 / (1.0 + jnp.exp2(pre + bg_ref[0, :, lo:lo + MIX_CHUNK]))

        merged_ref[:, cols] = (
            gate(0) * lax.dot_general(oa, wpa_ref[0, :, cols], _TN, preferred_element_type=_F32)
            + gate(1) * lax.dot_general(ob, wpb_ref[0, :, cols], _TN,
                                        preferred_element_type=_F32)
            + gate(2) * jnp.dot(y, wpc_ref[0, :, cols], preferred_element_type=_F32)
        ).astype(_BF16)
    o_ref[0] = x_ref[0] + jnp.dot(merged_ref[...], wout_ref[0], preferred_element_type=_F32)


def _mix(layer, oa, ob, x, g, w_all, convw, bg, wpa, wpb, wpc, wout):
    b, s, d = x.shape
    t = MIX_T
    tile = lambda width: pl.BlockSpec((1, t, width), lambda bi, i: (bi, i, 0))
    tile_t = pl.BlockSpec((1, WIDTH, t), lambda bi, i: (bi, 0, i))
    halo = pl.BlockSpec((1, HALO, d), lambda bi, i: (bi, jnp.maximum(i * (t // HALO) - 1, 0), 0))
    w_block = lambda n: pl.BlockSpec((1, d, GATE_SPLIT), lambda *_: (layer, 0, n))
    assert 6 * WIDTH == 2 * GATE_SPLIT and 3 * d == 2 * GATE_SPLIT
    return pl.pallas_call(
        _mix_kernel,
        grid=(b, s // t),
        in_specs=[tile_t, tile_t, halo, tile(d), _layer_spec(layer, (1, d)),
                  w_block(2), w_block(3), w_block(4),
                  _layer_spec(layer, (CONV_K, WIDTH)), _layer_spec(layer, (1, 3 * d)),
                  _layer_spec(layer, (WIDTH, d)), _layer_spec(layer, (WIDTH, d)),
                  _layer_spec(layer, (WIDTH, d)), _layer_spec(layer, (d, d))],
        out_specs=tile(d),
        out_shape=jax.ShapeDtypeStruct((b, s, d), _F32),
        scratch_shapes=[pltpu.VMEM((t, d), _BF16)],
        compiler_params=_params("arbitrary", "arbitrary"),
        name="mix",
    )(oa, ob, x, x, g, w_all, w_all, w_all, convw, bg, wpa, wpb, wpc, wout)


FFN_T = 1024
FF_CHUNK = 256


def _ffn_up_kernel(x_ref, g_ref, wg_ref, wu_ref, o_ref):
    h = _rms(x_ref[...], g_ref[0]).astype(_BF16)
    for n in range(D_FF // FF_CHUNK):
        cols = slice(n * FF_CHUNK, (n + 1) * FF_CHUNK)
        a = jnp.dot(h, wg_ref[0, :, cols], preferred_element_type=_F32)
        bgate = jnp.dot(h, wu_ref[0, :, cols], preferred_element_type=_F32)
        o_ref[:, cols] = (a * jax.nn.sigmoid(a) * bgate).astype(_BF16)


def _ffn_down_kernel(a_ref, x_ref, wd_ref, g_ref, o_ref, *, final_norm):
    y = x_ref[...] + jnp.dot(a_ref[...], wd_ref[0], preferred_element_type=_F32)
    if final_norm:
        y = _rms(y, g_ref[...])
    o_ref[...] = y


def _ffn(layer, x2d, g, wg, wu, wd, gfinal, final_norm):
    n, d = x2d.shape
    t = FFN_T
    act = pl.pallas_call(
        _ffn_up_kernel,
        grid=(n // t,),
        in_specs=[pl.BlockSpec((t, d), lambda i: (i, 0)), _layer_spec(layer, (1, d)),
                  _layer_spec(layer, (d, D_FF)), _layer_spec(layer, (d, D_FF))],
        out_specs=pl.BlockSpec((t, D_FF), lambda i: (i, 0)),
        out_shape=jax.ShapeDtypeStruct((n, D_FF), _BF16),
        compiler_params=_params("arbitrary"),
        name="ffn_up",
    )(x2d, g, wg, wu)
    return pl.pallas_call(
        functools.partial(_ffn_down_kernel, final_norm=final_norm),
        grid=(n // t,),
        in_specs=[pl.BlockSpec((t, D_FF), lambda i: (i, 0)),
                  pl.BlockSpec((t, d), lambda i: (i, 0)),
                  _layer_spec(layer, (D_FF, d)), _const_spec((1, d))],
        out_specs=pl.BlockSpec((t, d), lambda i: (i, 0)),
        out_shape=jax.ShapeDtypeStruct((n, d), _F32),
        compiler_params=_params("arbitrary"),
        name="ffn_down",
    )(act, x2d, wd, gfinal)


def _sb_permutation():
    rho = np.arange(KV_BLOCK)
    kappa = (rho % SUBLANES) * RUN + rho // SUBLANES
    p = np.zeros((KV_BLOCK, KV_BLOCK), np.float32)
    p[rho, kappa] = 1.0
    return jnp.asarray(p, _BF16)


def _moba_tables():
    slopes = np.exp2(-8.0 * (np.arange(N_HEADS, dtype=np.float64) + 1.0) / N_HEADS) * LOG2E
    kpos = np.zeros((KV_BLOCK, N_HEADS * PAIR), np.float32)
    qaug = np.zeros((N_HEADS, HEAD_DIM, MOBA_Q), np.float32)
    for hd in range(N_HEADS):
        rest = np.float64(slopes[hd])
        for a in range(N_AUG):
            piece = np.float64(np.float32(rest).astype(_BF16))
            kpos[:, hd * PAIR + HEAD_DIM + a] = np.arange(KV_BLOCK)
            qaug[hd, a, :] = piece
            rest = rest - piece
    key = np.arange(KV_BLOCK)[:, None]
    q_blk, q_off = np.divmod(np.arange(MOBA_Q)[None, :], KV_BLOCK)
    causal = np.stack([np.where((q_blk > g) | ((q_blk == g) & (key <= q_off)), 0.0, -np.inf)
                       for g in range(MOBA_G)])
    slope_tab = slopes[:, None, None] * np.ones((1, SUBLANES, MOBA_Q))
    return (jnp.asarray(kpos), jnp.asarray(qaug, _BF16),
            jnp.asarray(slope_tab.astype(np.float32)), jnp.asarray(causal.astype(np.float32)))


def kernel(x, norm_mix_g, w_in, b_gate, conv_w, w_proj_moba, w_proj_sb, w_proj_conv, w_out,
           norm_ffn_g, w_ffn_gate, w_ffn_up, w_ffn_down, norm_final_g):
    depth = w_in.shape[0]
    b, s, d = x.shape
    scale = HEAD_DIM ** -0.5
    perm = _sb_permutation()
    kpos, qaug, slopes, causal = _moba_tables()
    col_scale = np.ones((w_in.shape[2],), np.float32)
    col_scale[0:WIDTH] = scale * LOG2E
    col_scale[3 * WIDTH:4 * WIDTH] = scale * LOG2E
    col_scale[9 * WIDTH:] = -LOG2E
    w_all = (w_in * col_scale).astype(_BF16)
    bf16 = lambda p: p.astype(_BF16)
    wpa, wpb, wpc, wout = bf16(w_proj_moba), bf16(w_proj_sb), bf16(w_proj_conv), bf16(w_out)
    wg, wu, wd = bf16(w_ffn_gate), bf16(w_ffn_up), bf16(w_ffn_down)
    g_mix, g_ffn, bg = norm_mix_g[:, None, :], norm_ffn_g[:, None, :], b_gate[:, None, :] * -LOG2E
    for l in range(depth):
        qa_t, ka, va_t, kmean, kamax, qs_t, ks, vs_t = _in_proj(l, x, g_mix, perm, kpos, w_all)
        oa = _moba(qa_t, qaug, ka, va_t, kmean, kamax, slopes, causal)
        ob = _stick_breaking(qs_t, ks, vs_t)
        x = _mix(l, oa, ob, x, g_mix, w_all, conv_w, bg, wpa, wpb, wpc, wout)
        x = _ffn(l, x.reshape(b * s, d), g_ffn, wg, wu, wd, norm_final_g[None, :],
                 final_norm=(l == depth - 1)).reshape(b, s, d)
    return x
```

```python
import functools

import jax
import jax.numpy as jnp
import numpy as np
from jax import lax
from jax.experimental import pallas as pl
from jax.experimental.pallas import tpu as pltpu

D_MODEL = 1024
HEAD_DIM = 64
N_HEADS = 8
WIDTH = N_HEADS * HEAD_DIM
CONV_K = 3
KV_BLOCK = 256
IN_BLOCKS = 4
MOBA_TOPK = 3
MOBA_G = 2
MOBA_Q = MOBA_G * KV_BLOCK
SB_Q = KV_BLOCK
SB_TILES = 4
MOBA_SKIP = -150.0
D_FF = 2816
RMS_EPS = 1e-6
PAIR = 2 * HEAD_DIM
V_ROWS = HEAD_DIM + 16
N_AUG = 4
SUBLANES = 8
LOG2E = 1.4426950408889634
RUN = KV_BLOCK // SUBLANES
SB_STOP = 0.0
VMEM_LIMIT = 56 * 1024 * 1024

_TN = (((0,), (0,)), ((), ()))
_F32 = jnp.float32
_BF16 = jnp.bfloat16


def _params(*sem):
    return pltpu.CompilerParams(dimension_semantics=sem, vmem_limit_bytes=VMEM_LIMIT)


def _const_spec(shape):
    zeros = (0,) * len(shape)
    return pl.BlockSpec(shape, lambda *_: zeros)


def _layer_spec(layer, shape):
    index = (layer,) + (0,) * len(shape)
    return pl.BlockSpec((1,) + tuple(shape), lambda *_: index)


def _rms(x, g):
    y = x * lax.rsqrt(jnp.mean(x * x, axis=-1, keepdims=True) + RMS_EPS)
    return y * g


def _in_proj_kernel(x_ref, g_ref, perm_ref, kpos_ref, w_ref,
                    qa_ref, ka_ref, va_ref, kmean_ref, kamax_ref, qs_ref, ks_ref, vs_ref):
    i = pl.program_id(1)
    h = _rms(x_ref[0], g_ref[0]).astype(_BF16)
    blocks = [slice(n * KV_BLOCK, (n + 1) * KV_BLOCK) for n in range(IN_BLOCKS)]

    def proj(lhs, k):
        return jnp.dot(lhs, w_ref[0, :, k * WIDTH:(k + 1) * WIDTH], preferred_element_type=_F32)

    qa_ref[0] = proj(h, 0).T.astype(_BF16)
    ka = proj(h, 1)
    lane = lax.broadcasted_iota(jnp.int32, (ka.shape[0], PAIR), 1)
    tiles = []
    for pair in range(N_HEADS // 2):
        both = ka[:, pair * PAIR:(pair + 1) * PAIR]
        tiles += [jnp.where(lane < HEAD_DIM, both, 0.0),
                  jnp.where(lane < HEAD_DIM, pltpu.roll(both, HEAD_DIM, 1), 0.0)]
    ka_wide = jnp.concatenate(tiles, axis=1)
    va_t = proj(h, 2).T.astype(_BF16)
    extra = (lax.broadcasted_iota(jnp.int32, (V_ROWS - HEAD_DIM, KV_BLOCK), 0) == 0)
    extra = extra.astype(_F32).astype(_BF16)
    qs_ref[0] = proj(h, 3).T.astype(_BF16)
    hp = jnp.concatenate(
        [jnp.dot(perm_ref[...], h[rows, :], preferred_element_type=_F32).astype(_BF16)
         for rows in blocks], axis=0)
    ks = proj(hp, 4).astype(_BF16)
    vs_t = proj(hp, 5).T.astype(_BF16)
    for n, rows in enumerate(blocks):
        blk_k = ka_wide[rows, :]
        ka_ref[0, n] = (blk_k + kpos_ref[...]).astype(_BF16)
        kmean_ref[0, pl.ds(i * IN_BLOCKS + n, 1), :] = jnp.mean(blk_k, axis=0, keepdims=True)
        kamax_ref[0, pl.ds(i * IN_BLOCKS + n, 1), :] = jnp.max(
            jnp.abs(blk_k.astype(_BF16).astype(_F32)), axis=0, keepdims=True)
        va_ref[0, n] = jnp.concatenate(
            [piece for hd in range(N_HEADS)
             for piece in (va_t[hd * HEAD_DIM:(hd + 1) * HEAD_DIM, rows], extra)], axis=0)
        ks_ref[0, n] = ks[rows, :]
        vs_ref[0, n] = vs_t[:, rows]


def _in_proj(layer, x, g, perm, kpos, w_all):
    b, s, d = x.shape
    nb = s // KV_BLOCK
    kb = KV_BLOCK
    t = IN_BLOCKS * kb
    kw = N_HEADS * PAIR
    out_shape = (
        jax.ShapeDtypeStruct((b, WIDTH, s), _BF16),
        jax.ShapeDtypeStruct((b, nb, kb, kw), _BF16),
        jax.ShapeDtypeStruct((b, nb, N_HEADS * V_ROWS, kb), _BF16),
        jax.ShapeDtypeStruct((b, nb, kw), _F32),
        jax.ShapeDtypeStruct((b, nb, kw), _F32),
        jax.ShapeDtypeStruct((b, WIDTH, s), _BF16),
        jax.ShapeDtypeStruct((b, nb, kb, WIDTH), _BF16),
        jax.ShapeDtypeStruct((b, nb, WIDTH, kb), _BF16),
    )
    qt_spec = pl.BlockSpec((1, WIDTH, t), lambda bi, i: (bi, 0, i))
    per_block = lambda *shape: pl.BlockSpec((1, IN_BLOCKS) + shape, lambda bi, i: (bi, i, 0, 0))
    return pl.pallas_call(
        _in_proj_kernel,
        grid=(b, nb // IN_BLOCKS),
        in_specs=[
            pl.BlockSpec((1, t, d), lambda bi, i: (bi, i, 0)),
            _layer_spec(layer, (1, d)),
            _const_spec((kb, kb)),
            _const_spec((kb, kw)),
            _layer_spec(layer, (d, 6 * WIDTH)),
        ],
        out_specs=(
            qt_spec,
            per_block(kb, kw),
            per_block(N_HEADS * V_ROWS, kb),
            pl.BlockSpec((1, nb, kw), lambda bi, i: (bi, 0, 0)),
            pl.BlockSpec((1, nb, kw), lambda bi, i: (bi, 0, 0)),
            qt_spec, per_block(kb, WIDTH), per_block(WIDTH, kb),
        ),
        out_shape=out_shape,
        compiler_params=_params("arbitrary", "arbitrary"),
        name="in_proj",
    )(x, g, perm, kpos, w_all)


def _head_rows(q_pair, hh):
    row = lax.broadcasted_iota(jnp.int32, q_pair.shape, 0)
    keep = (row >= hh * HEAD_DIM) & (row < (hh + 1) * HEAD_DIM)
    return jnp.where(keep, q_pair, jnp.zeros_like(q_pair))


def _moba_kernel(q_ref, qaug_ref, k_ref, v_ref, kmean_ref, kamax_ref, slope_ref, causal_ref,
                 o_ref, sel_ref, sa_ref, sb_ref, acc_ref):
    last = pl.program_id(2)
    nb = k_ref.shape[1]
    nidx = lax.broadcasted_iota(jnp.int32, (nb, MOBA_Q), 0)
    blk = last * MOBA_G + (lax.broadcasted_iota(jnp.int32, (nb, MOBA_Q), 1)
                           >> (KV_BLOCK.bit_length() - 1))
    blk_row = blk[0:1, :]
    neg_inf = jnp.float32(-jnp.inf)
    slope = [slope_ref[hh, 0:1, :] for hh in range(2)]
    lanes = [slice(hh * PAIR, (hh + 1) * PAIR) for hh in range(2)]
    vrows = [slice(hh * V_ROWS, (hh + 1) * V_ROWS) for hh in range(2)]

    heads, reach = [], []
    for hh in range(2):
        qm = jnp.concatenate([q_ref[0, hh * HEAD_DIM:(hh + 1) * HEAD_DIM, :], qaug_ref[hh]], axis=0)
        km = kmean_ref[0, :, lanes[hh]].astype(_BF16)
        kamax = kamax_ref[0, :, lanes[hh]].astype(_BF16)
        bound = jnp.dot(kamax, jnp.abs(qm), preferred_element_type=_F32)
        reach.append(bound + slope[hh] * ((nidx + 1 - blk) * KV_BLOCK).astype(_F32))
        gate = jnp.dot(km, qm, preferred_element_type=_F32)
        g = jnp.where(nidx < blk, gate, neg_inf)
        sel = nidx == blk
        for r in range(MOBA_TOPK):
            mx = jnp.max(g, axis=0, keepdims=True)
            first = jnp.min(jnp.where(g == mx, nidx, nb), axis=0, keepdims=True)
            hit = nidx == first
            sel = sel | (hit & (jnp.full((nb, MOBA_Q), r, jnp.int32) < blk))
            g = jnp.where(hit, neg_inf, g)
        sel_ref[hh] = jnp.where(sel, 0.0, neg_inf)
        heads.append(qm)

    def shift(j):
        return ((j - blk_row) * KV_BLOCK).astype(_F32)

    def row_bias(hh, j):
        return sel_ref[hh, pl.ds(j, 1), :] + slope[hh] * shift(j)

    def scores(t, dst_ref, nearest=False):
        mx = []
        for hh in range(2):
            col = jnp.full((1, MOBA_Q), neg_inf, _F32)
            for g in range(MOBA_G):
                j = t * MOBA_G + g
                s = jnp.dot(k_ref[0, j, :, lanes[hh]], heads[hh], preferred_element_type=_F32)
                if nearest:
                    s = s + causal_ref[g]
                dst_ref[hh, g * KV_BLOCK:(g + 1) * KV_BLOCK, :] = s
                col = jnp.maximum(col, jnp.max(s, axis=0, keepdims=True) + row_bias(hh, j))
            mx.append(col)
        return mx

    def accumulate(t, src_ref, mx, st):
        out = []
        for hh in range(2):
            m_new = jnp.maximum(st[hh], mx[hh])
            acc = jnp.exp2(st[hh] - m_new) * acc_ref[hh]
            for g in range(MOBA_G):
                j = t * MOBA_G + g
                s = src_ref[hh, g * KV_BLOCK:(g + 1) * KV_BLOCK, :]
                p = jnp.exp2(s - (m_new - row_bias(hh, j)))
                acc = acc + jnp.dot(v_ref[0, j, vrows[hh], :], p.astype(_BF16),
                                    preferred_element_type=_F32)
            acc_ref[hh] = acc
            out.append(m_new)
        return out

    state = [jnp.full((1, MOBA_Q), jnp.finfo(_F32).min, _F32) for _ in range(2)]
    for hh in range(2):
        acc_ref[hh] = jnp.zeros((V_ROWS, MOBA_Q), _F32)

    mx_a = scores(last, sa_ref, nearest=True)
    mx_b = scores(jnp.maximum(last - 1, 0), sb_ref)
    first_needed = jnp.int32(nb)
    for hh in range(2):
        needed = (nidx < blk) & (reach[hh] - mx_a[hh] > MOBA_SKIP)
        first_needed = jnp.minimum(first_needed, jnp.min(jnp.where(needed, nidx, nb)))
    first = jnp.minimum(first_needed // MOBA_G, last)
    ntiles = last - first + 1
    state = accumulate(last, sa_ref, mx_a, state)

    def body(k, carry):
        st, mx_b = carry[:2], carry[2:]
        t = last - 1 - 2 * k
        mx_a = scores(t - 1, sa_ref)
        st = accumulate(t, sb_ref, mx_b, st)
        mx_b = scores(jnp.maximum(t - 2, 0), sb_ref)
        st = accumulate(t - 1, sa_ref, mx_a, st)
        return tuple(st) + tuple(mx_b)

    carry = lax.fori_loop(0, (ntiles - 1) // 2, body, tuple(state) + tuple(mx_b))

    @pl.when(ntiles % 2 == 0)
    def _():
        accumulate(first, sb_ref, carry[2:], carry[:2])

    o_ref[0] = jnp.concatenate(
        [acc_ref[hh, 0:HEAD_DIM, :] / acc_ref[hh, HEAD_DIM:HEAD_DIM + 1, :] for hh in range(2)],
        axis=0).astype(_BF16)


def _moba(qa_t, qaug, ka, va_t, kmean, kamax, slopes, causal):
    b, _, s = qa_t.shape
    nb = s // KV_BLOCK
    npair = N_HEADS // 2
    assert nb % MOBA_G == 0
    return pl.pallas_call(
        _moba_kernel,
        grid=(b, npair, s // MOBA_Q),
        in_specs=[
            pl.BlockSpec((1, PAIR, MOBA_Q), lambda bi, hp, c: (bi, hp, c)),
            pl.BlockSpec((2, HEAD_DIM, MOBA_Q), lambda bi, hp, c: (hp, 0, 0)),
            pl.BlockSpec((1, nb, KV_BLOCK, 2 * PAIR), lambda bi, hp, c: (bi, 0, 0, hp)),
            pl.BlockSpec((1, nb, 2 * V_ROWS, KV_BLOCK), lambda bi, hp, c: (bi, 0, hp, 0)),
            pl.BlockSpec((1, nb, 2 * PAIR), lambda bi, hp, c: (bi, 0, hp)),
            pl.BlockSpec((1, nb, 2 * PAIR), lambda bi, hp, c: (bi, 0, hp)),
            pl.BlockSpec((2, SUBLANES, MOBA_Q), lambda bi, hp, c: (hp, 0, 0)),
            _const_spec((MOBA_G, KV_BLOCK, MOBA_Q)),
        ],
        out_specs=pl.BlockSpec((1, PAIR, MOBA_Q), lambda bi, hp, c: (bi, hp, c)),
        out_shape=jax.ShapeDtypeStruct((b, WIDTH, s), _BF16),
        scratch_shapes=[pltpu.VMEM((2, nb, MOBA_Q), _F32),
                        pltpu.VMEM((2, MOBA_G * KV_BLOCK, MOBA_Q), _F32),
                        pltpu.VMEM((2, MOBA_G * KV_BLOCK, MOBA_Q), _F32),
                        pltpu.VMEM((2, V_ROWS, MOBA_Q), _F32)],
        compiler_params=_params("arbitrary", "arbitrary", "arbitrary"),
        name="moba",
    )(qa_t, qaug, ka, va_t, kmean, kamax, slopes, causal)


def _suffix_product_over_sublanes(x):
    sub = lax.broadcasted_iota(jnp.int32, x.shape, 0)
    y = x
    for d in (1, 2, 4):
        up = pltpu.roll(y, SUBLANES - d, 0)
        y = y * jnp.where(sub < SUBLANES - d, up, 1.0)
    return y


def _sb_scores(k_j, qm, mask):
    z = jnp.dot(k_j, qm, preferred_element_type=_F32)
    keep = 1.0 / (1.0 + jnp.exp2(z))
    if mask is not None:
        keep = jnp.where(mask, keep, 1.0)
    run = jnp.ones((SUBLANES, z.shape[1]), _F32)
    diff = [None] * RUN
    for r in reversed(range(RUN)):
        nxt = run * keep[r * SUBLANES:(r + 1) * SUBLANES, :]
        diff[r] = run - nxt
        run = nxt
    sub = lax.broadcasted_iota(jnp.int32, run.shape, 0)
    shifted = jnp.where(sub < SUBLANES - 1, pltpu.roll(run, SUBLANES - 1, 0), 1.0)
    return diff, run, _suffix_product_over_sublanes(shifted)


def _sb_weights(scores, v_jh, carry, acc):
    diff, run, later = scores
    base = later * carry
    w = jnp.concatenate([d * base for d in diff], axis=0)
    acc = acc + jnp.dot(v_jh, w.astype(_BF16), preferred_element_type=_F32)
    return carry * (later[0:1, :] * run[0:1, :]), acc


def _sb_kernel(q_ref, k_ref, v_ref, o_ref):
    rho = lax.broadcasted_iota(jnp.int32, (KV_BLOCK, SB_Q), 0)
    kpos = (rho & (SUBLANES - 1)) * RUN + (rho >> (SUBLANES.bit_length() - 1))
    qpos = lax.broadcasted_iota(jnp.int32, (KV_BLOCK, SB_Q), 1)
    strict = kpos < qpos
    rows = [slice(hh * HEAD_DIM, (hh + 1) * HEAD_DIM) for hh in range(2)]

    jds = [pl.program_id(2) * SB_TILES + u for u in range(SB_TILES)]
    jps = [jnp.maximum(jd - 1, 0) for jd in jds]
    heads = [[_head_rows(q_ref[0, :, u * SB_Q:(u + 1) * SB_Q], hh) for hh in range(2)]
             for u in range(SB_TILES)]
    diags = [[_sb_scores(k_ref[0, jds[u]], heads[u][hh], strict) for hh in range(2)]
             for u in range(SB_TILES)]
    prevs = [[_sb_scores(k_ref[0, jps[u]], heads[u][hh], None) for hh in range(2)]
             for u in range(SB_TILES)]
    states = []
    for u in range(SB_TILES):
        state = []
        for hh in range(2):
            carry, acc = _sb_weights(diags[u][hh], v_ref[0, jds[u], rows[hh], :],
                                     jnp.ones((1, SB_Q), _F32),
                                     jnp.zeros((HEAD_DIM, SB_Q), _F32))
            carry = jnp.where(jds[u] > 0, carry, 0.0)
            state += list(_sb_weights(prevs[u][hh], v_ref[0, jps[u], rows[hh], :], carry, acc))
        states.append(state)

    def more(trip, sts):
        flags = [jnp.logical_and(jds[u] - 2 - trip >= 0,
                                 jnp.maximum(jnp.max(sts[u][0]), jnp.max(sts[u][2])) > SB_STOP)
                 for u in range(SB_TILES)]
        return functools.reduce(jnp.logical_or, flags).astype(jnp.int32)

    def body(loop):
        trip, flat = loop[0], loop[2:]
        new = []
        for u in range(SB_TILES):
            j = jds[u] - 2 - trip
            jc = jnp.maximum(j, 0)
            out = []
            for hh in range(2):
                carry, acc = flat[4 * u + 2 * hh:4 * u + 2 * hh + 2]
                carry = jnp.where(j >= 0, carry, 0.0)
                out += list(_sb_weights(_sb_scores(k_ref[0, jc], heads[u][hh], None),
                                        v_ref[0, jc, rows[hh], :], carry, acc))
            new.append(out)
        return (trip + 1, more(trip + 1, new), *[a for st in new for a in st])

    zero = jnp.int32(0)
    final = lax.while_loop(lambda loop: loop[1] > 0, body,
                           (zero, more(zero, states), *[a for st in states for a in st]))[2:]
    for u in range(SB_TILES):
        o_ref[0, :, u * SB_Q:(u + 1) * SB_Q] = jnp.concatenate(
            [final[4 * u + 1], final[4 * u + 3]], axis=0).astype(_BF16)


def _stick_breaking(qs_t, ks, vs_t):
    b, _, s = qs_t.shape
    nb = s // KV_BLOCK
    npair = N_HEADS // 2
    return pl.pallas_call(
        _sb_kernel,
        grid=(b, npair, s // (SB_Q * SB_TILES)),
        in_specs=[
            pl.BlockSpec((1, PAIR, SB_Q * SB_TILES), lambda bi, hp, i: (bi, hp, i)),
            pl.BlockSpec((1, nb, KV_BLOCK, PAIR), lambda bi, hp, i: (bi, 0, 0, hp)),
            pl.BlockSpec((1, nb, PAIR, KV_BLOCK), lambda bi, hp, i: (bi, 0, hp, 0)),
        ],
        out_specs=pl.BlockSpec((1, PAIR, SB_Q * SB_TILES), lambda bi, hp, i: (bi, hp, i)),
        out_shape=jax.ShapeDtypeStruct((b, WIDTH, s), _BF16),
        compiler_params=_params("arbitrary", "arbitrary", "arbitrary"),
        name="stick_breaking",
    )(qs_t, ks, vs_t)


MIX_T = 1024
MIX_CHUNK = 256
HALO = SUBLANES
GATE_SPLIT = 3 * WIDTH


def _mix_kernel(oa_ref, ob_ref, halo_ref, x_ref, g_ref, wconv_ref, wg0_ref, wg1_ref, convw_ref,
                bg_ref, wpa_ref, wpb_ref, wpc_ref, wout_ref, o_ref, merged_ref):
    i = pl.program_id(1)
    h = _rms(x_ref[0], g_ref[0]).astype(_BF16)
    r = jnp.dot(h, wconv_ref[0], preferred_element_type=_F32)
    xc, bc, cc = r[:, 0:WIDTH], r[:, WIDTH:2 * WIDTH], r[:, 2 * WIDTH:3 * WIDTH]
    u = cc * xc
    hl = jnp.dot(_rms(halo_ref[0], g_ref[0]).astype(_BF16), wconv_ref[0],
                 preferred_element_type=_F32)
    uh = hl[:, 2 * WIDTH:3 * WIDTH] * hl[:, 0:WIDTH]
    uh = jnp.where(i > 0, uh, 0.0)
    prev1 = uh[HALO - 1:HALO, :]
    prev2 = uh[HALO - 2:HALO - 1, :]
    row = lax.broadcasted_iota(jnp.int32, u.shape, 0)
    u1 = jnp.where(row == 0, prev1, pltpu.roll(u, 1, 0))
    u2 = jnp.where(row == 0, prev2, jnp.where(row == 1, prev1, pltpu.roll(u, 2, 0)))
    w = convw_ref[0]
    y = (bc * (u2 * w[0:1, :] + u1 * w[1:2, :] + u * w[2:3, :])).astype(_BF16)
    oa, ob = oa_ref[0], ob_ref[0]
    d = D_MODEL
    for c in range(d // MIX_CHUNK):
        cols = slice(c * MIX_CHUNK, (c + 1) * MIX_CHUNK)

        def gate(k):
            lo = k * d + c * MIX_CHUNK
            w_ref, at = (wg0_ref, lo) if lo < GATE_SPLIT else (wg1_ref, lo - GATE_SPLIT)
            pre = jnp.dot(h, w_ref[0, :, at:at + MIX_CHUNK], preferred_element_type=_F32)
            return 1.0 / (1.0 + jnp.exp2(pre + bg_ref[0, :, lo:lo + MIX_CHUNK]))

        merged_ref[:, cols] = (
            gate(0) * lax.dot_general(oa, wpa_ref[0, :, cols], _TN, preferred_element_type=_F32)
            + gate(1) * lax.dot_general(ob, wpb_ref[0, :, cols], _TN,
                                        preferred_element_type=_F32)
            + gate(2) * jnp.dot(y, wpc_ref[0, :, cols], preferred_element_type=_F32)
        ).astype(_BF16)
    o_ref[0] = x_ref[0] + jnp.dot(merged_ref[...], wout_ref[0], preferred_element_type=_F32)


def _mix(layer, oa, ob, x, g, w_all, convw, bg, wpa, wpb, wpc, wout):
    b, s, d = x.shape
    t = MIX_T
    tile = lambda width: pl.BlockSpec((1, t, width), lambda bi, i: (bi, i, 0))
    tile_t = pl.BlockSpec((1, WIDTH, t), lambda bi, i: (bi, 0, i))
    halo = pl.BlockSpec((1, HALO, d), lambda bi, i: (bi, jnp.maximum(i * (t // HALO) - 1, 0), 0))
    w_block = lambda n: pl.BlockSpec((1, d, GATE_SPLIT), lambda *_: (layer, 0, n),
                                     pipeline_mode=pl.Buffered(1))
    assert 6 * WIDTH == 2 * GATE_SPLIT and 3 * d == 2 * GATE_SPLIT
    return pl.pallas_call(
        _mix_kernel,
        grid=(b, s // t),
        in_specs=[tile_t, tile_t, halo, tile(d), _layer_spec(layer, (1, d)),
                  w_block(2), w_block(3), w_block(4),
                  _layer_spec(layer, (CONV_K, WIDTH)), _layer_spec(layer, (1, 3 * d)),
                  _layer_spec(layer, (WIDTH, d)), _layer_spec(layer, (WIDTH, d)),
                  _layer_spec(layer, (WIDTH, d)), _layer_spec(layer, (d, d))],
        out_specs=tile(d),
        out_shape=jax.ShapeDtypeStruct((b, s, d), _F32),
        scratch_shapes=[pltpu.VMEM((t, d), _BF16)],
        compiler_params=_params("arbitrary", "arbitrary"),
        name="mix",
    )(oa, ob, x, x, g, w_all, w_all, w_all, convw, bg, wpa, wpb, wpc, wout)


FFN_T = 1024
FF_CHUNK = 256


def _ffn_up_kernel(x_ref, g_ref, wg_ref, wu_ref, o_ref):
    h = _rms(x_ref[...], g_ref[0]).astype(_BF16)
    for n in range(D_FF // FF_CHUNK):
        cols = slice(n * FF_CHUNK, (n + 1) * FF_CHUNK)
        a = jnp.dot(h, wg_ref[0, :, cols], preferred_element_type=_F32)
        bgate = jnp.dot(h, wu_ref[0, :, cols], preferred_element_type=_F32)
        o_ref[:, cols] = (a * jax.nn.sigmoid(a) * bgate).astype(_BF16)


def _ffn_down_kernel(a_ref, x_ref, wd_ref, g_ref, o_ref, *, final_norm):
    y = x_ref[...] + jnp.dot(a_ref[...], wd_ref[0], preferred_element_type=_F32)
    if final_norm:
        y = _rms(y, g_ref[...])
    o_ref[...] = y


def _ffn(layer, x2d, g, wg, wu, wd, gfinal, final_norm):
    n, d = x2d.shape
    t = FFN_T
    act = pl.pallas_call(
        _ffn_up_kernel,
        grid=(n // t,),
        in_specs=[pl.BlockSpec((t, d), lambda i: (i, 0)), _layer_spec(layer, (1, d)),
                  _layer_spec(layer, (d, D_FF)), _layer_spec(layer, (d, D_FF))],
        out_specs=pl.BlockSpec((t, D_FF), lambda i: (i, 0)),
        out_shape=jax.ShapeDtypeStruct((n, D_FF), _BF16),
        compiler_params=_params("arbitrary"),
        name="ffn_up",
    )(x2d, g, wg, wu)
    return pl.pallas_call(
        functools.partial(_ffn_down_kernel, final_norm=final_norm),
        grid=(n // t,),
        in_specs=[pl.BlockSpec((t, D_FF), lambda i: (i, 0)),
                  pl.BlockSpec((t, d), lambda i: (i, 0)),
                  _layer_spec(layer, (D_FF, d)), _const_spec((1, d))],
        out_specs=pl.BlockSpec((t, d), lambda i: (i, 0)),
        out_shape=jax.ShapeDtypeStruct((n, d), _F32),
        compiler_params=_params("arbitrary"),
        name="ffn_down",
    )(act, x2d, wd, gfinal)


def _sb_permutation():
    rho = np.arange(KV_BLOCK)
    kappa = (rho % SUBLANES) * RUN + rho // SUBLANES
    p = np.zeros((KV_BLOCK, KV_BLOCK), np.float32)
    p[rho, kappa] = 1.0
    return jnp.asarray(p, _BF16)


def _moba_tables():
    slopes = np.exp2(-8.0 * (np.arange(N_HEADS, dtype=np.float64) + 1.0) / N_HEADS) * LOG2E
    kpos = np.zeros((KV_BLOCK, N_HEADS * PAIR), np.float32)
    qaug = np.zeros((N_HEADS, HEAD_DIM, MOBA_Q), np.float32)
    for hd in range(N_HEADS):
        rest = np.float64(slopes[hd])
        for a in range(N_AUG):
            piece = np.float64(np.float32(rest).astype(_BF16))
            kpos[:, hd * PAIR + HEAD_DIM + a] = np.arange(KV_BLOCK)
            qaug[hd, a, :] = piece
            rest = rest - piece
    key = np.arange(KV_BLOCK)[:, None]
    q_blk, q_off = np.divmod(np.arange(MOBA_Q)[None, :], KV_BLOCK)
    causal = np.stack([np.where((q_blk > g) | ((q_blk == g) & (key <= q_off)), 0.0, -np.inf)
                       for g in range(MOBA_G)])
    slope_tab = slopes[:, None, None] * np.ones((1, SUBLANES, MOBA_Q))
    return (jnp.asarray(kpos), jnp.asarray(qaug, _BF16),
            jnp.asarray(slope_tab.astype(np.float32)), jnp.asarray(causal.astype(np.float32)))


def kernel(x, norm_mix_g, w_in, b_gate, conv_w, w_proj_moba, w_proj_sb, w_proj_conv, w_out,
           norm_ffn_g, w_ffn_gate, w_ffn_up, w_ffn_down, norm_final_g):
    depth = w_in.shape[0]
    b, s, d = x.shape
    scale = HEAD_DIM ** -0.5
    perm = _sb_permutation()
    kpos, qaug, slopes, causal = _moba_tables()
    col_scale = np.ones((w_in.shape[2],), np.float32)
    col_scale[0:WIDTH] = scale * LOG2E
    col_scale[3 * WIDTH:4 * WIDTH] = scale * LOG2E
    col_scale[9 * WIDTH:] = -LOG2E
    w_all = (w_in * col_scale).astype(_BF16)
    bf16 = lambda p: p.astype(_BF16)
    wpa, wpb, wpc, wout = bf16(w_proj_moba), bf16(w_proj_sb), bf16(w_proj_conv), bf16(w_out)
    wg, wu, wd = bf16(w_ffn_gate), bf16(w_ffn_up), bf16(w_ffn_down)
    g_mix, g_ffn, bg = norm_mix_g[:, None, :], norm_ffn_g[:, None, :], b_gate[:, None, :] * -LOG2E
    for l in range(depth):
        qa_t, ka, va_t, kmean, kamax, qs_t, ks, vs_t = _in_proj(l, x, g_mix, perm, kpos, w_all)
        oa = _moba(qa_t, qaug, ka, va_t, kmean, kamax, slopes, causal)
        ob = _stick_breaking(qs_t, ks, vs_t)
        x = _mix(l, oa, ob, x, g_mix, w_all, conv_w, bg, wpa, wpb, wpc, wout)
        x = _ffn(l, x.reshape(b * s, d), g_ffn, wg, wu, wd, norm_final_g[None, :],
                 final_norm=(l == depth - 1)).reshape(b, s, d)
    return x
```

```python
import functools

import jax
import jax.numpy as jnp
import numpy as np
from jax import lax
from jax.experimental import pallas as pl
from jax.experimental.pallas import tpu as pltpu

D_MODEL = 1024
HEAD_DIM = 64
N_HEADS = 8
WIDTH = N_HEADS * HEAD_DIM
CONV_K = 3
KV_BLOCK = 256
IN_BLOCKS = 4
MOBA_TOPK = 3
MOBA_G = 2
MOBA_Q = MOBA_G * KV_BLOCK
SB_Q = KV_BLOCK
SB_TILES = 4
MOBA_SKIP = -150.0
D_FF = 2816
RMS_EPS = 1e-6
PAIR = 2 * HEAD_DIM
BF16_SUBLANES = 16
V_ROWS = HEAD_DIM + BF16_SUBLANES
N_AUG = 4
SUBLANES = 8
LOG2E = 1.4426950408889634
RUN = KV_BLOCK // SUBLANES
SB_STOP = 0.0
VMEM_LIMIT = 56 * 1024 * 1024

_TN = (((0,), (0,)), ((), ()))
_F32 = jnp.float32
_BF16 = jnp.bfloat16


def _params(*sem):
    return pltpu.CompilerParams(dimension_semantics=sem, vmem_limit_bytes=VMEM_LIMIT)


def _const_spec(shape):
    zeros = (0,) * len(shape)
    return pl.BlockSpec(shape, lambda *_: zeros)


def _layer_spec(layer, shape):
    index = (layer,) + (0,) * len(shape)
    return pl.BlockSpec((1,) + tuple(shape), lambda *_: index)


def _rms(x, g):
    y = x * lax.rsqrt(jnp.mean(x * x, axis=-1, keepdims=True) + RMS_EPS)
    return y * g


def _in_proj_kernel(x_ref, g_ref, perm_ref, kpos_ref, w_ref,
                    qa_ref, ka_ref, va_ref, kmean_ref, kamax_ref, qs_ref, ks_ref, vs_ref):
    i = pl.program_id(1)
    h = _rms(x_ref[0], g_ref[0]).astype(_BF16)
    blocks = [slice(n * KV_BLOCK, (n + 1) * KV_BLOCK) for n in range(IN_BLOCKS)]

    def proj(lhs, k):
        return jnp.dot(lhs, w_ref[0, :, k * WIDTH:(k + 1) * WIDTH], preferred_element_type=_F32)

    qa_ref[0] = proj(h, 0).T.astype(_BF16)
    ka = proj(h, 1)
    lane = lax.broadcasted_iota(jnp.int32, (ka.shape[0], PAIR), 1)
    tiles = []
    for pair in range(N_HEADS // 2):
        both = ka[:, pair * PAIR:(pair + 1) * PAIR]
        tiles += [jnp.where(lane < HEAD_DIM, both, 0.0),
                  jnp.where(lane < HEAD_DIM, pltpu.roll(both, HEAD_DIM, 1), 0.0)]
    ka_wide = jnp.concatenate(tiles, axis=1)
    va_t = proj(h, 2).T.astype(_BF16)
    extra = (lax.broadcasted_iota(jnp.int32, (V_ROWS - HEAD_DIM, KV_BLOCK), 0) == 0)
    extra = extra.astype(_F32).astype(_BF16)
    qs_ref[0] = proj(h, 3).T.astype(_BF16)
    hp = jnp.concatenate(
        [jnp.dot(perm_ref[...], h[rows, :], preferred_element_type=_F32).astype(_BF16)
         for rows in blocks], axis=0)
    ks = proj(hp, 4).astype(_BF16)
    vs_t = proj(hp, 5).T.astype(_BF16)
    for n, rows in enumerate(blocks):
        blk_k = ka_wide[rows, :]
        ka_ref[0, n] = (blk_k + kpos_ref[...]).astype(_BF16)
        kmean_ref[0, pl.ds(i * IN_BLOCKS + n, 1), :] = jnp.mean(blk_k, axis=0, keepdims=True)
        kamax_ref[0, pl.ds(i * IN_BLOCKS + n, 1), :] = jnp.max(
            jnp.abs(blk_k.astype(_BF16).astype(_F32)), axis=0, keepdims=True)
        va_ref[0, n] = jnp.concatenate(
            [piece for hd in range(N_HEADS)
             for piece in (va_t[hd * HEAD_DIM:(hd + 1) * HEAD_DIM, rows], extra)], axis=0)
        ks_ref[0, n] = ks[rows, :]
        vs_ref[0, n] = vs_t[:, rows]


def _in_proj(layer, x, g, perm, kpos, w_all):
    b, s, d = x.shape
    nb = s // KV_BLOCK
    kb = KV_BLOCK
    t = IN_BLOCKS * kb
    kw = N_HEADS * PAIR
    out_shape = (
        jax.ShapeDtypeStruct((b, WIDTH, s), _BF16),
        jax.ShapeDtypeStruct((b, nb, kb, kw), _BF16),
        jax.ShapeDtypeStruct((b, nb, N_HEADS * V_ROWS, kb), _BF16),
        jax.ShapeDtypeStruct((b, nb, kw), _F32),
        jax.ShapeDtypeStruct((b, nb, kw), _F32),
        jax.ShapeDtypeStruct((b, WIDTH, s), _BF16),
        jax.ShapeDtypeStruct((b, nb, kb, WIDTH), _BF16),
        jax.ShapeDtypeStruct((b, nb, WIDTH, kb), _BF16),
    )
    qt_spec = pl.BlockSpec((1, WIDTH, t), lambda bi, i: (bi, 0, i))
    per_block = lambda *shape: pl.BlockSpec((1, IN_BLOCKS) + shape, lambda bi, i: (bi, i, 0, 0))
    return pl.pallas_call(
        _in_proj_kernel,
        grid=(b, nb // IN_BLOCKS),
        in_specs=[
            pl.BlockSpec((1, t, d), lambda bi, i: (bi, i, 0)),
            _layer_spec(layer, (1, d)),
            _const_spec((kb, kb)),
            _const_spec((kb, kw)),
            _layer_spec(layer, (d, 6 * WIDTH)),
        ],
        out_specs=(
            qt_spec,
            per_block(kb, kw),
            per_block(N_HEADS * V_ROWS, kb),
            pl.BlockSpec((1, nb, kw), lambda bi, i: (bi, 0, 0)),
            pl.BlockSpec((1, nb, kw), lambda bi, i: (bi, 0, 0)),
            qt_spec, per_block(kb, WIDTH), per_block(WIDTH, kb),
        ),
        out_shape=out_shape,
        compiler_params=_params("arbitrary", "arbitrary"),
        name="in_proj",
    )(x, g, perm, kpos, w_all)


def _head_rows(q_pair, hh):
    row = lax.broadcasted_iota(jnp.int32, q_pair.shape, 0)
    keep = (row >= hh * HEAD_DIM) & (row < (hh + 1) * HEAD_DIM)
    return jnp.where(keep, q_pair, jnp.zeros_like(q_pair))


def _moba_kernel(q_ref, qaug_ref, k_ref, v_ref, kmean_ref, kamax_ref, slope_ref, causal_ref,
                 o_ref, sel_ref, sa_ref, sb_ref, acc_ref):
    last = pl.program_id(2)
    nb = k_ref.shape[1]
    nidx = lax.broadcasted_iota(jnp.int32, (nb, MOBA_Q), 0)
    blk = last * MOBA_G + (lax.broadcasted_iota(jnp.int32, (nb, MOBA_Q), 1)
                           >> (KV_BLOCK.bit_length() - 1))
    blk_row = blk[0:1, :]
    neg_inf = jnp.float32(-jnp.inf)
    slope = [slope_ref[hh, 0:1, :] for hh in range(2)]
    lanes = [slice(hh * PAIR, (hh + 1) * PAIR) for hh in range(2)]
    vrows = [slice(hh * V_ROWS, (hh + 1) * V_ROWS) for hh in range(2)]

    heads, reach = [], []
    for hh in range(2):
        qm = jnp.concatenate([q_ref[0, hh * HEAD_DIM:(hh + 1) * HEAD_DIM, :], qaug_ref[hh]], axis=0)
        km = kmean_ref[0, :, lanes[hh]].astype(_BF16)
        kamax = kamax_ref[0, :, lanes[hh]].astype(_BF16)
        bound = jnp.dot(kamax, jnp.abs(qm), preferred_element_type=_F32)
        reach.append(bound + slope[hh] * ((nidx + 1 - blk) * KV_BLOCK).astype(_F32))
        gate = jnp.dot(km, qm, preferred_element_type=_F32)
        g = jnp.where(nidx < blk, gate, neg_inf)
        sel = nidx == blk
        for r in range(MOBA_TOPK):
            mx = jnp.max(g, axis=0, keepdims=True)
            first = jnp.min(jnp.where(g == mx, nidx, nb), axis=0, keepdims=True)
            hit = nidx == first
            sel = sel | (hit & (jnp.full((nb, MOBA_Q), r, jnp.int32) < blk))
            g = jnp.where(hit, neg_inf, g)
        sel_ref[hh] = jnp.where(sel, 0.0, neg_inf)
        heads.append(qm)

    def shift(j):
        return ((j - blk_row) * KV_BLOCK).astype(_F32)

    def row_bias(hh, j):
        return sel_ref[hh, pl.ds(j, 1), :] + slope[hh] * shift(j)

    def scores(t, dst_ref, nearest=False):
        mx = []
        for hh in range(2):
            col = jnp.full((1, MOBA_Q), neg_inf, _F32)
            for g in range(MOBA_G):
                j = t * MOBA_G + g
                s = jnp.dot(k_ref[0, j, :, lanes[hh]], heads[hh], preferred_element_type=_F32)
                if nearest:
                    s = s + causal_ref[g]
                dst_ref[hh, g * KV_BLOCK:(g + 1) * KV_BLOCK, :] = s
                col = jnp.maximum(col, jnp.max(s, axis=0, keepdims=True) + row_bias(hh, j))
            mx.append(col)
        return mx

    def accumulate(t, src_ref, mx, st):
        out = []
        for hh in range(2):
            m_new = jnp.maximum(st[hh], mx[hh])
            acc = jnp.exp2(st[hh] - m_new) * acc_ref[hh]
            for g in range(MOBA_G):
                j = t * MOBA_G + g
                s = src_ref[hh, g * KV_BLOCK:(g + 1) * KV_BLOCK, :]
                p = jnp.exp2(s - (m_new - row_bias(hh, j)))
                acc = acc + jnp.dot(v_ref[0, j, vrows[hh], :], p.astype(_BF16),
                                    preferred_element_type=_F32)
            acc_ref[hh] = acc
            out.append(m_new)
        return out

    state = [jnp.full((1, MOBA_Q), jnp.finfo(_F32).min, _F32) for _ in range(2)]
    for hh in range(2):
        acc_ref[hh] = jnp.zeros((V_ROWS, MOBA_Q), _F32)

    mx_a = scores(last, sa_ref, nearest=True)
    mx_b = scores(jnp.maximum(last - 1, 0), sb_ref)
    first_needed = jnp.int32(nb)
    for hh in range(2):
        needed = (nidx < blk) & (reach[hh] - mx_a[hh] > MOBA_SKIP)
        first_needed = jnp.minimum(first_needed, jnp.min(jnp.where(needed, nidx, nb)))
    first = jnp.minimum(first_needed // MOBA_G, last)
    ntiles = last - first + 1
    state = accumulate(last, sa_ref, mx_a, state)

    def body(k, carry):
        st, mx_b = carry[:2], carry[2:]
        t = last - 1 - 2 * k
        mx_a = scores(t - 1, sa_ref)
        st = accumulate(t, sb_ref, mx_b, st)
        mx_b = scores(jnp.maximum(t - 2, 0), sb_ref)
        st = accumulate(t - 1, sa_ref, mx_a, st)
        return tuple(st) + tuple(mx_b)

    carry = lax.fori_loop(0, (ntiles - 1) // 2, body, tuple(state) + tuple(mx_b))

    @pl.when(ntiles % 2 == 0)
    def _():
        accumulate(first, sb_ref, carry[2:], carry[:2])

    o_ref[0] = jnp.concatenate(
        [acc_ref[hh, 0:HEAD_DIM, :] / acc_ref[hh, HEAD_DIM:HEAD_DIM + 1, :] for hh in range(2)],
        axis=0).astype(_BF16)


def _moba(qa_t, qaug, ka, va_t, kmean, kamax, slopes, causal):
    b, _, s = qa_t.shape
    nb = s // KV_BLOCK
    npair = N_HEADS // 2
    assert nb % MOBA_G == 0
    return pl.pallas_call(
        _moba_kernel,
        grid=(b, npair, s // MOBA_Q),
        in_specs=[
            pl.BlockSpec((1, PAIR, MOBA_Q), lambda bi, hp, c: (bi, hp, c)),
            pl.BlockSpec((2, HEAD_DIM, MOBA_Q), lambda bi, hp, c: (hp, 0, 0)),
            pl.BlockSpec((1, nb, KV_BLOCK, 2 * PAIR), lambda bi, hp, c: (bi, 0, 0, hp)),
            pl.BlockSpec((1, nb, 2 * V_ROWS, KV_BLOCK), lambda bi, hp, c: (bi, 0, hp, 0)),
            pl.BlockSpec((1, nb, 2 * PAIR), lambda bi, hp, c: (bi, 0, hp)),
            pl.BlockSpec((1, nb, 2 * PAIR), lambda bi, hp, c: (bi, 0, hp)),
            pl.BlockSpec((2, SUBLANES, MOBA_Q), lambda bi, hp, c: (hp, 0, 0)),
            _const_spec((MOBA_G, KV_BLOCK, MOBA_Q)),
        ],
        out_specs=pl.BlockSpec((1, PAIR, MOBA_Q), lambda bi, hp, c: (bi, hp, c)),
        out_shape=jax.ShapeDtypeStruct((b, WIDTH, s), _BF16),
        scratch_shapes=[pltpu.VMEM((2, nb, MOBA_Q), _F32),
                        pltpu.VMEM((2, MOBA_G * KV_BLOCK, MOBA_Q), _F32),
                        pltpu.VMEM((2, MOBA_G * KV_BLOCK, MOBA_Q), _F32),
                        pltpu.VMEM((2, V_ROWS, MOBA_Q), _F32)],
        compiler_params=_params("arbitrary", "arbitrary", "arbitrary"),
        name="moba",
    )(qa_t, qaug, ka, va_t, kmean, kamax, slopes, causal)


def _suffix_product_over_sublanes(x):
    sub = lax.broadcasted_iota(jnp.int32, x.shape, 0)
    y = x
    for d in (1, 2, 4):
        up = pltpu.roll(y, SUBLANES - d, 0)
        y = y * jnp.where(sub < SUBLANES - d, up, 1.0)
    return y


def _sb_scores(k_j, qm, mask):
    z = jnp.dot(k_j, qm, preferred_element_type=_F32)
    keep = 1.0 / (1.0 + jnp.exp2(z))
    if mask is not None:
        keep = jnp.where(mask, keep, 1.0)
    run = jnp.ones((SUBLANES, z.shape[1]), _F32)
    diff = [None] * RUN
    for r in reversed(range(RUN)):
        nxt = run * keep[r * SUBLANES:(r + 1) * SUBLANES, :]
        diff[r] = run - nxt
        run = nxt
    sub = lax.broadcasted_iota(jnp.int32, run.shape, 0)
    shifted = jnp.where(sub < SUBLANES - 1, pltpu.roll(run, SUBLANES - 1, 0), 1.0)
    return diff, run, _suffix_product_over_sublanes(shifted)


def _sb_weights(scores, v_jh, carry, acc):
    diff, run, later = scores
    base = later * carry
    w = jnp.concatenate([d * base for d in diff], axis=0)
    acc = acc + jnp.dot(v_jh, w.astype(_BF16), preferred_element_type=_F32)
    return carry * (later[0:1, :] * run[0:1, :]), acc


def _sb_kernel(q_ref, k_ref, v_ref, o_ref):
    rho = lax.broadcasted_iota(jnp.int32, (KV_BLOCK, SB_Q), 0)
    kpos = (rho & (SUBLANES - 1)) * RUN + (rho >> (SUBLANES.bit_length() - 1))
    qpos = lax.broadcasted_iota(jnp.int32, (KV_BLOCK, SB_Q), 1)
    strict = kpos < qpos
    rows = [slice(hh * HEAD_DIM, (hh + 1) * HEAD_DIM) for hh in range(2)]

    jds = [pl.program_id(2) * SB_TILES + u for u in range(SB_TILES)]
    jps = [jnp.maximum(jd - 1, 0) for jd in jds]
    heads = [[_head_rows(q_ref[0, :, u * SB_Q:(u + 1) * SB_Q], hh) for hh in range(2)]
             for u in range(SB_TILES)]
    diags = [[_sb_scores(k_ref[0, jds[u]], heads[u][hh], strict) for hh in range(2)]
             for u in range(SB_TILES)]
    prevs = [[_sb_scores(k_ref[0, jps[u]], heads[u][hh], None) for hh in range(2)]
             for u in range(SB_TILES)]
    states = []
    for u in range(SB_TILES):
        state = []
        for hh in range(2):
            carry, acc = _sb_weights(diags[u][hh], v_ref[0, jds[u], rows[hh], :],
                                     jnp.ones((1, SB_Q), _F32),
                                     jnp.zeros((HEAD_DIM, SB_Q), _F32))
            carry = jnp.where(jds[u] > 0, carry, 0.0)
            state += list(_sb_weights(prevs[u][hh], v_ref[0, jps[u], rows[hh], :], carry, acc))
        states.append(state)

    def more(trip, sts):
        flags = [jnp.logical_and(jds[u] - 2 - trip >= 0,
                                 jnp.maximum(jnp.max(sts[u][0]), jnp.max(sts[u][2])) > SB_STOP)
                 for u in range(SB_TILES)]
        return functools.reduce(jnp.logical_or, flags).astype(jnp.int32)

    def body(loop):
        trip, flat = loop[0], loop[2:]
        new = []
        for u in range(SB_TILES):
            j = jds[u] - 2 - trip
            jc = jnp.maximum(j, 0)
            out = []
            for hh in range(2):
                carry, acc = flat[4 * u + 2 * hh:4 * u + 2 * hh + 2]
                carry = jnp.where(j >= 0, carry, 0.0)
                out += list(_sb_weights(_sb_scores(k_ref[0, jc], heads[u][hh], None),
                                        v_ref[0, jc, rows[hh], :], carry, acc))
            new.append(out)
        return (trip + 1, more(trip + 1, new), *[a for st in new for a in st])

    zero = jnp.int32(0)
    final = lax.while_loop(lambda loop: loop[1] > 0, body,
                           (zero, more(zero, states), *[a for st in states for a in st]))[2:]
    for u in range(SB_TILES):
        o_ref[0, :, u * SB_Q:(u + 1) * SB_Q] = jnp.concatenate(
            [final[4 * u + 1], final[4 * u + 3]], axis=0).astype(_BF16)


def _stick_breaking(qs_t, ks, vs_t):
    b, _, s = qs_t.shape
    nb = s // KV_BLOCK
    npair = N_HEADS // 2
    return pl.pallas_call(
        _sb_kernel,
        grid=(b, npair, s // (SB_Q * SB_TILES)),
        in_specs=[
            pl.BlockSpec((1, PAIR, SB_Q * SB_TILES), lambda bi, hp, i: (bi, hp, i)),
            pl.BlockSpec((1, nb, KV_BLOCK, PAIR), lambda bi, hp, i: (bi, 0, 0, hp)),
            pl.BlockSpec((1, nb, PAIR, KV_BLOCK), lambda bi, hp, i: (bi, 0, hp, 0)),
        ],
        out_specs=pl.BlockSpec((1, PAIR, SB_Q * SB_TILES), lambda bi, hp, i: (bi, hp, i)),
        out_shape=jax.ShapeDtypeStruct((b, WIDTH, s), _BF16),
        compiler_params=_params("arbitrary", "arbitrary", "arbitrary"),
        name="stick_breaking",
    )(qs_t, ks, vs_t)


MIX_T = 1024
MIX_CHUNK = 256
HALO = SUBLANES
GATE_SPLIT = 3 * WIDTH


def _mix_kernel(oa_ref, ob_ref, halo_ref, x_ref, g_ref, wconv_ref, wg0_ref, wg1_ref, convw_ref,
                bg_ref, wpa_ref, wpb_ref, wpc_ref, wout_ref, o_ref, merged_ref):
    i = pl.program_id(1)
    h = _rms(x_ref[0], g_ref[0]).astype(_BF16)
    r = jnp.dot(h, wconv_ref[0], preferred_element_type=_F32)
    xc, bc, cc = r[:, 0:WIDTH], r[:, WIDTH:2 * WIDTH], r[:, 2 * WIDTH:3 * WIDTH]
    u = cc * xc
    hl = jnp.dot(_rms(halo_ref[0], g_ref[0]).astype(_BF16), wconv_ref[0],
                 preferred_element_type=_F32)
    uh = hl[:, 2 * WIDTH:3 * WIDTH] * hl[:, 0:WIDTH]
    uh = jnp.where(i > 0, uh, 0.0)
    prev1 = uh[HALO - 1:HALO, :]
    prev2 = uh[HALO - 2:HALO - 1, :]
    row = lax.broadcasted_iota(jnp.int32, u.shape, 0)
    u1 = jnp.where(row == 0, prev1, pltpu.roll(u, 1, 0))
    u2 = jnp.where(row == 0, prev2, jnp.where(row == 1, prev1, pltpu.roll(u, 2, 0)))
    w = convw_ref[0]
    y = (bc * (u2 * w[0:1, :] + u1 * w[1:2, :] + u * w[2:3, :])).astype(_BF16)
    oa, ob = oa_ref[0], ob_ref[0]
    d = D_MODEL
    for c in range(d // MIX_CHUNK):
        cols = slice(c * MIX_CHUNK, (c + 1) * MIX_CHUNK)

        def gate(k):
            lo = k * d + c * MIX_CHUNK
            w_ref, at = (wg0_ref, lo) if lo < GATE_SPLIT else (wg1_ref, lo - GATE_SPLIT)
            pre = jnp.dot(h, w_ref[0, :, at:at + MIX_CHUNK], preferred_element_type=_F32)
            return 1.0 / (1.0 + jnp.exp2(pre + bg_ref[0, :, lo:lo + MIX_CHUNK]))

        merged_ref[:, cols] = (
            gate(0) * lax.dot_general(oa, wpa_ref[0, :, cols], _TN, preferred_element_type=_F32)
            + gate(1) * lax.dot_general(ob, wpb_ref[0, :, cols], _TN,
                                        preferred_element_type=_F32)
            + gate(2) * jnp.dot(y, wpc_ref[0, :, cols], preferred_element_type=_F32)
        ).astype(_BF16)
    o_ref[0] = x_ref[0] + jnp.dot(merged_ref[...], wout_ref[0], preferred_element_type=_F32)


def _mix(layer, oa, ob, x, g, w_all, convw, bg, wpa, wpb, wpc, wout):
    b, s, d = x.shape
    t = MIX_T
    tile = lambda width: pl.BlockSpec((1, t, width), lambda bi, i: (bi, i, 0))
    tile_t = pl.BlockSpec((1, WIDTH, t), lambda bi, i: (bi, 0, i))
    halo = pl.BlockSpec((1, HALO, d), lambda bi, i: (bi, jnp.maximum(i * (t // HALO) - 1, 0), 0))
    w_block = lambda n: pl.BlockSpec((1, d, GATE_SPLIT), lambda *_: (layer, 0, n),
                                     pipeline_mode=pl.Buffered(1))
    assert 6 * WIDTH == 2 * GATE_SPLIT and 3 * d == 2 * GATE_SPLIT
    return pl.pallas_call(
        _mix_kernel,
        grid=(b, s // t),
        in_specs=[tile_t, tile_t, halo, tile(d), _layer_spec(layer, (1, d)),
                  w_block(2), w_block(3), w_block(4),
                  _layer_spec(layer, (CONV_K, WIDTH)), _layer_spec(layer, (1, 3 * d)),
                  _layer_spec(layer, (WIDTH, d)), _layer_spec(layer, (WIDTH, d)),
                  _layer_spec(layer, (WIDTH, d)), _layer_spec(layer, (d, d))],
        out_specs=tile(d),
        out_shape=jax.ShapeDtypeStruct((b, s, d), _F32),
        scratch_shapes=[pltpu.VMEM((t, d), _BF16)],
        compiler_params=_params("arbitrary", "arbitrary"),
        name="mix",
    )(oa, ob, x, x, g, w_all, w_all, w_all, convw, bg, wpa, wpb, wpc, wout)


FFN_T = 1024
FF_CHUNK = 256


def _ffn_up_kernel(x_ref, g_ref, wg_ref, wu_ref, o_ref):
    h = _rms(x_ref[...], g_ref[0]).astype(_BF16)
    for n in range(D_FF // FF_CHUNK):
        cols = slice(n * FF_CHUNK, (n + 1) * FF_CHUNK)
        a = jnp.dot(h, wg_ref[0, :, cols], preferred_element_type=_F32)
        bgate = jnp.dot(h, wu_ref[0, :, cols], preferred_element_type=_F32)
        o_ref[:, cols] = (a * jax.nn.sigmoid(a) * bgate).astype(_BF16)


def _ffn_down_kernel(a_ref, x_ref, wd_ref, g_ref, o_ref, *, final_norm):
    y = x_ref[...] + jnp.dot(a_ref[...], wd_ref[0], preferred_element_type=_F32)
    if final_norm:
        y = _rms(y, g_ref[...])
    o_ref[...] = y


def _ffn(layer, x2d, g, wg, wu, wd, gfinal, final_norm):
    n, d = x2d.shape
    t = FFN_T
    act = pl.pallas_call(
        _ffn_up_kernel,
        grid=(n // t,),
        in_specs=[pl.BlockSpec((t, d), lambda i: (i, 0)), _layer_spec(layer, (1, d)),
                  _layer_spec(layer, (d, D_FF)), _layer_spec(layer, (d, D_FF))],
        out_specs=pl.BlockSpec((t, D_FF), lambda i: (i, 0)),
        out_shape=jax.ShapeDtypeStruct((n, D_FF), _BF16),
        compiler_params=_params("arbitrary"),
        name="ffn_up",
    )(x2d, g, wg, wu)
    return pl.pallas_call(
        functools.partial(_ffn_down_kernel, final_norm=final_norm),
        grid=(n // t,),
        in_specs=[pl.BlockSpec((t, D_FF), lambda i: (i, 0)),
                  pl.BlockSpec((t, d), lambda i: (i, 0)),
                  _layer_spec(layer, (D_FF, d)), _const_spec((1, d))],
        out_specs=pl.BlockSpec((t, d), lambda i: (i, 0)),
        out_shape=jax.ShapeDtypeStruct((n, d), _F32),
        compiler_params=_params("arbitrary"),
        name="ffn_down",
    )(act, x2d, wd, gfinal)


def _sb_permutation():
    rho = np.arange(KV_BLOCK)
    kappa = (rho % SUBLANES) * RUN + rho // SUBLANES
    p = np.zeros((KV_BLOCK, KV_BLOCK), np.float32)
    p[rho, kappa] = 1.0
    return jnp.asarray(p, _BF16)


def _moba_tables():
    slopes = np.exp2(-8.0 * (np.arange(N_HEADS, dtype=np.float64) + 1.0) / N_HEADS) * LOG2E
    kpos = np.zeros((KV_BLOCK, N_HEADS * PAIR), np.float32)
    qaug = np.zeros((N_HEADS, HEAD_DIM, MOBA_Q), np.float32)
    for hd in range(N_HEADS):
        rest = np.float64(slopes[hd])
        for a in range(N_AUG):
            piece = np.float64(np.float32(rest).astype(_BF16))
            kpos[:, hd * PAIR + HEAD_DIM + a] = np.arange(KV_BLOCK)
            qaug[hd, a, :] = piece
            rest = rest - piece
    key = np.arange(KV_BLOCK)[:, None]
    q_blk, q_off = np.divmod(np.arange(MOBA_Q)[None, :], KV_BLOCK)
    causal = np.stack([np.where((q_blk > g) | ((q_blk == g) & (key <= q_off)), 0.0, -np.inf)
                       for g in range(MOBA_G)])
    slope_tab = slopes[:, None, None] * np.ones((1, SUBLANES, MOBA_Q))
    return (jnp.asarray(kpos), jnp.asarray(qaug, _BF16),
            jnp.asarray(slope_tab.astype(np.float32)), jnp.asarray(causal.astype(np.float32)))


def kernel(x, norm_mix_g, w_in, b_gate, conv_w, w_proj_moba, w_proj_sb, w_proj_conv, w_out,
           norm_ffn_g, w_ffn_gate, w_ffn_up, w_ffn_down, norm_final_g):
    depth = w_in.shape[0]
    b, s, d = x.shape
    scale = HEAD_DIM ** -0.5
    perm = _sb_permutation()
    kpos, qaug, slopes, causal = _moba_tables()
    col_scale = np.ones((w_in.shape[2],), np.float32)
    col_scale[0:WIDTH] = scale * LOG2E
    col_scale[3 * WIDTH:4 * WIDTH] = scale * LOG2E
    col_scale[9 * WIDTH:] = -LOG2E
    w_all = (w_in * col_scale).astype(_BF16)
    bf16 = lambda p: p.astype(_BF16)
    wpa, wpb, wpc, wout = bf16(w_proj_moba), bf16(w_proj_sb), bf16(w_proj_conv), bf16(w_out)
    wg, wu, wd = bf16(w_ffn_gate), bf16(w_ffn_up), bf16(w_ffn_down)
    g_mix, g_ffn, bg = norm_mix_g[:, None, :], norm_ffn_g[:, None, :], b_gate[:, None, :] * -LOG2E
    for l in range(depth):
        qa_t, ka, va_t, kmean, kamax, qs_t, ks, vs_t = _in_proj(l, x, g_mix, perm, kpos, w_all)
        oa = _moba(qa_t, qaug, ka, va_t, kmean, kamax, slopes, causal)
        ob = _stick_breaking(qs_t, ks, vs_t)
        x = _mix(l, oa, ob, x, g_mix, w_all, conv_w, bg, wpa, wpb, wpc, wout)
        x = _ffn(l, x.reshape(b * s, d), g_ffn, wg, wu, wd, norm_final_g[None, :],
                 final_norm=(l == depth - 1)).reshape(b, s, d)
    return x
```
